```python
import jax, jax.numpy as jnp
from jax import lax
import numpy as np

D_MODEL = 2048
BATCH = 4
SEQ = 8192
DEPTH = 4

HEAD_DIM = 64
D_MIX = D_MODEL
GROUP_WIDTH = D_MIX // 4
A_HEADS = GROUP_WIDTH // HEAD_DIM
DILATED_PATTERNS = ((128, 1), (512, 4), (2048, 16))
B_HEADS = GROUP_WIDTH // HEAD_DIM
B_DECAY_RANK = 32
B_ICLR_RANK = 32
B_GATE_RANK = 96
LNX_EPS = 64e-5
C_Q_HEADS = GROUP_WIDTH // HEAD_DIM
C_KV_HEADS = C_Q_HEADS // 4
C_HALF_WINDOW = 128
D_HEAD_K = 128
D_HEAD_V = 128
D_HEADS = GROUP_WIDTH // D_HEAD_K
D_CHUNK = 64
D_FF = 5632
ROPE_THETA = 500000.0
ROT_DIM = HEAD_DIM // 4
NORM_EPS = 1e-6
NEG_INF = -1e30
D_IN_SIZES = (GROUP_WIDTH,) * 3 + (GROUP_WIDTH,) * 3 + (C_Q_HEADS * HEAD_DIM, C_KV_HEADS * HEAD_DIM, C_KV_HEADS * HEAD_DIM) + (GROUP_WIDTH,) * 5
D_IN = sum(D_IN_SIZES)
SPLIT_POINTS = tuple(int(c) for c in np.cumsum(D_IN_SIZES)[:-1])

kernel_name = 'hybrid_parallel_heads_bidir_encoder'


def rmsnorm(x, g, eps=NORM_EPS):
    xf = x.astype(jnp.float32)
    y = xf * lax.rsqrt(jnp.mean(xf * xf, axis=-1, keepdims=True) + eps)
    return (y * g).astype(x.dtype)


def swiglu(h, w_in, w_out):
    gate, up = jnp.split(h @ w_in, 2, axis=-1)
    return (jax.nn.silu(gate) * up) @ w_out


def rope_tables(seq):
    inv = ROPE_THETA ** (-jnp.arange(0, ROT_DIM, 2, dtype=jnp.float32) / ROT_DIM)
    ang = jnp.arange(seq, dtype=jnp.float32)[:, None] * inv[None]
    return jnp.cos(ang), jnp.sin(ang)


def qk_norm_rope(t, g, cos, sin):
    t = rmsnorm(t, g).astype(jnp.float32)
    half = ROT_DIM // 2
    x1, x2, rest = t[..., :half], t[..., half:ROT_DIM], t[..., ROT_DIM:]
    c, s = cos[None, :, None, :], sin[None, :, None, :]
    return jnp.concatenate([x1 * c - x2 * s, x2 * c + x1 * s, rest], axis=-1)


def banded_attention(q, k, v, half, blk):
    B, L, Hk, G, D = q.shape
    nb = -(-L // blk)
    pad = nb * blk - L
    qb = jnp.pad(q, ((0, 0), (0, pad), (0, 0), (0, 0), (0, 0))).reshape(B, nb, blk, Hk, G, D)

    def windows(t):
        t = jnp.pad(t, ((0, 0), (blk, blk + pad), (0, 0), (0, 0))).reshape(B, nb + 2, blk, Hk, D)
        return jnp.concatenate([t[:, :-2], t[:, 1:-1], t[:, 2:]], axis=2)

    kw, vw = windows(k), windows(v)
    s = jnp.einsum('bnqhgd,bnkhd->bnhgqk', qb, kw).astype(jnp.float32)
    start = jnp.arange(nb)[:, None] * blk
    qpos = start + jnp.arange(blk)[None]
    kpos = start - blk + jnp.arange(3 * blk)[None]
    valid = ((jnp.abs(qpos[:, :, None] - kpos[:, None, :]) <= half)
             & (kpos[:, None, :] >= 0) & (kpos[:, None, :] < L))
    s = jnp.where(valid[None, :, None, None], s, NEG_INF)
    m = jnp.max(s, axis=-1)
    p = jnp.exp(s - m[..., None])
    l = jnp.sum(p, axis=-1)
    acc = jnp.einsum('bnhgqk,bnkhd->bnqhgd', p, vw.astype(jnp.float32))
    m = m.transpose(0, 1, 4, 2, 3).reshape(B, nb * blk, Hk, G)[:, :L]
    l = l.transpose(0, 1, 4, 2, 3).reshape(B, nb * blk, Hk, G)[:, :L]
    acc = acc.reshape(B, nb * blk, Hk, G, D)[:, :L]
    return m, l, acc


def _stride_split(t, d):
    B, S = t.shape[:2]
    return t.reshape(B, S // d, d, *t.shape[2:]).swapaxes(1, 2).reshape(B * d, S // d, *t.shape[2:])


def _stride_merge(t, d):
    Bd, L = t.shape[:2]
    return t.reshape(Bd // d, d, L, *t.shape[2:]).swapaxes(1, 2).reshape(Bd // d, L * d, *t.shape[2:])


def dilated_attention(q, k, v):
    ms, ls, accs = [], [], []
    for window, dil in DILATED_PATTERNS:
        half = window // (2 * dil)
        m, l, acc = banded_attention(_stride_split(q[:, :, :, None], dil), _stride_split(k, dil),
                                     _stride_split(v, dil), half, half)
        ms.append(_stride_merge(m, dil))
        ls.append(_stride_merge(l, dil))
        accs.append(_stride_merge(acc, dil))
    m, l, acc = jnp.stack(ms), jnp.stack(ls), jnp.stack(accs)
    m_max = jnp.max(m, axis=0)
    w = jnp.exp(m - m_max)
    out = jnp.sum(acc * w[..., None], axis=0) / jnp.sum(l * w, axis=0)[..., None]
    return out[:, :, :, 0]


def sink_window_attention(q, k, v, sink):
    B, S, Hq, D = q.shape
    Hk = k.shape[2]
    G = Hq // Hk
    m, l, acc = banded_attention(q.reshape(B, S, Hk, G, D), k, v, C_HALF_WINDOW, C_HALF_WINDOW)
    s = sink.reshape(Hk, G).astype(jnp.float32)
    m_all = jnp.maximum(m, s)
    sc = jnp.exp(m - m_all)
    den = l * sc + jnp.exp(s - m_all)
    return (acc * (sc / den)[..., None]).reshape(B, S, Hq, D)


def _shift(t):
    pad = [(0, 0), (1, 0)] + [(0, 0)] * (t.ndim - 2)
    return jnp.pad(t, pad)[:, :-1]


def rwkv7_scan(r, w, k, v, a, b):
    B, S, H, N = r.shape

    def step(state, inp):
        rt, wt, kt, vt, at, bt = inp
        sa = jnp.einsum('bhvk,bhk->bhv', state, at)
        state = state * wt[:, :, None, :] + sa[..., None] * bt[:, :, None, :] + vt[..., None] * kt[:, :, None, :]
        return state, jnp.einsum('bhvk,bhk->bhv', state, rt)

    xs = tuple(t.transpose(1, 0, 2, 3) for t in (r, w, k, v, a, b))
    _, y = lax.scan(step, jnp.zeros((B, H, N, N), jnp.float32), xs)
    return y.transpose(1, 0, 2, 3)


def rwkv7_direction(h, r, k, v, mu_rkv, mu_wa, w0, w1, w2, a0, a1, a2, k_k, k_a, r_k):
    B, S, _ = h.shape
    f32 = jnp.float32
    dh = _shift(h) - h
    hw = h + dh * mu_wa[0]
    ha = h + dh * mu_wa[1]
    r, k, v = (t + (_shift(t) - t) * mu for t, mu in zip((r, k, v), mu_rkv))
    w_log = -jax.nn.softplus(-(w0 + jnp.tanh(hw @ w1) @ w2).astype(f32)) - 0.5
    decay = jnp.exp(-jnp.exp(w_log))
    a = jax.nn.sigmoid((a0 + (ha @ a1) @ a2).astype(f32))
    heads = lambda t: t.astype(f32).reshape(B, S, B_HEADS, HEAD_DIM)
    kk = heads(k * k_k)
    kk = kk / jnp.maximum(jnp.sqrt(jnp.sum(kk * kk, axis=-1, keepdims=True)), 1e-12)
    kx = heads(k.astype(f32) * (1.0 + (a - 1.0) * k_a))
    rh, vh, ah = heads(r), heads(v), heads(a)
    y = rwkv7_scan(rh, heads(decay), kx, vh, -kk, kk * ah)
    bonus = jnp.sum(rh * kx * r_k, axis=-1, keepdims=True) * vh
    return y, bonus


def rwkv7_mixer(h, r, k, v, mu_rkv, mu_wa, w0, w1, w2, a0, a1, a2, k_k, k_a, r_k,
                lnx_gain, lnx_bias, g1, g2):
    B, S, _ = h.shape
    y_sum, bonus_sum = 0.0, 0.0
    for d in range(2):
        orient = (lambda t: jnp.flip(t, 1)) if d == 1 else (lambda t: t)
        y, bonus = rwkv7_direction(orient(h), orient(r), orient(k), orient(v), mu_rkv[d], mu_wa[d],
                                   w0[d], w1[d], w2[d], a0[d], a1[d], a2[d], k_k, k_a, r_k)
        y_sum = y_sum + orient(y)
        bonus_sum = bonus_sum + orient(bonus)
    mu = jnp.mean(y_sum, axis=-1, keepdims=True)
    var = jnp.mean(jnp.square(y_sum - mu), axis=-1, keepdims=True)
    y = ((y_sum - mu) * lax.rsqrt(var + LNX_EPS)).reshape(B, S, GROUP_WIDTH) * lnx_gain + lnx_bias
    gate = jax.nn.sigmoid(h @ g1) @ g2
    return (y + bonus_sum.reshape(B, S, GROUP_WIDTH)) * gate


def hgrn2_chunk_scan(q, logf, k, v):
    B, S, H, K = q.shape
    V = v.shape[-1]
    n = S // D_CHUNK

    def chunks(t):
        return t.reshape(B, n, D_CHUNK, H, t.shape[-1]).transpose(1, 0, 3, 2, 4)

    lower = jnp.tril(jnp.ones((D_CHUNK, D_CHUNK), bool))[:, :, None]

    def step(state, inp):
        qc, fc, kc, vc = inp
        b = jnp.cumsum(fc, axis=2)
        rel = jnp.exp(jnp.where(lower, b[:, :, :, None] - b[:, :, None], -jnp.inf))
        scores = jnp.einsum('bhtk,bhsk,bhtsk->bhts', qc, kc, rel)
        o = (jnp.einsum('bhts,bhsv->bhtv', scores, vc)
             + jnp.einsum('bhtk,bhkv->bhtv', qc * jnp.exp(b), state))
        b_end = b[:, :, -1]
        state = (jnp.exp(b_end)[..., None] * state
                 + jnp.einsum('bhsk,bhsv->bhkv', kc * jnp.exp(b_end[:, :, None] - b), vc))
        return state, o

    _, o = lax.scan(step, jnp.zeros((B, H, K, V), jnp.float32), tuple(chunks(t) for t in (q, logf, k, v)))
    return o.transpose(1, 0, 3, 2, 4).reshape(B, S, H, V)


def hgrn2_lower_bounds(logits):
    p = jax.nn.softmax(logits.astype(jnp.float32), axis=0)
    c = jnp.cumsum(p, axis=0)
    return c - c[0:1]


def hgrn2_mixer(q, i, g, f_fwd, f_bwd, lb, norm_gain):
    B, S, _ = q.shape
    f32 = jnp.float32
    heads = lambda t: t.astype(f32).reshape(B, S, D_HEADS, -1)
    qh, vh = heads(q), heads(i)
    o = 0.0
    for d, f_raw in enumerate((f_fwd, f_bwd)):
        logf = jnp.logaddexp(jnp.log(lb[d]), jnp.log1p(-lb[d]) + jax.nn.log_sigmoid(f_raw.astype(f32)))
        kf = -jnp.expm1(logf)
        orient = (lambda t: jnp.flip(t, 1)) if d == 1 else (lambda t: t)
        o = o + orient(hgrn2_chunk_scan(orient(qh), orient(heads(logf)), orient(heads(kf)), orient(vh)))
    o = rmsnorm(o, norm_gain.reshape(D_HEADS, D_HEAD_V))
    return o.reshape(B, S, GROUP_WIDTH) * jax.nn.silu(g.astype(f32))


def token_mixers(h, w_in, w_out, qk_gain, sink, rwkv_params, lb, d_norm_gain, cos, sin):
    B, S, _ = h.shape
    f32 = jnp.float32
    (aq, ak, av, br, bk, bv, cq, ck, cv, dq, di, dg, dff, dfb) = jnp.split(h @ w_in, SPLIT_POINTS, axis=-1)
    heads = lambda t: t.reshape(B, S, -1, HEAD_DIM)
    scale = HEAD_DIM ** -0.5
    qa = qk_norm_rope(heads(aq), qk_gain[0, 0], cos, sin) * scale
    ka = qk_norm_rope(heads(ak), qk_gain[0, 1], cos, sin)
    out_a = dilated_attention(qa, ka, heads(av).astype(f32))
    out_b = rwkv7_mixer(h, br, bk, bv, *rwkv_params)
    qc = qk_norm_rope(heads(cq), qk_gain[1, 0], cos, sin) * scale
    kc = qk_norm_rope(heads(ck), qk_gain[1, 1], cos, sin)
    out_c = sink_window_attention(qc, kc, heads(cv).astype(f32), sink)
    out_d = hgrn2_mixer(dq, di, dg, dff, dfb, lb, d_norm_gain)
    mix = jnp.concatenate([out_a.reshape(B, S, -1), out_b, out_c.reshape(B, S, -1), out_d], axis=-1)
    return mix.astype(h.dtype) @ w_out


def setup_inputs(seed: int = 0) -> dict:
    key = jax.random.key(seed)
    ks = iter(jax.random.split(key, 32))
    nrm = lambda shape, scale: scale * jax.random.normal(next(ks), shape, jnp.float32)
    uni = lambda shape, lo, hi: jax.random.uniform(next(ks), shape, jnp.float32, lo, hi)
    L, GW = DEPTH, GROUP_WIDTH
    return {
        'x': nrm((BATCH, SEQ, D_MODEL), 1.0),
        'ln_gain': 1.0 + nrm((L, 3, D_MODEL), 0.02),
        'w_in': nrm((L, D_MODEL, D_IN), D_MODEL ** -0.5),
        'w_out': nrm((L, D_MIX, D_MODEL), D_MIX ** -0.5),
        'w_ffn_in': nrm((L, 2, D_MODEL, 2 * D_FF), D_MODEL ** -0.5),
        'w_ffn_out': nrm((L, 2, D_FF, D_MODEL), D_FF ** -0.5),
        'qk_gain': 1.0 + nrm((L, 2, 2, HEAD_DIM), 0.02),
        'sink': nrm((L, C_Q_HEADS), 0.5),
        'b_mu_rkv': uni((L, 2, 3, GW), 0.0, 1.0),
        'b_mu_wa': uni((L, 2, 2, D_MODEL), 0.0, 1.0),
        'b_w0': uni((L, 2, GW), -6.0, 1.0),
        'b_w1': nrm((L, 2, D_MODEL, B_DECAY_RANK), D_MODEL ** -0.5),
        'b_w2': nrm((L, 2, B_DECAY_RANK, GW), 0.1 * B_DECAY_RANK ** -0.5),
        'b_a0': nrm((L, 2, GW), 0.1),
        'b_a1': nrm((L, 2, D_MODEL, B_ICLR_RANK), D_MODEL ** -0.5),
        'b_a2': nrm((L, 2, B_ICLR_RANK, GW), 0.1 * B_ICLR_RANK ** -0.5),
        'b_k_k': 0.85 + nrm((L, GW), 0.02),
        'b_k_a': 1.0 + nrm((L, GW), 0.02),
        'b_r_k': nrm((L, B_HEADS, HEAD_DIM), 0.1),
        'b_lnx_gain': 1.0 + nrm((L, GW), 0.02),
        'b_lnx_bias': nrm((L, GW), 0.02),
        'b_g1': nrm((L, D_MODEL, B_GATE_RANK), D_MODEL ** -0.5),
        'b_g2': nrm((L, B_GATE_RANK, GW), B_GATE_RANK ** -0.5),
        'd_lb_logits': nrm((L, 2, GW), 0.5),
        'd_norm_gain': 1.0 + nrm((L, GW), 0.02),
    }


def reference(x, ln_gain, w_in, w_out, w_ffn_in, w_ffn_out, qk_gain, sink,
              b_mu_rkv, b_mu_wa, b_w0, b_w1, b_w2, b_a0, b_a1, b_a2, b_k_k, b_k_a, b_r_k,
              b_lnx_gain, b_lnx_bias, b_g1, b_g2, d_lb_logits, d_norm_gain):
    S = x.shape[1]
    cos, sin = rope_tables(S)
    lb_all = hgrn2_lower_bounds(d_lb_logits)
    for l in range(DEPTH):
        x = x + 0.5 * swiglu(rmsnorm(x, ln_gain[l, 0]), w_ffn_in[l, 0], w_ffn_out[l, 0])
        rwkv_params = (b_mu_rkv[l], b_mu_wa[l], b_w0[l], b_w1[l], b_w2[l], b_a0[l], b_a1[l], b_a2[l],
                       b_k_k[l], b_k_a[l], b_r_k[l], b_lnx_gain[l], b_lnx_bias[l], b_g1[l], b_g2[l])
        x = x + token_mixers(rmsnorm(x, ln_gain[l, 1]), w_in[l], w_out[l], qk_gain[l], sink[l],
                             rwkv_params, lb_all[l], d_norm_gain[l], cos, sin)
        x = x + 0.5 * swiglu(rmsnorm(x, ln_gain[l, 2]), w_ffn_in[l, 1], w_ffn_out[l, 1])
    return x
```

```python
import functools

import jax
import jax.numpy as jnp
from jax import lax
from jax.experimental import pallas as pl
from jax.experimental.pallas import tpu as pltpu

F32 = jnp.float32
BF16 = jnp.bfloat16

D_MODEL = 2048
DEPTH = 4
HEAD_DIM = 64
GROUP_WIDTH = 512
DILATED_PATTERNS = ((128, 1), (512, 4), (2048, 16))
LNX_EPS = 64e-5
C_Q_HEADS = 8
C_KV_HEADS = 2
C_HALF_WINDOW = 128
D_HEAD_K = 128
D_HEADS = 4
D_FF = 5632
ROPE_THETA = 500000.0
ROT_DIM = 16
NORM_EPS = 1e-6
NEG_INF = -1e30

D_IN_EXT = 7168
(CB_AQ, CB_AK, CB_AV, CB_BR, CB_BK, CB_BV, CB_CQ, CB_DQ, CB_DI, CB_DG, CB_DFF, CB_DFB) = range(12)
CB_CK, CB_CV, CB_SECA, CB_SECB, CB_G1 = 48, 49, 50, 51, 52

VMEM_LIMIT = 56 * 1024 * 1024
RWKV_CHUNK = 64
HGRN_CHUNK = 64
HGRN_SUB = 16


def _cparams(sem):
    return pltpu.CompilerParams(dimension_semantics=sem, vmem_limit_bytes=VMEM_LIMIT)


def _split3(x):
    hi = x.astype(BF16)
    r1 = x - hi.astype(F32)
    mid = r1.astype(BF16)
    lo = (r1 - mid.astype(F32)).astype(BF16)
    return hi, mid, lo


def _seg_sum(x, g):
    hi, mid, lo = _split3(x)
    d = lambda a: jnp.dot(a, g, preferred_element_type=F32)
    return d(hi) + d(mid) + d(lo)


def _dot(a, b):
    return jnp.dot(a.astype(BF16), b.astype(BF16), preferred_element_type=F32)


def _dot_nt(a, b):
    return lax.dot_general(a.astype(BF16), b.astype(BF16), (((1,), (1,)), ((), ())),
                           preferred_element_type=F32)


def _dot_tn(a, b):
    return lax.dot_general(a.astype(BF16), b.astype(BF16), (((0,), (0,)), ((), ())),
                           preferred_element_type=F32)


def _split2(x):
    hi = x.astype(BF16)
    lo = (x - hi.astype(F32)).astype(BF16)
    return hi, lo


def _dot_hp(a, b):
    ah, al = _split2(a)
    bh, bl = _split2(b)
    d = lambda x, y: jnp.dot(x, y, preferred_element_type=F32)
    return d(ah, bh) + d(ah, bl) + d(al, bh)


def _softplus(u):
    return jnp.maximum(u, 0.0) + jnp.log(1.0 + jnp.exp(-jnp.abs(u)))


def _scan_cumsum(x, n, reverse):
    row = lax.broadcasted_iota(jnp.int32, x.shape, 0)
    s = 1
    while s < n:
        if reverse:
            x = x + jnp.where(row < n - s, pltpu.roll(x, n - s, 0), 0.0)
        else:
            x = x + jnp.where(row >= s, pltpu.roll(x, s, 0), 0.0)
        s *= 2
    return x


def _scan_shift(x, carry_row, n, reverse):
    row = lax.broadcasted_iota(jnp.int32, x.shape, 0)
    if reverse:
        return jnp.where(row == n - 1, carry_row, pltpu.roll(x, n - 1, 0))
    return jnp.where(row == 0, carry_row, pltpu.roll(x, 1, 0))


def _ffn_kernel(x_ref, g_ref, wg_ref, wu_ref, wo_ref, o_ref, hn_ref):
    @pl.when(pl.program_id(1) == 0)
    def _():
        x = x_ref[...]
        ms = jnp.mean(x * x, axis=-1, keepdims=True)
        hn_ref[...] = (x * lax.rsqrt(ms + NORM_EPS) * g_ref[...]).astype(BF16)
        o_ref[...] = x

    h = hn_ref[...]
    gate = jnp.dot(h, wg_ref[...], preferred_element_type=F32)
    up = jnp.dot(h, wu_ref[...], preferred_element_type=F32)
    act = (0.5 * gate * jax.nn.sigmoid(gate) * up).astype(BF16)
    o_ref[...] += jnp.dot(act, wo_ref[...], preferred_element_type=F32)


def _ffn(x2, gain, w_in, w_out, tm=512, tf=512):
    n, d = x2.shape
    nf = D_FF // tf
    return pl.pallas_call(
        _ffn_kernel,
        grid=(n // tm, nf),
        in_specs=[
            pl.BlockSpec((tm, d), lambda i, j: (i, 0)),
            pl.BlockSpec((1, d), lambda i, j: (0, 0)),
            pl.BlockSpec((d, tf), lambda i, j: (0, j)),
            pl.BlockSpec((d, tf), lambda i, j: (0, j + nf)),
            pl.BlockSpec((tf, d), lambda i, j: (j, 0)),
        ],
        out_specs=pl.BlockSpec((tm, d), lambda i, j: (i, 0)),
        out_shape=jax.ShapeDtypeStruct((n, d), F32),
        scratch_shapes=[pltpu.VMEM((tm, d), BF16)],
        compiler_params=_cparams(("parallel", "arbitrary")),
        name="ffn",
    )(x2, gain, w_in, w_in, w_out)


def _inproj_kernel(x_ref, g_ref, w_ref, o_ref, hn_ref):
    @pl.when(pl.program_id(1) == 0)
    def _():
        x = x_ref[...]
        ms = jnp.mean(x * x, axis=-1, keepdims=True)
        hn_ref[...] = (x * lax.rsqrt(ms + NORM_EPS) * g_ref[...]).astype(BF16)

    o_ref[...] = jnp.dot(hn_ref[...], w_ref[...], preferred_element_type=F32)


def _inproj(x2, gain, w_ext, tm=1024, tn=512):
    n, d = x2.shape
    nc = w_ext.shape[1]
    return pl.pallas_call(
        _inproj_kernel,
        grid=(n // tm, nc // tn),
        in_specs=[
            pl.BlockSpec((tm, d), lambda i, j: (i, 0)),
            pl.BlockSpec((1, d), lambda i, j: (0, 0)),
            pl.BlockSpec((d, tn), lambda i, j: (0, j)),
        ],
        out_specs=pl.BlockSpec((tm, tn), lambda i, j: (i, j)),
        out_shape=jax.ShapeDtypeStruct((n, nc), F32),
        scratch_shapes=[pltpu.VMEM((tm, d), BF16)],
        compiler_params=_cparams(("parallel", "arbitrary")),
        name="inproj",
    )(x2, gain, w_ext)


def _norm_rope(t, gain, g, c, s1, s2, scale):
    w = t.shape[-1]
    ss = _seg_sum(t * t, g)
    y = t * lax.rsqrt(ss * (1.0 / HEAD_DIM) + NORM_EPS) * gain
    half = ROT_DIM // 2
    out = y * c + pltpu.roll(y, w - half, 1) * s1 + pltpu.roll(y, half, 1) * s2
    return out * scale if scale != 1.0 else out


def _prep_kernel(aq_ref, ak_ref, av_ref, cq_ref, ck_ref, cv_ref, c_ref, s1_ref, s2_ref,
                 gain_ref, g_ref, qa_ref, ka_ref, va_ref, qc_ref, kc_ref, vc_ref):
    c1, s11, s21 = c_ref[...], s1_ref[...], s2_ref[...]
    c4 = jnp.concatenate([c1] * 4, axis=1)
    s14 = jnp.concatenate([s11] * 4, axis=1)
    s24 = jnp.concatenate([s21] * 4, axis=1)
    g = g_ref[...]
    g1 = g_ref[0:128, 0:128]
    scale = HEAD_DIM ** -0.5
    qa_ref[...] = _norm_rope(aq_ref[...], gain_ref[0:1, :], g, c4, s14, s24, scale).astype(BF16)
    ka_ref[...] = _norm_rope(ak_ref[...], gain_ref[1:2, :], g, c4, s14, s24, 1.0).astype(BF16)
    va_ref[...] = av_ref[...].astype(BF16)
    qc_ref[...] = _norm_rope(cq_ref[...], gain_ref[2:3, :], g, c4, s14, s24, scale).astype(BF16)
    kc_ref[...] = _norm_rope(ck_ref[...], gain_ref[3:4, 0:128], g1, c1, s11, s21, 1.0).astype(BF16)
    vc_ref[...] = cv_ref[...].astype(BF16)


def _prep(p, rope_c, rope_s1, rope_s2, gains, g512, seq, tm=512):
    n = p.shape[0]
    nseq = seq // tm
    wide = lambda cb: pl.BlockSpec((tm, 512), lambda i, cb=cb: (i, cb))
    narrow = lambda cb: pl.BlockSpec((tm, 128), lambda i, cb=cb: (i, cb))
    tab = pl.BlockSpec((tm, 128), lambda i: (i % nseq, 0))
    o512 = pl.BlockSpec((tm, 512), lambda i: (i, 0))
    o128 = pl.BlockSpec((tm, 128), lambda i: (i, 0))
    return pl.pallas_call(
        _prep_kernel,
        grid=(n // tm,),
        in_specs=[wide(CB_AQ), wide(CB_AK), wide(CB_AV), wide(CB_CQ), narrow(CB_CK), narrow(CB_CV),
                  tab, tab, tab,
                  pl.BlockSpec((8, 512), lambda i: (0, 0)),
                  pl.BlockSpec((512, 512), lambda i: (0, 0))],
        out_specs=[o512, o512, o512, o512, o128, o128],
        out_shape=[jax.ShapeDtypeStruct((n, 512), BF16)] * 4 + [jax.ShapeDtypeStruct((n, 128), BF16)] * 2,
        compiler_params=_cparams(("parallel",)),
        name="prep_qk",
    )(p, p, p, p, p, p, rope_c, rope_s1, rope_s2, gains, g512)


def _band_kernel(*refs, tq, hb, half, length, n_kv, group, with_sink):
    if with_sink:
        q_ref, kp_ref, km_ref, kn_ref, vp_ref, vm_ref, vn_ref, sink_ref, o_ref = refs
    else:
        q_ref, kp_ref, km_ref, kn_ref, vp_ref, vm_ref, vn_ref, o_ref, lse_ref = refs
    i = pl.program_id(2)
    nk = tq + 2 * hb
    row = lax.broadcasted_iota(jnp.int32, (tq, nk), 0)
    col = lax.broadcasted_iota(jnp.int32, (tq, nk), 1)
    kpos = i * tq - hb + col
    valid = (jnp.abs(row + hb - col) <= half) & (kpos >= 0) & (kpos < length)
    for kvh in range(n_kv):
        ks = slice(kvh * HEAD_DIM, (kvh + 1) * HEAD_DIM)
        k = jnp.concatenate([kp_ref[0, :, ks], km_ref[0, :, ks], kn_ref[0, :, ks]], axis=0)
        v = jnp.concatenate([vp_ref[0, :, ks], vm_ref[0, :, ks], vn_ref[0, :, ks]], axis=0)
        for gi in range(group):
            h = kvh * group + gi
            hs = slice(h * HEAD_DIM, (h + 1) * HEAD_DIM)
            q = q_ref[0, :, hs]
            s = lax.dot_general(q, k, (((1,), (1,)), ((), ())), preferred_element_type=F32)
            s = jnp.where(valid, s, NEG_INF)
            m = jnp.max(s, axis=-1, keepdims=True)
            p = jnp.exp(s - m)
            l = jnp.sum(p, axis=-1, keepdims=True)
            acc = jnp.dot(p.astype(BF16), v, preferred_element_type=F32)
            if with_sink:
                sk = sink_ref[0:1, h:h + 1]
                m_all = jnp.maximum(m, sk)
                sc = jnp.exp(m - m_all)
                den = l * sc + jnp.exp(sk - m_all)
                o_ref[0, :, hs] = acc * (sc / den)
            else:
                o_ref[0, :, hs] = acc / l
                lse_ref[0, :, hs] = jnp.broadcast_to(m + jnp.log(l), (tq, HEAD_DIM))


def _band_attention(q, k, v, sink, batch, seq, dil, half, n_kv, group, tq=256):
    wq, wk = q.shape[1], k.shape[1]
    hb = half
    sl = seq // dil
    tq = min(tq, sl)
    nq = sl // tq
    per = tq // hb
    nhb = sl // hb
    qv = q.reshape(batch, sl, dil * wq)
    kv = k.reshape(batch, sl, dil * wk)
    vv = v.reshape(batch, sl, dil * wk)
    main = lambda w: pl.BlockSpec((1, tq, w), lambda b, r, i: (b, i, r))
    prev = pl.BlockSpec((1, hb, wk), lambda b, r, i: (b, jnp.maximum(i * per - 1, 0), r))
    nxt = pl.BlockSpec((1, hb, wk), lambda b, r, i: (b, jnp.minimum((i + 1) * per, nhb - 1), r))
    with_sink = sink is not None
    in_specs = [main(wq), prev, main(wk), nxt, prev, main(wk), nxt]
    args = [qv, kv, kv, kv, vv, vv, vv]
    if with_sink:
        in_specs.append(pl.BlockSpec((1, wq // HEAD_DIM), lambda b, r, i: (0, 0)))
        args.append(sink)
        out_specs = main(wq)
        out_shape = jax.ShapeDtypeStruct(qv.shape, F32)
    else:
        out_specs = [main(wq), main(wq)]
        out_shape = [jax.ShapeDtypeStruct(qv.shape, F32)] * 2
    kern = functools.partial(_band_kernel, tq=tq, hb=hb, half=half, length=sl, n_kv=n_kv,
                             group=group, with_sink=with_sink)
    out = pl.pallas_call(
        kern,
        grid=(batch, dil, nq),
        in_specs=in_specs,
        out_specs=out_specs,
        out_shape=out_shape,
        compiler_params=_cparams(("parallel", "parallel", "parallel")),
        name="band_attn_sink" if with_sink else f"band_attn_d{dil}",
    )(*args)
    if with_sink:
        return out.reshape(batch * seq, wq)
    return out[0].reshape(batch * seq, wq), out[1].reshape(batch * seq, wq)


def _rwkv_kernel(r_ref, k_ref, v_ref, sa_ref, sb_ref, pv_ref, w2_ref, a2_ref, g_ref,
                 y_ref, bonus_ref, state_ref, carry_ref, carryb_ref, *, n, reverse):
    @pl.when(pl.program_id(1) == 0)
    def _():
        state_ref[...] = jnp.zeros_like(state_ref)
        carry_ref[...] = jnp.zeros_like(carry_ref)
        carryb_ref[...] = jnp.zeros_like(carryb_ref)

    last = 0 if reverse else n - 1
    r0, k0, v0, sb = r_ref[...], k_ref[...], v_ref[...], sb_ref[...]
    shift = lambda x, c: _scan_shift(x, c, n, reverse)
    r = r0 + (shift(r0, carry_ref[0:1, :]) - r0) * pv_ref[0:1, :]
    k = k0 + (shift(k0, carry_ref[1:2, :]) - k0) * pv_ref[1:2, :]
    v = v0 + (shift(v0, carry_ref[2:3, :]) - v0) * pv_ref[2:3, :]
    xlr = sa_ref[...] + shift(sb, carryb_ref[0:1, :])
    carry_ref[0:1, :] = r0[last:last + 1, :]
    carry_ref[1:2, :] = k0[last:last + 1, :]
    carry_ref[2:3, :] = v0[last:last + 1, :]
    carryb_ref[0:1, :] = sb[last:last + 1, :]

    wl = pv_ref[3:4, :] + jnp.dot(jnp.tanh(xlr).astype(BF16), w2_ref[...], preferred_element_type=F32)
    w_log = -_softplus(-wl) - 0.5
    ld = -jnp.exp(w_log)
    a = jax.nn.sigmoid(pv_ref[4:5, :] + jnp.dot(xlr.astype(BF16), a2_ref[...], preferred_element_type=F32))
    g = g_ref[...]
    kkr = k * pv_ref[5:6, :]
    kk = kkr / jnp.maximum(jnp.sqrt(_seg_sum(kkr * kkr, g)), 1e-12)
    kx = k * (1.0 + (a - 1.0) * pv_ref[6:7, :])
    bonus_ref[...] = _seg_sum(r * kx * pv_ref[7:8, :], g) * v

    c = _scan_cumsum(ld, n, reverse)
    c_last = c[last:last + 1, :]
    e_c = jnp.exp(c)
    e_cx = jnp.exp(c - ld)
    e_nc = jnp.exp(-c)
    e_end = jnp.exp(c_last - c)
    wc = jnp.exp(c_last)
    kb = kk * a
    rt = r * e_c
    at = -kk * e_cx
    bt = kb * e_nc
    kt = kx * e_nc
    bh = kb * e_end
    kh = kx * e_end

    row = lax.broadcasted_iota(jnp.int32, (n, n), 0)
    col = lax.broadcasted_iota(jnp.int32, (n, n), 1)
    strict = (col > row) if reverse else (col < row)
    incl = (col >= row) if reverse else (col <= row)
    eye = (col == row).astype(F32)

    for h in range(GROUP_WIDTH // HEAD_DIM):
        hs = slice(h * HEAD_DIM, (h + 1) * HEAD_DIM)
        rt_h, at_h, bt_h, kt_h, v_h = rt[:, hs], at[:, hs], bt[:, hs], kt[:, hs], v[:, hs]
        st = state_ref[h]
        a_ab = jnp.where(strict, _dot_hp_nt(at_h, bt_h), 0.0)
        a_ak = jnp.where(strict, _dot_nt(at_h, kt_h), 0.0)
        a_rb = jnp.where(incl, _dot_nt(rt_h, bt_h), 0.0)
        a_rk = jnp.where(incl, _dot_nt(rt_h, kt_h), 0.0)
        t_inv = eye + a_ab
        pw = a_ab
        s = 2
        while s < n:
            pw = _dot_hp(pw, pw)
            t_inv = t_inv + _dot_hp(t_inv, pw)
            s *= 2
        u = _dot_hp(t_inv, _dot_nt(at_h, st) + _dot(a_ak, v_h))
        y_ref[:, hs] = _dot_nt(rt_h, st) + _dot(a_rb, u) + _dot(a_rk, v_h)
        state_ref[h] = st * wc[:, hs] + _dot_tn(u, bh[:, hs]) + _dot_tn(v_h, kh[:, hs])


def _dot_hp_nt(a, b):
    ah, al = _split2(a)
    bh, bl = _split2(b)
    d = lambda x, y: lax.dot_general(x, y, (((1,), (1,)), ((), ())), preferred_element_type=F32)
    return d(ah, bh) + d(ah, bl) + d(al, bh)


def _rwkv(p, pvec, w2pad, a2pad, g512, batch, seq, reverse):
    n = p.shape[0]
    c = RWKV_CHUNK
    nc = seq // c
    if reverse:
        rowidx = lambda b, j: b * nc + (nc - 1 - j)
    else:
        rowidx = lambda b, j: b * nc + j
    wide = lambda cb: pl.BlockSpec((c, 512), lambda b, j, cb=cb: (rowidx(b, j), cb))
    narrow = lambda cb: pl.BlockSpec((c, 128), lambda b, j, cb=cb: (rowidx(b, j), cb))
    const = lambda shape: pl.BlockSpec(shape, lambda b, j: (0,) * len(shape))
    out = pl.BlockSpec((c, 512), lambda b, j: (rowidx(b, j), 0))
    return pl.pallas_call(
        functools.partial(_rwkv_kernel, n=c, reverse=reverse),
        grid=(batch, nc),
        in_specs=[wide(CB_BR), wide(CB_BK), wide(CB_BV), narrow(CB_SECA), narrow(CB_SECB),
                  const((8, 512)), const((128, 512)), const((128, 512)), const((512, 512))],
        out_specs=[out, out],
        out_shape=[jax.ShapeDtypeStruct((n, 512), F32)] * 2,
        scratch_shapes=[pltpu.VMEM((8, HEAD_DIM, HEAD_DIM), F32), pltpu.VMEM((8, 512), F32),
                        pltpu.VMEM((8, 128), F32)],
        compiler_params=_cparams(("parallel", "arbitrary")),
        name="rwkv_bwd" if reverse else "rwkv_fwd",
    )(p, p, p, p, p, pvec, w2pad, a2pad, g512)


def _hgrn_kernel(q_ref, v_ref, z_ref, lb_ref, o_ref, state_ref, *, n, sub, reverse):
    @pl.when(pl.program_id(1) == 0)
    def _():
        state_ref[...] = jnp.zeros_like(state_ref)

    last = 0 if reverse else n - 1
    q, v, z = q_ref[...], v_ref[...], z_ref[...]
    log_sig = -_softplus(-z)
    x1 = lb_ref[0:1, :]
    x2 = lb_ref[1:2, :] + log_sig
    logf = jnp.maximum(x1, x2) + jnp.log(1.0 + jnp.exp(-jnp.abs(x1 - x2)))
    kf = 1.0 - jnp.exp(logf)
    b = _scan_cumsum(logf, n, reverse)
    b_last = b[last:last + 1, :]
    qe = q * jnp.exp(b)
    ke = kf * jnp.exp(b_last - b)
    wend = jnp.exp(b_last)
    nsub = n // sub
    srow = lax.broadcasted_iota(jnp.int32, (sub, 1), 0)
    lane = lax.broadcasted_iota(jnp.int32, (sub, sub), 1)

    for h in range(D_HEADS):
        hs = slice(h * D_HEAD_K, (h + 1) * D_HEAD_K)
        st = state_ref[h]
        q_h, v_h, kf_h, b_h, lf_h = q[:, hs], v[:, hs], kf[:, hs], b[:, hs], logf[:, hs]
        o_inter = _dot_nt(qe[:, hs], st)
        for m in range(nsub):
            ms = slice(m * sub, (m + 1) * sub)
            q_m, b_m, kf_m = q_h[ms], b_h[ms], kf_h[ms]
            first = (m + 1) * sub - 1 if reverse else m * sub
            rho = b_h[first:first + 1] - lf_h[first:first + 1]
            dmat = jnp.zeros((sub, sub), F32)
            for s in range(sub):
                diff = b_m - b_m[s:s + 1]
                ok = (srow <= s) if reverse else (srow >= s)
                w = q_m * (kf_m[s:s + 1] * jnp.exp(jnp.where(ok, diff, NEG_INF)))
                dmat = jnp.where(lane == s, jnp.sum(w, axis=-1, keepdims=True), dmat)
            o_m = o_inter[ms] + _dot(dmat, v_h[ms])
            es = slice((m + 1) * sub, n) if reverse else slice(0, m * sub)
            if es.stop > es.start:
                qt = q_m * jnp.exp(b_m - rho)
                kt = kf_h[es] * jnp.exp(rho - b_h[es])
                o_m = o_m + _dot(_dot_nt(qt, kt), v_h[es])
            o_ref[ms, hs] = o_m
        state_ref[h] = st * wend[:, hs] + _dot_tn(v_h, ke[:, hs])


def _hgrn(p, lbvec, batch, seq, reverse):
    n = p.shape[0]
    c = HGRN_CHUNK
    nc = seq // c
    if reverse:
        rowidx = lambda b, j: b * nc + (nc - 1 - j)
    else:
        rowidx = lambda b, j: b * nc + j
    wide = lambda cb: pl.BlockSpec((c, 512), lambda b, j, cb=cb: (rowidx(b, j), cb))
    return pl.pallas_call(
        functools.partial(_hgrn_kernel, n=c, sub=HGRN_SUB, reverse=reverse),
        grid=(batch, nc),
        in_specs=[wide(CB_DQ), wide(CB_DI), wide(CB_DFB if reverse else CB_DFF),
                  pl.BlockSpec((8, 512), lambda b, j: (0, 0))],
        out_specs=pl.BlockSpec((c, 512), lambda b, j: (rowidx(b, j), 0)),
        out_shape=jax.ShapeDtypeStruct((n, 512), F32),
        scratch_shapes=[pltpu.VMEM((D_HEADS, D_HEAD_K, D_HEAD_K), F32)],
        compiler_params=_cparams(("parallel", "arbitrary")),
        name="hgrn_bwd" if reverse else "hgrn_fwd",
    )(p, p, p, lbvec)


def _mixpost_kernel(o1_ref, l1_ref, o2_ref, l2_ref, o3_ref, l3_ref,
                    y0_ref, y1_ref, bo0_ref, bo1_ref, g1_ref, oc_ref,
                    h0_ref, h1_ref, dg_ref, pv_ref, g_ref, g2_ref, mix_ref):
    l1, l2, l3 = l1_ref[...], l2_ref[...], l3_ref[...]
    mx = jnp.maximum(jnp.maximum(l1, l2), l3)
    w1, w2, w3 = jnp.exp(l1 - mx), jnp.exp(l2 - mx), jnp.exp(l3 - mx)
    out_a = (o1_ref[...] * w1 + o2_ref[...] * w2 + o3_ref[...] * w3) / (w1 + w2 + w3)
    mix_ref[:, 0:512] = out_a.astype(BF16)
    g = g_ref[...]
    y = y0_ref[...] + y1_ref[...]
    mu = _seg_sum(y, g) * (1.0 / HEAD_DIM)
    yc = y - mu
    var = _seg_sum(yc * yc, g) * (1.0 / HEAD_DIM)
    yn = yc * lax.rsqrt(var + LNX_EPS) * pv_ref[0:1, :] + pv_ref[1:2, :]
    gate = jnp.dot(jax.nn.sigmoid(g1_ref[...]).astype(BF16), g2_ref[...], preferred_element_type=F32)
    mix_ref[:, 512:1024] = ((yn + bo0_ref[...] + bo1_ref[...]) * gate).astype(BF16)
    mix_ref[:, 1024:1536] = oc_ref[...].astype(BF16)
    o = h0_ref[...] + h1_ref[...]
    dg = dg_ref[...]
    silu = dg * jax.nn.sigmoid(dg)
    for h in range(D_HEADS):
        hs = slice(h * D_HEAD_K, (h + 1) * D_HEAD_K)
        oh = o[:, hs]
        ms = jnp.mean(oh * oh, axis=-1, keepdims=True)
        res = oh * lax.rsqrt(ms + NORM_EPS) * pv_ref[2:3, hs] * silu[:, hs]
        mix_ref[:, 1536 + h * D_HEAD_K:1536 + (h + 1) * D_HEAD_K] = res.astype(BF16)


def _mixpost(a_parts, yb, bonus, p, oc, od, pvec, g512, g2pad, tm=256):
    n = p.shape[0]
    blk = pl.BlockSpec((tm, 512), lambda i: (i, 0))
    const = lambda shape: pl.BlockSpec(shape, lambda i: (0,) * len(shape))
    args = []
    for o, l in a_parts:
        args += [o, l]
    args += [yb[0], yb[1], bonus[0], bonus[1], p, oc, od[0], od[1], p, pvec, g512, g2pad]
    in_specs = [blk] * 10 + [pl.BlockSpec((tm, 128), lambda i: (i, CB_G1)), blk, blk, blk,
                             pl.BlockSpec((tm, 512), lambda i: (i, CB_DG)),
                             const((8, 512)), const((512, 512)), const((128, 512))]
    return pl.pallas_call(
        _mixpost_kernel,
        grid=(n // tm,),
        in_specs=in_specs,
        out_specs=pl.BlockSpec((tm, D_MODEL), lambda i: (i, 0)),
        out_shape=jax.ShapeDtypeStruct((n, D_MODEL), BF16),
        compiler_params=_cparams(("parallel",)),
        name="mixpost",
    )(*args)


def _outproj_kernel(mix_ref, w_ref, x_ref, o_ref):
    o_ref[...] = x_ref[...] + jnp.dot(mix_ref[...], w_ref[...], preferred_element_type=F32)


def _outproj(mix, w, x2, tm=1024, tn=512):
    n, d = x2.shape
    return pl.pallas_call(
        _outproj_kernel,
        grid=(n // tm, d // tn),
        in_specs=[pl.BlockSpec((tm, mix.shape[1]), lambda i, j: (i, 0)),
                  pl.BlockSpec((mix.shape[1], tn), lambda i, j: (0, j)),
                  pl.BlockSpec((tm, tn), lambda i, j: (i, j))],
        out_specs=pl.BlockSpec((tm, tn), lambda i, j: (i, j)),
        out_shape=jax.ShapeDtypeStruct((n, d), F32),
        compiler_params=_cparams(("parallel", "arbitrary")),
        name="outproj",
    )(mix, w, x2)


def _rope_tables(seq):
    half = ROT_DIM // 2
    inv = ROPE_THETA ** (-jnp.arange(0, ROT_DIM, 2, dtype=F32) / ROT_DIM)
    ang = jnp.arange(seq, dtype=F32)[:, None] * inv[None]
    cos, sin = jnp.cos(ang), jnp.sin(ang)
    ones = jnp.ones((seq, HEAD_DIM - ROT_DIM), F32)
    zeros = jnp.zeros((seq, HEAD_DIM - ROT_DIM), F32)
    z8 = jnp.zeros((seq, half), F32)
    c = jnp.concatenate([cos, cos, ones], axis=1)
    s1 = jnp.concatenate([-sin, z8, zeros], axis=1)
    s2 = jnp.concatenate([z8, sin, zeros], axis=1)
    tile2 = lambda t: jnp.concatenate([t, t], axis=1)
    return tile2(c), tile2(s1), tile2(s2)


def _block_diag_ones(width, seg):
    idx = jnp.arange(width) // seg
    return (idx[:, None] == idx[None, :]).astype(BF16)


def _pad_rows(rows, width=512, total=8):
    rows = [jnp.pad(r.astype(F32).reshape(-1), (0, width - r.size)) for r in rows]
    rows += [jnp.zeros((width,), F32)] * (total - len(rows))
    return jnp.stack(rows)


def _w_ext(w_in, mu_wa, w1, a1, g1):
    cols_a, cols_b = [], []
    for d in range(2):
        for mu, w in ((mu_wa[d, 0], w1[d]), (mu_wa[d, 1], a1[d])):
            cols_a.append((1.0 - mu)[:, None] * w)
            cols_b.append(mu[:, None] * w)
    parts = [w_in[:, 0:3584], w_in[:, 3840:6400], w_in[:, 3584:3840]] + cols_a + cols_b
    parts.append(jnp.pad(g1, ((0, 0), (0, 128 - g1.shape[1]))))
    w = jnp.concatenate(parts, axis=1)
    return jnp.pad(w, ((0, 0), (0, D_IN_EXT - w.shape[1]))).astype(BF16)


def _pad_lowrank(w, row0):
    return jnp.pad(w, ((row0, 128 - row0 - w.shape[0]), (0, 0))).astype(BF16)


def kernel(x, ln_gain, w_in, w_out, w_ffn_in, w_ffn_out, qk_gain, sink, b_mu_rkv, b_mu_wa, b_w0, b_w1,
           b_w2, b_a0, b_a1, b_a2, b_k_k, b_k_a, b_r_k, b_lnx_gain, b_lnx_bias, b_g1, b_g2,
           d_lb_logits, d_norm_gain):
    batch, seq, d = x.shape
    x2 = x.reshape(batch * seq, d)
    rope_c, rope_s1, rope_s2 = _rope_tables(seq)
    g512 = _block_diag_ones(512, HEAD_DIM)
    pr = jax.nn.softmax(d_lb_logits.astype(F32), axis=0)
    cs = jnp.cumsum(pr, axis=0)
    lb_all = cs - cs[0:1]

    for l in range(DEPTH):
        x2 = _ffn(x2, ln_gain[l, 0][None], w_ffn_in[l, 0].astype(BF16), w_ffn_out[l, 0].astype(BF16))

        p = _inproj(x2, ln_gain[l, 1][None], _w_ext(w_in[l], b_mu_wa[l], b_w1[l], b_a1[l], b_g1[l]))

        tile8 = lambda t: jnp.tile(t, 8)
        gains = _pad_rows([tile8(qk_gain[l, 0, 0]), tile8(qk_gain[l, 0, 1]),
                           tile8(qk_gain[l, 1, 0]), jnp.tile(qk_gain[l, 1, 1], 2)])
        qa, ka, va, qc, kc, vc = _prep(p, rope_c, rope_s1, rope_s2, gains, g512, seq)

        a_parts = []
        for window, dil in DILATED_PATTERNS:
            a_parts.append(_band_attention(qa, ka, va, None, batch, seq, dil, window // (2 * dil),
                                           GROUP_WIDTH // HEAD_DIM, 1))
        oc = _band_attention(qc, kc, vc, sink[l][None].astype(F32), batch, seq, 1, C_HALF_WINDOW,
                             C_KV_HEADS, C_Q_HEADS // C_KV_HEADS)

        yb, bonus = [], []
        for dr in range(2):
            pvec = _pad_rows([b_mu_rkv[l, dr, 0], b_mu_rkv[l, dr, 1], b_mu_rkv[l, dr, 2], b_w0[l, dr],
                              b_a0[l, dr], b_k_k[l], b_k_a[l], b_r_k[l]])
            y, bo = _rwkv(p, pvec, _pad_lowrank(b_w2[l, dr], 64 * dr), _pad_lowrank(b_a2[l, dr], 64 * dr + 32),
                          g512, batch, seq, reverse=(dr == 1))
            yb.append(y)
            bonus.append(bo)

        od = []
        for dr in range(2):
            lb = lb_all[l, dr]
            od.append(_hgrn(p, _pad_rows([jnp.log(lb), jnp.log1p(-lb)]), batch, seq, reverse=(dr == 1)))

        pvec = _pad_rows([b_lnx_gain[l], b_lnx_bias[l], d_norm_gain[l]])
        mix = _mixpost(a_parts, yb, bonus, p, oc, od, pvec, g512, _pad_lowrank(b_g2[l], 0))
        x2 = _outproj(mix, w_out[l].astype(BF16), x2)

        x2 = _ffn(x2, ln_gain[l, 2][None], w_ffn_in[l, 1].astype(BF16), w_ffn_out[l, 1].astype(BF16))
    return x2.reshape(batch, seq, d)
```

```python
import functools

import jax
import jax.numpy as jnp
from jax import lax
from jax.experimental import pallas as pl
from jax.experimental.pallas import tpu as pltpu

F32 = jnp.float32
BF16 = jnp.bfloat16

D_MODEL = 2048
DEPTH = 4
HEAD_DIM = 64
GROUP_WIDTH = 512
DILATED_PATTERNS = ((128, 1), (512, 4), (2048, 16))
LNX_EPS = 64e-5
C_Q_HEADS = 8
C_KV_HEADS = 2
C_HALF_WINDOW = 128
D_HEAD_K = 128
D_HEADS = 4
D_FF = 5632
ROPE_THETA = 500000.0
ROT_DIM = 16
NORM_EPS = 1e-6
NEG_INF = -1e30

D_IN_EXT = 7168
(CB_AQ, CB_AK, CB_AV, CB_BR, CB_BK, CB_BV, CB_CQ, CB_DQ, CB_DI, CB_DG, CB_DFF, CB_DFB) = range(12)
CB_CK, CB_CV, CB_SECA, CB_SECB, CB_G1 = 48, 49, 50, 51, 52

VMEM_LIMIT = 56 * 1024 * 1024
RWKV_CHUNK = 64
HGRN_CHUNK = 64
HGRN_SUB = 16


def _cparams(sem):
    return pltpu.CompilerParams(dimension_semantics=sem, vmem_limit_bytes=VMEM_LIMIT)


def _split3(x):
    hi = x.astype(BF16)
    r1 = x - hi.astype(F32)
    mid = r1.astype(BF16)
    lo = (r1 - mid.astype(F32)).astype(BF16)
    return hi, mid, lo


def _seg_sum(x, g):
    hi, mid, lo = _split3(x)
    d = lambda a: jnp.dot(a, g, preferred_element_type=F32)
    return d(hi) + d(mid) + d(lo)


def _dot(a, b):
    return jnp.dot(a.astype(BF16), b.astype(BF16), preferred_element_type=F32)


def _dot_nt(a, b):
    return lax.dot_general(a.astype(BF16), b.astype(BF16), (((1,), (1,)), ((), ())),
                           preferred_element_type=F32)


def _dot_tn(a, b):
    return lax.dot_general(a.astype(BF16), b.astype(BF16), (((0,), (0,)), ((), ())),
                           preferred_element_type=F32)


def _split2(x):
    hi = x.astype(BF16)
    lo = (x - hi.astype(F32)).astype(BF16)
    return hi, lo


def _dot_hp(a, b):
    ah, al = _split2(a)
    bh, bl = _split2(b)
    d = lambda x, y: jnp.dot(x, y, preferred_element_type=F32)
    return d(ah, bh) + d(ah, bl) + d(al, bh)


def _softplus(u):
    return jnp.maximum(u, 0.0) + jnp.log(1.0 + jnp.exp(-jnp.abs(u)))


def _scan_cumsum(x, n, reverse):
    row = lax.broadcasted_iota(jnp.int32, x.shape, 0)
    s = 1
    while s < n:
        if reverse:
            x = x + jnp.where(row < n - s, pltpu.roll(x, n - s, 0), 0.0)
        else:
            x = x + jnp.where(row >= s, pltpu.roll(x, s, 0), 0.0)
        s *= 2
    return x


def _scan_shift(x, carry_row, n, reverse):
    row = lax.broadcasted_iota(jnp.int32, x.shape, 0)
    if reverse:
        return jnp.where(row == n - 1, carry_row, pltpu.roll(x, n - 1, 0))
    return jnp.where(row == 0, carry_row, pltpu.roll(x, 1, 0))


def _ffn_kernel(x_ref, g_ref, wg_ref, wu_ref, wo_ref, o_ref, hn_ref):
    @pl.when(pl.program_id(1) == 0)
    def _():
        x = x_ref[...]
        ms = jnp.mean(x * x, axis=-1, keepdims=True)
        hn_ref[...] = (x * lax.rsqrt(ms + NORM_EPS) * g_ref[...]).astype(BF16)
        o_ref[...] = x

    h = hn_ref[...]
    gate = jnp.dot(h, wg_ref[...], preferred_element_type=F32)
    up = jnp.dot(h, wu_ref[...], preferred_element_type=F32)
    act = (0.5 * gate * jax.nn.sigmoid(gate) * up).astype(BF16)
    o_ref[...] += jnp.dot(act, wo_ref[...], preferred_element_type=F32)


def _ffn(x2, gain, w_in, w_out, tm=512, tf=512):
    n, d = x2.shape
    nf = D_FF // tf
    return pl.pallas_call(
        _ffn_kernel,
        grid=(n // tm, nf),
        in_specs=[
            pl.BlockSpec((tm, d), lambda i, j: (i, 0)),
            pl.BlockSpec((1, d), lambda i, j: (0, 0)),
            pl.BlockSpec((d, tf), lambda i, j: (0, j)),
            pl.BlockSpec((d, tf), lambda i, j: (0, j + nf)),
            pl.BlockSpec((tf, d), lambda i, j: (j, 0)),
        ],
        out_specs=pl.BlockSpec((tm, d), lambda i, j: (i, 0)),
        out_shape=jax.ShapeDtypeStruct((n, d), F32),
        scratch_shapes=[pltpu.VMEM((tm, d), BF16)],
        compiler_params=_cparams(("parallel", "arbitrary")),
        name="ffn",
    )(x2, gain, w_in, w_in, w_out)


def _inproj_kernel(x_ref, g_ref, w_ref, o_ref, hn_ref):
    @pl.when(pl.program_id(1) == 0)
    def _():
        x = x_ref[...]
        ms = jnp.mean(x * x, axis=-1, keepdims=True)
        hn_ref[...] = (x * lax.rsqrt(ms + NORM_EPS) * g_ref[...]).astype(BF16)

    o_ref[...] = jnp.dot(hn_ref[...], w_ref[...], preferred_element_type=F32)


def _inproj(x2, gain, w_ext, tm=1024, tn=512):
    n, d = x2.shape
    nc = w_ext.shape[1]
    return pl.pallas_call(
        _inproj_kernel,
        grid=(n // tm, nc // tn),
        in_specs=[
            pl.BlockSpec((tm, d), lambda i, j: (i, 0)),
            pl.BlockSpec((1, d), lambda i, j: (0, 0)),
            pl.BlockSpec((d, tn), lambda i, j: (0, j)),
        ],
        out_specs=pl.BlockSpec((tm, tn), lambda i, j: (i, j)),
        out_shape=jax.ShapeDtypeStruct((n, nc), F32),
        scratch_shapes=[pltpu.VMEM((tm, d), BF16)],
        compiler_params=_cparams(("parallel", "arbitrary")),
        name="inproj",
    )(x2, gain, w_ext)


def _norm_rope(t, gain, g, c, s1, s2, scale):
    w = t.shape[-1]
    ss = _seg_sum(t * t, g)
    y = t * lax.rsqrt(ss * (1.0 / HEAD_DIM) + NORM_EPS) * gain
    half = ROT_DIM // 2
    out = y * c + pltpu.roll(y, w - half, 1) * s1 + pltpu.roll(y, half, 1) * s2
    return out * scale if scale != 1.0 else out


def _prep_kernel(aq_ref, ak_ref, av_ref, cq_ref, ck_ref, cv_ref, c_ref, s1_ref, s2_ref,
                 gain_ref, g_ref, qa_ref, ka_ref, va_ref, qc_ref, kc_ref, vc_ref):
    c1, s11, s21 = c_ref[...], s1_ref[...], s2_ref[...]
    c4 = jnp.concatenate([c1] * 4, axis=1)
    s14 = jnp.concatenate([s11] * 4, axis=1)
    s24 = jnp.concatenate([s21] * 4, axis=1)
    g = g_ref[...]
    g1 = g_ref[0:128, 0:128]
    scale = HEAD_DIM ** -0.5
    qa_ref[...] = _norm_rope(aq_ref[...], gain_ref[0:1, :], g, c4, s14, s24, scale).astype(BF16)
    ka_ref[...] = _norm_rope(ak_ref[...], gain_ref[1:2, :], g, c4, s14, s24, 1.0).astype(BF16)
    va_ref[...] = av_ref[...].astype(BF16)
    qc_ref[...] = _norm_rope(cq_ref[...], gain_ref[2:3, :], g, c4, s14, s24, scale).astype(BF16)
    kc_ref[...] = _norm_rope(ck_ref[...], gain_ref[3:4, 0:128], g1, c1, s11, s21, 1.0).astype(BF16)
    vc_ref[...] = cv_ref[...].astype(BF16)


def _prep(p, rope_c, rope_s1, rope_s2, gains, g512, seq, tm=512):
    n = p.shape[0]
    nseq = seq // tm
    wide = lambda cb: pl.BlockSpec((tm, 512), lambda i, cb=cb: (i, cb))
    narrow = lambda cb: pl.BlockSpec((tm, 128), lambda i, cb=cb: (i, cb))
    tab = pl.BlockSpec((tm, 128), lambda i: (i % nseq, 0))
    o512 = pl.BlockSpec((tm, 512), lambda i: (i, 0))
    o128 = pl.BlockSpec((tm, 128), lambda i: (i, 0))
    return pl.pallas_call(
        _prep_kernel,
        grid=(n // tm,),
        in_specs=[wide(CB_AQ), wide(CB_AK), wide(CB_AV), wide(CB_CQ), narrow(CB_CK), narrow(CB_CV),
                  tab, tab, tab,
                  pl.BlockSpec((8, 512), lambda i: (0, 0)),
                  pl.BlockSpec((512, 512), lambda i: (0, 0))],
        out_specs=[o512, o512, o512, o512, o128, o128],
        out_shape=[jax.ShapeDtypeStruct((n, 512), BF16)] * 4 + [jax.ShapeDtypeStruct((n, 128), BF16)] * 2,
        compiler_params=_cparams(("parallel",)),
        name="prep_qk",
    )(p, p, p, p, p, p, rope_c, rope_s1, rope_s2, gains, g512)


def _band_kernel(*refs, tq, hb, half, length, n_kv, group, with_sink):
    if with_sink:
        q_ref, kp_ref, km_ref, kn_ref, vp_ref, vm_ref, vn_ref, sink_ref, o_ref = refs
    else:
        q_ref, kp_ref, km_ref, kn_ref, vp_ref, vm_ref, vn_ref, o_ref, lse_ref = refs
    i = pl.program_id(2)
    nsub = tq // hb
    row = lax.broadcasted_iota(jnp.int32, (hb, 3 * hb), 0)
    col = lax.broadcasted_iota(jnp.int32, (hb, 3 * hb), 1)
    band = jnp.abs(row + hb - col) <= half
    valid = []
    for j in range(nsub):
        kpos = i * tq + (j - 1) * hb + col
        valid.append(band & (kpos >= 0) & (kpos < length))
    q_all = q_ref[0]
    sinks = sink_ref[...] if with_sink else None
    n_heads = n_kv * group
    hsl = [slice(h * HEAD_DIM, (h + 1) * HEAD_DIM) for h in range(n_heads)]
    k_heads = [jnp.concatenate([kp_ref[0, :, s], km_ref[0, :, s], kn_ref[0, :, s]], axis=0) for s in hsl[:n_kv]]
    v_heads = [jnp.concatenate([vp_ref[0, :, s], vm_ref[0, :, s], vn_ref[0, :, s]], axis=0) for s in hsl[:n_kv]]
    heads = range(n_heads)
    outs, lses = [], []
    for j in range(nsub):
        rows, win = slice(j * hb, (j + 1) * hb), slice(j * hb, (j + 3) * hb)
        s = [lax.dot_general(q_all[rows, hsl[h]], k_heads[h // group][win], (((1,), (1,)), ((), ())),
                             preferred_element_type=F32) for h in heads]
        s = [jnp.where(valid[j], x, NEG_INF) for x in s]
        m = [jnp.max(x, axis=-1, keepdims=True) for x in s]
        p = [jnp.exp(x - y) for x, y in zip(s, m)]
        l = [jnp.sum(x, axis=-1, keepdims=True) for x in p]
        acc = [jnp.dot(p[h].astype(BF16), v_heads[h // group][win], preferred_element_type=F32) for h in heads]
        if with_sink:
            m_all = [jnp.maximum(m[h], sinks[0:1, h:h + 1]) for h in heads]
            sc = [jnp.exp(x - y) for x, y in zip(m, m_all)]
            den = [l[h] * sc[h] + jnp.exp(sinks[0:1, h:h + 1] - m_all[h]) for h in heads]
            outs.append(jnp.concatenate([acc[h] * (sc[h] / den[h]) for h in heads], axis=1))
        else:
            outs.append(jnp.concatenate([x / y for x, y in zip(acc, l)], axis=1))
            lses.append(jnp.concatenate([jnp.broadcast_to(x + jnp.log(y), (hb, HEAD_DIM))
                                         for x, y in zip(m, l)], axis=1))
    for j in range(nsub):
        o_ref[0, j * hb:(j + 1) * hb, :] = outs[j]
        if not with_sink:
            lse_ref[0, j * hb:(j + 1) * hb, :] = lses[j]


def _band_attention(q, k, v, sink, batch, seq, dil, half, n_kv, group, tq=512):
    wq, wk = q.shape[1], k.shape[1]
    hb = half
    sl = seq // dil
    tq = min(tq, sl)
    nq = sl // tq
    per = tq // hb
    nhb = sl // hb
    qv = q.reshape(batch, sl, dil * wq)
    kv = k.reshape(batch, sl, dil * wk)
    vv = v.reshape(batch, sl, dil * wk)
    main = lambda w: pl.BlockSpec((1, tq, w), lambda b, r, i: (b, i, r))
    prev = pl.BlockSpec((1, hb, wk), lambda b, r, i: (b, jnp.maximum(i * per - 1, 0), r))
    nxt = pl.BlockSpec((1, hb, wk), lambda b, r, i: (b, jnp.minimum((i + 1) * per, nhb - 1), r))
    with_sink = sink is not None
    in_specs = [main(wq), prev, main(wk), nxt, prev, main(wk), nxt]
    args = [qv, kv, kv, kv, vv, vv, vv]
    if with_sink:
        in_specs.append(pl.BlockSpec((1, wq // HEAD_DIM), lambda b, r, i: (0, 0)))
        args.append(sink)
        out_specs = main(wq)
        out_shape = jax.ShapeDtypeStruct(qv.shape, F32)
    else:
        out_specs = [main(wq), main(wq)]
        out_shape = [jax.ShapeDtypeStruct(qv.shape, F32)] * 2
    kern = functools.partial(_band_kernel, tq=tq, hb=hb, half=half, length=sl, n_kv=n_kv,
                             group=group, with_sink=with_sink)
    out = pl.pallas_call(
        kern,
        grid=(batch, dil, nq),
        in_specs=in_specs,
        out_specs=out_specs,
        out_shape=out_shape,
        compiler_params=_cparams(("parallel", "parallel", "parallel")),
        name="band_attn_sink" if with_sink else f"band_attn_d{dil}",
    )(*args)
    if with_sink:
        return out.reshape(batch * seq, wq)
    return out[0].reshape(batch * seq, wq), out[1].reshape(batch * seq, wq)


def _rwkv_kernel(r_ref, k_ref, v_ref, sa_ref, sb_ref, pv_ref, w2_ref, a2_ref, g_ref,
                 y_ref, bonus_ref, state_ref, carry_ref, carryb_ref, *, n, reverse):
    @pl.when(pl.program_id(1) == 0)
    def _():
        state_ref[...] = jnp.zeros_like(state_ref)
        carry_ref[...] = jnp.zeros_like(carry_ref)
        carryb_ref[...] = jnp.zeros_like(carryb_ref)

    last = 0 if reverse else n - 1
    r0, k0, v0, sb = r_ref[...], k_ref[...], v_ref[...], sb_ref[...]
    shift = lambda x, c: _scan_shift(x, c, n, reverse)
    r = r0 + (shift(r0, carry_ref[0:1, :]) - r0) * pv_ref[0:1, :]
    k = k0 + (shift(k0, carry_ref[1:2, :]) - k0) * pv_ref[1:2, :]
    v = v0 + (shift(v0, carry_ref[2:3, :]) - v0) * pv_ref[2:3, :]
    xlr = sa_ref[...] + shift(sb, carryb_ref[0:1, :])
    carry_ref[0:1, :] = r0[last:last + 1, :]
    carry_ref[1:2, :] = k0[last:last + 1, :]
    carry_ref[2:3, :] = v0[last:last + 1, :]
    carryb_ref[0:1, :] = sb[last:last + 1, :]

    wl = pv_ref[3:4, :] + jnp.dot(jnp.tanh(xlr).astype(BF16), w2_ref[...], preferred_element_type=F32)
    w_log = -_softplus(-wl) - 0.5
    ld = -jnp.exp(w_log)
    a = jax.nn.sigmoid(pv_ref[4:5, :] + jnp.dot(xlr.astype(BF16), a2_ref[...], preferred_element_type=F32))
    g = g_ref[...]
    kkr = k * pv_ref[5:6, :]
    kk = kkr / jnp.maximum(jnp.sqrt(_seg_sum(kkr * kkr, g)), 1e-12)
    kx = k * (1.0 + (a - 1.0) * pv_ref[6:7, :])
    bonus_ref[...] = _seg_sum(r * kx * pv_ref[7:8, :], g) * v

    c = _scan_cumsum(ld, n, reverse)
    c_last = c[last:last + 1, :]
    e_c = jnp.exp(c)
    e_cx = jnp.exp(c - ld)
    e_nc = jnp.exp(-c)
    e_end = jnp.exp(c_last - c)
    wc = jnp.exp(c_last)
    kb = kk * a
    rt = r * e_c
    at = -kk * e_cx
    bt = kb * e_nc
    kt = kx * e_nc
    bh = kb * e_end
    kh = kx * e_end

    rt_b, at_b, bt_b, kt_b, v_b = (t.astype(BF16) for t in (rt, at, bt, kt, v))
    bh_b, kh_b = bh.astype(BF16), kh.astype(BF16)

    row = lax.broadcasted_iota(jnp.int32, (2 * n, 2 * n), 0)
    col = lax.broadcasted_iota(jnp.int32, (2 * n, 2 * n), 1)
    rr, cc = row & (n - 1), col & (n - 1)
    strict = (cc > rr) if reverse else (cc < rr)
    mask = strict | ((cc == rr) & (row >= n))
    eye = (lax.broadcasted_iota(jnp.int32, (n, n), 0) == lax.broadcasted_iota(jnp.int32, (n, n), 1)).astype(F32)

    heads = range(GROUP_WIDTH // HEAD_DIM)
    hsl = [slice(h * HEAD_DIM, (h + 1) * HEAD_DIM) for h in heads]
    cat = lambda a, b: jnp.concatenate([a, b], axis=0)
    st = [state_ref[h] for h in heads]
    big = [jnp.where(mask, _dot_nt(cat(at_b[:, s], rt_b[:, s]), cat(bt_b[:, s], kt_b[:, s])), 0.0)
           for s in hsl]
    a_ab = [b[:n, :n] for b in big]
    akv = [_dot(b[:n, n:], v_b[:, s]) for b, s in zip(big, hsl)]
    t_inv = [eye + a for a in a_ab]
    pw = a_ab
    lvl = 2
    while lvl < n:
        pw = [_dot(x, x) for x in pw]
        t_inv = [t + _dot(t, x) for t, x in zip(t_inv, pw)]
        lvl *= 2
    ta = [_dot(t, at_b[:, s]) for t, s in zip(t_inv, hsl)]
    tv = [_dot(t, x) for t, x in zip(t_inv, akv)]
    u = [_dot_nt(a, s0) + x for a, s0, x in zip(ta, st, tv)]
    uv = [cat(x, v[:, s]).astype(BF16) for x, s in zip(u, hsl)]
    y = [_dot_nt(rt_b[:, s], s0) + _dot(b[n:, :], x) for s, s0, b, x in zip(hsl, st, big, uv)]
    new_st = [s0 * wc[:, s] + _dot_tn(x, cat(bh_b[:, s], kh_b[:, s])) for s0, s, x in zip(st, hsl, uv)]
    for h in heads:
        y_ref[:, hsl[h]] = y[h]
        state_ref[h] = new_st[h]


def _rwkv(p, pvec, w2pad, a2pad, g512, batch, seq, reverse):
    n = p.shape[0]
    c = RWKV_CHUNK
    nc = seq // c
    if reverse:
        rowidx = lambda b, j: b * nc + (nc - 1 - j)
    else:
        rowidx = lambda b, j: b * nc + j
    wide = lambda cb: pl.BlockSpec((c, 512), lambda b, j, cb=cb: (rowidx(b, j), cb))
    narrow = lambda cb: pl.BlockSpec((c, 128), lambda b, j, cb=cb: (rowidx(b, j), cb))
    const = lambda shape: pl.BlockSpec(shape, lambda b, j: (0,) * len(shape))
    out = pl.BlockSpec((c, 512), lambda b, j: (rowidx(b, j), 0))
    return pl.pallas_call(
        functools.partial(_rwkv_kernel, n=c, reverse=reverse),
        grid=(batch, nc),
        in_specs=[wide(CB_BR), wide(CB_BK), wide(CB_BV), narrow(CB_SECA), narrow(CB_SECB),
                  const((8, 512)), const((128, 512)), const((128, 512)), const((512, 512))],
        out_specs=[out, out],
        out_shape=[jax.ShapeDtypeStruct((n, 512), F32)] * 2,
        scratch_shapes=[pltpu.VMEM((8, HEAD_DIM, HEAD_DIM), F32), pltpu.VMEM((8, 512), F32),
                        pltpu.VMEM((8, 128), F32)],
        compiler_params=_cparams(("parallel", "arbitrary")),
        name="rwkv_bwd" if reverse else "rwkv_fwd",
    )(p, p, p, p, p, pvec, w2pad, a2pad, g512)


def _hgrn_kernel(q_ref, v_ref, z_ref, lb_ref, o_ref, state_ref, *, n, sub, reverse):
    @pl.when(pl.program_id(1) == 0)
    def _():
        state_ref[...] = jnp.zeros_like(state_ref)

    last = 0 if reverse else n - 1
    q, v, z = q_ref[...], v_ref[...], z_ref[...]
    log_sig = -_softplus(-z)
    x1 = lb_ref[0:1, :]
    x2 = lb_ref[1:2, :] + log_sig
    logf = jnp.maximum(x1, x2) + jnp.log(1.0 + jnp.exp(-jnp.abs(x1 - x2)))
    kf = 1.0 - jnp.exp(logf)
    b = _scan_cumsum(logf, n, reverse)
    b_last = b[last:last + 1, :]
    qe = q * jnp.exp(b)
    ke = kf * jnp.exp(b_last - b)
    wend = jnp.exp(b_last)
    nsub = n // sub
    srow = lax.broadcasted_iota(jnp.int32, (sub, 1), 0)
    lane = lax.broadcasted_iota(jnp.int32, (sub, sub), 1)

    states = [state_ref[h] for h in range(D_HEADS)]
    new_states = []
    for h in range(D_HEADS):
        hs = slice(h * D_HEAD_K, (h + 1) * D_HEAD_K)
        st = states[h]
        q_h, v_h, kf_h, b_h, lf_h = q[:, hs], v[:, hs], kf[:, hs], b[:, hs], logf[:, hs]
        o_inter = _dot_nt(qe[:, hs], st)
        for m in range(nsub):
            ms = slice(m * sub, (m + 1) * sub)
            q_m, b_m, kf_m = q_h[ms], b_h[ms], kf_h[ms]
            first = (m + 1) * sub - 1 if reverse else m * sub
            rho = b_h[first:first + 1] - lf_h[first:first + 1]
            dmat = jnp.zeros((sub, sub), F32)
            for s in range(sub):
                diff = b_m - b_m[s:s + 1]
                ok = (srow <= s) if reverse else (srow >= s)
                w = q_m * (kf_m[s:s + 1] * jnp.exp(jnp.where(ok, diff, NEG_INF)))
                dmat = jnp.where(lane == s, jnp.sum(w, axis=-1, keepdims=True), dmat)
            o_m = o_inter[ms] + _dot(dmat, v_h[ms])
            es = slice((m + 1) * sub, n) if reverse else slice(0, m * sub)
            if es.stop > es.start:
                qt = q_m * jnp.exp(b_m - rho)
                kt = kf_h[es] * jnp.exp(rho - b_h[es])
                o_m = o_m + _dot(_dot_nt(qt, kt), v_h[es])
            o_ref[ms, hs] = o_m
        new_states.append(st * wend[:, hs] + _dot_tn(v_h, ke[:, hs]))
    for h in range(D_HEADS):
        state_ref[h] = new_states[h]


def _hgrn(p, lbvec, batch, seq, reverse):
    n = p.shape[0]
    c = HGRN_CHUNK
    nc = seq // c
    if reverse:
        rowidx = lambda b, j: b * nc + (nc - 1 - j)
    else:
        rowidx = lambda b, j: b * nc + j
    wide = lambda cb: pl.BlockSpec((c, 512), lambda b, j, cb=cb: (rowidx(b, j), cb))
    return pl.pallas_call(
        functools.partial(_hgrn_kernel, n=c, sub=HGRN_SUB, reverse=reverse),
        grid=(batch, nc),
        in_specs=[wide(CB_DQ), wide(CB_DI), wide(CB_DFB if reverse else CB_DFF),
                  pl.BlockSpec((8, 512), lambda b, j: (0, 0))],
        out_specs=pl.BlockSpec((c, 512), lambda b, j: (rowidx(b, j), 0)),
        out_shape=jax.ShapeDtypeStruct((n, 512), F32),
        scratch_shapes=[pltpu.VMEM((D_HEADS, D_HEAD_K, D_HEAD_K), F32)],
        compiler_params=_cparams(("parallel", "arbitrary")),
        name="hgrn_bwd" if reverse else "hgrn_fwd",
    )(p, p, p, lbvec)


def _mixpost_kernel(o1_ref, l1_ref, o2_ref, l2_ref, o3_ref, l3_ref,
                    y0_ref, y1_ref, bo0_ref, bo1_ref, g1_ref, oc_ref,
                    h0_ref, h1_ref, dg_ref, pv_ref, g_ref, g2_ref, mix_ref):
    l1, l2, l3 = l1_ref[...], l2_ref[...], l3_ref[...]
    mx = jnp.maximum(jnp.maximum(l1, l2), l3)
    w1, w2, w3 = jnp.exp(l1 - mx), jnp.exp(l2 - mx), jnp.exp(l3 - mx)
    out_a = (o1_ref[...] * w1 + o2_ref[...] * w2 + o3_ref[...] * w3) / (w1 + w2 + w3)
    mix_ref[:, 0:512] = out_a.astype(BF16)
    g = g_ref[...]
    y = y0_ref[...] + y1_ref[...]
    mu = _seg_sum(y, g) * (1.0 / HEAD_DIM)
    yc = y - mu
    var = _seg_sum(yc * yc, g) * (1.0 / HEAD_DIM)
    yn = yc * lax.rsqrt(var + LNX_EPS) * pv_ref[0:1, :] + pv_ref[1:2, :]
    gate = jnp.dot(jax.nn.sigmoid(g1_ref[...]).astype(BF16), g2_ref[...], preferred_element_type=F32)
    mix_ref[:, 512:1024] = ((yn + bo0_ref[...] + bo1_ref[...]) * gate).astype(BF16)
    mix_ref[:, 1024:1536] = oc_ref[...].astype(BF16)
    o = h0_ref[...] + h1_ref[...]
    dg = dg_ref[...]
    silu = dg * jax.nn.sigmoid(dg)
    for h in range(D_HEADS):
        hs = slice(h * D_HEAD_K, (h + 1) * D_HEAD_K)
        oh = o[:, hs]
        ms = jnp.mean(oh * oh, axis=-1, keepdims=True)
        res = oh * lax.rsqrt(ms + NORM_EPS) * pv_ref[2:3, hs] * silu[:, hs]
        mix_ref[:, 1536 + h * D_HEAD_K:1536 + (h + 1) * D_HEAD_K] = res.astype(BF16)


def _mixpost(a_parts, yb, bonus, p, oc, od, pvec, g512, g2pad, tm=256):
    n = p.shape[0]
    blk = pl.BlockSpec((tm, 512), lambda i: (i, 0))
    const = lambda shape: pl.BlockSpec(shape, lambda i: (0,) * len(shape))
    args = []
    for o, l in a_parts:
        args += [o, l]
    args += [yb[0], yb[1], bonus[0], bonus[1], p, oc, od[0], od[1], p, pvec, g512, g2pad]
    in_specs = [blk] * 10 + [pl.BlockSpec((tm, 128), lambda i: (i, CB_G1)), blk, blk, blk,
                             pl.BlockSpec((tm, 512), lambda i: (i, CB_DG)),
                             const((8, 512)), const((512, 512)), const((128, 512))]
    return pl.pallas_call(
        _mixpost_kernel,
        grid=(n // tm,),
        in_specs=in_specs,
        out_specs=pl.BlockSpec((tm, D_MODEL), lambda i: (i, 0)),
        out_shape=jax.ShapeDtypeStruct((n, D_MODEL), BF16),
        compiler_params=_cparams(("parallel",)),
        name="mixpost",
    )(*args)


def _outproj_kernel(mix_ref, w_ref, x_ref, o_ref):
    o_ref[...] = x_ref[...] + jnp.dot(mix_ref[...], w_ref[...], preferred_element_type=F32)


def _outproj(mix, w, x2, tm=1024, tn=512):
    n, d = x2.shape
    return pl.pallas_call(
        _outproj_kernel,
        grid=(n // tm, d // tn),
        in_specs=[pl.BlockSpec((tm, mix.shape[1]), lambda i, j: (i, 0)),
                  pl.BlockSpec((mix.shape[1], tn), lambda i, j: (0, j)),
                  pl.BlockSpec((tm, tn), lambda i, j: (i, j))],
        out_specs=pl.BlockSpec((tm, tn), lambda i, j: (i, j)),
        out_shape=jax.ShapeDtypeStruct((n, d), F32),
        compiler_params=_cparams(("parallel", "arbitrary")),
        name="outproj",
    )(mix, w, x2)


def _rope_tables(seq):
    half = ROT_DIM // 2
    inv = ROPE_THETA ** (-jnp.arange(0, ROT_DIM, 2, dtype=F32) / ROT_DIM)
    ang = jnp.arange(seq, dtype=F32)[:, None] * inv[None]
    cos, sin = jnp.cos(ang), jnp.sin(ang)
    ones = jnp.ones((seq, HEAD_DIM - ROT_DIM), F32)
    zeros = jnp.zeros((seq, HEAD_DIM - ROT_DIM), F32)
    z8 = jnp.zeros((seq, half), F32)
    c = jnp.concatenate([cos, cos, ones], axis=1)
    s1 = jnp.concatenate([-sin, z8, zeros], axis=1)
    s2 = jnp.concatenate([z8, sin, zeros], axis=1)
    tile2 = lambda t: jnp.concatenate([t, t], axis=1)
    return tile2(c), tile2(s1), tile2(s2)


def _block_diag_ones(width, seg):
    idx = jnp.arange(width) // seg
    return (idx[:, None] == idx[None, :]).astype(BF16)


def _pad_rows(rows, width=512, total=8):
    rows = [jnp.pad(r.astype(F32).reshape(-1), (0, width - r.size)) for r in rows]
    rows += [jnp.zeros((width,), F32)] * (total - len(rows))
    return jnp.stack(rows)


def _w_ext(w_in, mu_wa, w1, a1, g1):
    cols_a, cols_b = [], []
    for d in range(2):
        for mu, w in ((mu_wa[d, 0], w1[d]), (mu_wa[d, 1], a1[d])):
            cols_a.append((1.0 - mu)[:, None] * w)
            cols_b.append(mu[:, None] * w)
    parts = [w_in[:, 0:3584], w_in[:, 3840:6400], w_in[:, 3584:3840]] + cols_a + cols_b
    parts.append(jnp.pad(g1, ((0, 0), (0, 128 - g1.shape[1]))))
    w = jnp.concatenate(parts, axis=1)
    return jnp.pad(w, ((0, 0), (0, D_IN_EXT - w.shape[1]))).astype(BF16)


def _pad_lowrank(w, row0):
    return jnp.pad(w, ((row0, 128 - row0 - w.shape[0]), (0, 0))).astype(BF16)


def kernel(x, ln_gain, w_in, w_out, w_ffn_in, w_ffn_out, qk_gain, sink, b_mu_rkv, b_mu_wa, b_w0, b_w1,
           b_w2, b_a0, b_a1, b_a2, b_k_k, b_k_a, b_r_k, b_lnx_gain, b_lnx_bias, b_g1, b_g2,
           d_lb_logits, d_norm_gain):
    batch, seq, d = x.shape
    x2 = x.reshape(batch * seq, d)
    rope_c, rope_s1, rope_s2 = _rope_tables(seq)
    g512 = _block_diag_ones(512, HEAD_DIM)
    pr = jax.nn.softmax(d_lb_logits.astype(F32), axis=0)
    cs = jnp.cumsum(pr, axis=0)
    lb_all = cs - cs[0:1]

    for l in range(DEPTH):
        x2 = _ffn(x2, ln_gain[l, 0][None], w_ffn_in[l, 0].astype(BF16), w_ffn_out[l, 0].astype(BF16))

        p = _inproj(x2, ln_gain[l, 1][None], _w_ext(w_in[l], b_mu_wa[l], b_w1[l], b_a1[l], b_g1[l]))

        tile8 = lambda t: jnp.tile(t, 8)
        gains = _pad_rows([tile8(qk_gain[l, 0, 0]), tile8(qk_gain[l, 0, 1]),
                           tile8(qk_gain[l, 1, 0]), jnp.tile(qk_gain[l, 1, 1], 2)])
        qa, ka, va, qc, kc, vc = _prep(p, rope_c, rope_s1, rope_s2, gains, g512, seq)

        a_parts = []
        for window, dil in DILATED_PATTERNS:
            a_parts.append(_band_attention(qa, ka, va, None, batch, seq, dil, window // (2 * dil),
                                           GROUP_WIDTH // HEAD_DIM, 1))
        oc = _band_attention(qc, kc, vc, sink[l][None].astype(F32), batch, seq, 1, C_HALF_WINDOW,
                             C_KV_HEADS, C_Q_HEADS // C_KV_HEADS)

        yb, bonus = [], []
        for dr in range(2):
            pvec = _pad_rows([b_mu_rkv[l, dr, 0], b_mu_rkv[l, dr, 1], b_mu_rkv[l, dr, 2], b_w0[l, dr],
                              b_a0[l, dr], b_k_k[l], b_k_a[l], b_r_k[l]])
            y, bo = _rwkv(p, pvec, _pad_lowrank(b_w2[l, dr], 64 * dr), _pad_lowrank(b_a2[l, dr], 64 * dr + 32),
                          g512, batch, seq, reverse=(dr == 1))
            yb.append(y)
            bonus.append(bo)

        od = []
        for dr in range(2):
            lb = lb_all[l, dr]
            od.append(_hgrn(p, _pad_rows([jnp.log(lb), jnp.log1p(-lb)]), batch, seq, reverse=(dr == 1)))

        pvec = _pad_rows([b_lnx_gain[l], b_lnx_bias[l], d_norm_gain[l]])
        mix = _mixpost(a_parts, yb, bonus, p, oc, od, pvec, g512, _pad_lowrank(b_g2[l], 0))
        x2 = _outproj(mix, w_out[l].astype(BF16), x2)

        x2 = _ffn(x2, ln_gain[l, 2][None], w_ffn_in[l, 1].astype(BF16), w_ffn_out[l, 1].astype(BF16))
    return x2.reshape(batch, seq, d)
```

```python
import functools

import jax
import jax.numpy as jnp
from jax import lax
from jax.experimental import pallas as pl
from jax.experimental.pallas import tpu as pltpu

F32 = jnp.float32
BF16 = jnp.bfloat16

D_MODEL = 2048
DEPTH = 4
HEAD_DIM = 64
GROUP_WIDTH = 512
DILATED_PATTERNS = ((128, 1), (512, 4), (2048, 16))
LNX_EPS = 64e-5
C_Q_HEADS = 8
C_KV_HEADS = 2
C_HALF_WINDOW = 128
D_HEAD_K = 128
D_HEADS = 4
D_FF = 5632
ROPE_THETA = 500000.0
ROT_DIM = 16
NORM_EPS = 1e-6
NEG_INF = -1e30
LOG2_E = 1.4426950408889634

D_IN_EXT = 7168
(CB_AQ, CB_AK, CB_AV, CB_BR, CB_BK, CB_BV, CB_CQ, CB_DQ, CB_DI, CB_DG, CB_DFF, CB_DFB) = range(12)
CB_CK, CB_CV, CB_SECA, CB_SECB, CB_G1 = 48, 49, 50, 51, 52

VMEM_LIMIT = 56 * 1024 * 1024
RWKV_CHUNK = 64
RWKV_BLOCK = 256
HGRN_CHUNK = 64
HGRN_BLOCK = 128
HGRN_SUB = 16


def _cparams(sem):
    return pltpu.CompilerParams(dimension_semantics=sem, vmem_limit_bytes=VMEM_LIMIT)


def _seg_sum(x, g):
    hi = x.astype(BF16)
    lo = (x - hi.astype(F32)).astype(BF16)
    d = lambda a: jnp.dot(a, g, preferred_element_type=F32)
    return d(hi) + d(lo)


def _dot(a, b):
    return jnp.dot(a.astype(BF16), b.astype(BF16), preferred_element_type=F32)


def _dot_nt(a, b):
    return lax.dot_general(a.astype(BF16), b.astype(BF16), (((1,), (1,)), ((), ())),
                           preferred_element_type=F32)


def _dot_tn(a, b):
    return lax.dot_general(a.astype(BF16), b.astype(BF16), (((0,), (0,)), ((), ())),
                           preferred_element_type=F32)


def _softplus(u):
    return jnp.maximum(u, 0.0) + jnp.log(1.0 + jnp.exp(-jnp.abs(u)))


def _scan_cumsum(x, n, reverse, seg=None):
    seg = n if seg is None else seg
    row = lax.broadcasted_iota(jnp.int32, x.shape, 0) & (seg - 1)
    s = 1
    while s < seg:
        if reverse:
            x = x + jnp.where(row < seg - s, pltpu.roll(x, n - s, 0), 0.0)
        else:
            x = x + jnp.where(row >= s, pltpu.roll(x, s, 0), 0.0)
        s *= 2
    return x


def _scan_shift(x, carry_row, n, reverse):
    row = lax.broadcasted_iota(jnp.int32, x.shape, 0)
    if reverse:
        return jnp.where(row == n - 1, carry_row, pltpu.roll(x, n - 1, 0))
    return jnp.where(row == 0, carry_row, pltpu.roll(x, 1, 0))


def _ffn_kernel(x_ref, g_ref, wg_ref, wu_ref, wo_ref, o_ref, hn_ref):
    @pl.when(pl.program_id(1) == 0)
    def _():
        x = x_ref[...]
        ms = jnp.mean(x * x, axis=-1, keepdims=True)
        hn_ref[...] = (x * lax.rsqrt(ms + NORM_EPS) * g_ref[...]).astype(BF16)
        o_ref[...] = x

    h = hn_ref[...]
    gate = jnp.dot(h, wg_ref[...], preferred_element_type=F32)
    up = jnp.dot(h, wu_ref[...], preferred_element_type=F32)
    act = (0.5 * gate * jax.nn.sigmoid(gate) * up).astype(BF16)
    o_ref[...] += jnp.dot(act, wo_ref[...], preferred_element_type=F32)


def _ffn(x2, gain, w_in, w_out, tm=512, tf=512):
    n, d = x2.shape
    nf = D_FF // tf
    return pl.pallas_call(
        _ffn_kernel,
        grid=(n // tm, nf),
        in_specs=[
            pl.BlockSpec((tm, d), lambda i, j: (i, 0)),
            pl.BlockSpec((1, d), lambda i, j: (0, 0)),
            pl.BlockSpec((d, tf), lambda i, j: (0, j)),
            pl.BlockSpec((d, tf), lambda i, j: (0, j + nf)),
            pl.BlockSpec((tf, d), lambda i, j: (j, 0)),
        ],
        out_specs=pl.BlockSpec((tm, d), lambda i, j: (i, 0)),
        out_shape=jax.ShapeDtypeStruct((n, d), F32),
        scratch_shapes=[pltpu.VMEM((tm, d), BF16)],
        compiler_params=_cparams(("parallel", "arbitrary")),
        name="ffn",
    )(x2, gain, w_in, w_in, w_out)


def _inproj_kernel(x_ref, g_ref, w_ref, o_ref, hn_ref):
    @pl.when(pl.program_id(1) == 0)
    def _():
        x = x_ref[...]
        ms = jnp.mean(x * x, axis=-1, keepdims=True)
        hn_ref[...] = (x * lax.rsqrt(ms + NORM_EPS) * g_ref[...]).astype(BF16)

    o_ref[...] = jnp.dot(hn_ref[...], w_ref[...], preferred_element_type=F32)


def _inproj(x2, gain, w_ext, tm=1024, tn=1024):
    n, d = x2.shape
    nc = w_ext.shape[1]
    return pl.pallas_call(
        _inproj_kernel,
        grid=(n // tm, nc // tn),
        in_specs=[
            pl.BlockSpec((tm, d), lambda i, j: (i, 0)),
            pl.BlockSpec((1, d), lambda i, j: (0, 0)),
            pl.BlockSpec((d, tn), lambda i, j: (0, j)),
        ],
        out_specs=pl.BlockSpec((tm, tn), lambda i, j: (i, j)),
        out_shape=jax.ShapeDtypeStruct((n, nc), F32),
        scratch_shapes=[pltpu.VMEM((tm, d), BF16)],
        compiler_params=_cparams(("parallel", "arbitrary")),
        name="inproj",
    )(x2, gain, w_ext)


def _norm_rope(t, gain, g, c, s1, s2, scale):
    w = t.shape[-1]
    ss = _seg_sum(t * t, g)
    y = t * lax.rsqrt(ss * (1.0 / HEAD_DIM) + NORM_EPS) * gain
    half = ROT_DIM // 2
    out = y * c + pltpu.roll(y, w - half, 1) * s1 + pltpu.roll(y, half, 1) * s2
    return out * scale if scale != 1.0 else out


def _prep_kernel(aq_ref, ak_ref, av_ref, cq_ref, ck_ref, cv_ref, c_ref, s1_ref, s2_ref,
                 gain_ref, g_ref,
                 qa_ref, ka_ref, va_ref, qa4_ref, ka4_ref, va4_ref, qa16_ref, ka16_ref, va16_ref,
                 qc_ref, kc_ref, vc_ref, scr_ref, *, tm):
    c1, s11, s21 = c_ref[...], s1_ref[...], s2_ref[...]
    c4 = jnp.concatenate([c1] * 4, axis=1)
    s14 = jnp.concatenate([s11] * 4, axis=1)
    s24 = jnp.concatenate([s21] * 4, axis=1)
    g = g_ref[...]
    g1 = g_ref[0:128, 0:128]
    scale = HEAD_DIM ** -0.5

    def emit(y, nat_ref, dil_refs):
        nat_ref[...] = y.astype(BF16)
        for c in range(4):
            scr_ref[c] = y[:, c * 128:(c + 1) * 128]
        for d, ref in dil_refs:
            for r in range(d):
                for c in range(4):
                    lo = r * 512 + c * 128
                    ref[:, lo:lo + 128] = scr_ref[c, pl.ds(r, tm // d, stride=d), :].astype(BF16)

    emit(_norm_rope(aq_ref[...], gain_ref[0:1, :], g, c4, s14, s24, scale), qa_ref,
         ((4, qa4_ref), (16, qa16_ref)))
    emit(_norm_rope(ak_ref[...], gain_ref[1:2, :], g, c4, s14, s24, 1.0), ka_ref,
         ((4, ka4_ref), (16, ka16_ref)))
    emit(av_ref[...], va_ref, ((4, va4_ref), (16, va16_ref)))
    qc_ref[...] = _norm_rope(cq_ref[...], gain_ref[2:3, :], g, c4, s14, s24, scale).astype(BF16)
    kc_ref[...] = _norm_rope(ck_ref[...], gain_ref[3:4, 0:128], g1, c1, s11, s21, 1.0).astype(BF16)
    vc_ref[...] = cv_ref[...].astype(BF16)


def _prep(p, rope_c, rope_s1, rope_s2, gains, g512, seq, tm=512):
    n = p.shape[0]
    nseq = seq // tm
    wide = lambda cb: pl.BlockSpec((tm, 512), lambda i, cb=cb: (i, cb))
    narrow = lambda cb: pl.BlockSpec((tm, 128), lambda i, cb=cb: (i, cb))
    tab = pl.BlockSpec((tm, 128), lambda i: (i % nseq, 0))
    rowblk = lambda rows, width: pl.BlockSpec((rows, width), lambda i: (i, 0))
    shape = lambda rows, width: jax.ShapeDtypeStruct((rows, width), BF16)
    trio = lambda d: [rowblk(tm // d, d * 512)] * 3
    trio_shape = lambda d: [shape(n // d, d * 512)] * 3
    return pl.pallas_call(
        functools.partial(_prep_kernel, tm=tm),
        grid=(n // tm,),
        in_specs=[wide(CB_AQ), wide(CB_AK), wide(CB_AV), wide(CB_CQ), narrow(CB_CK), narrow(CB_CV),
                  tab, tab, tab,
                  pl.BlockSpec((8, 512), lambda i: (0, 0)),
                  pl.BlockSpec((512, 512), lambda i: (0, 0))],
        out_specs=trio(1) + trio(4) + trio(16) + [rowblk(tm, 512), rowblk(tm, 128), rowblk(tm, 128)],
        out_shape=trio_shape(1) + trio_shape(4) + trio_shape(16) + [shape(n, 512), shape(n, 128), shape(n, 128)],
        scratch_shapes=[pltpu.VMEM((4, tm, 128), F32)],
        compiler_params=_cparams(("parallel",)),
        name="prep_qk",
    )(p, p, p, p, p, p, rope_c, rope_s1, rope_s2, gains, g512)


def _band_kernel(*refs, tq, hb, half, length, n_kv, group, with_sink):
    if with_sink:
        q_ref, kp_ref, km_ref, kn_ref, vp_ref, vm_ref, vn_ref, sink_ref, o_ref = refs
    else:
        q_ref, kp_ref, km_ref, kn_ref, vp_ref, vm_ref, vn_ref, o_ref, lse_ref = refs
    i = pl.program_id(2)
    nsub = tq // hb
    row = lax.broadcasted_iota(jnp.int32, (hb, 3 * hb), 0)
    col = lax.broadcasted_iota(jnp.int32, (hb, 3 * hb), 1)
    band = jnp.abs(row + hb - col) <= half
    valid = []
    for j in range(nsub):
        kpos = i * tq + (j - 1) * hb + col
        valid.append(band & (kpos >= 0) & (kpos < length))
    q_all = q_ref[0]
    sinks = sink_ref[...] if with_sink else None
    n_heads = n_kv * group
    hsl = [slice(h * HEAD_DIM, (h + 1) * HEAD_DIM) for h in range(n_heads)]
    k_heads = [jnp.concatenate([kp_ref[0, :, s], km_ref[0, :, s], kn_ref[0, :, s]], axis=0) for s in hsl[:n_kv]]
    v_heads = [jnp.concatenate([vp_ref[0, :, s], vm_ref[0, :, s], vn_ref[0, :, s]], axis=0) for s in hsl[:n_kv]]
    heads = range(n_heads)
    outs, lses = [], []
    for j in range(nsub):
        rows, win = slice(j * hb, (j + 1) * hb), slice(j * hb, (j + 3) * hb)
        s = [lax.dot_general(q_all[rows, hsl[h]], k_heads[h // group][win], (((1,), (1,)), ((), ())),
                             preferred_element_type=F32) for h in heads]
        s = [jnp.where(valid[j], x, NEG_INF) for x in s]
        m = [jnp.max(x, axis=-1, keepdims=True) for x in s]
        p = [jnp.exp(x - y) for x, y in zip(s, m)]
        l = [jnp.sum(x, axis=-1, keepdims=True) for x in p]
        acc = [jnp.dot(p[h].astype(BF16), v_heads[h // group][win], preferred_element_type=F32) for h in heads]
        if with_sink:
            m_all = [jnp.maximum(m[h], sinks[0:1, h:h + 1]) for h in heads]
            sc = [jnp.exp(x - y) for x, y in zip(m, m_all)]
            den = [l[h] * sc[h] + jnp.exp(sinks[0:1, h:h + 1] - m_all[h]) for h in heads]
            outs.append(jnp.concatenate([acc[h] * (sc[h] / den[h]) for h in heads], axis=1))
        else:
            outs.append(jnp.concatenate([x / y for x, y in zip(acc, l)], axis=1))
            lses.append(jnp.concatenate([jnp.broadcast_to(x + jnp.log(y), (hb, HEAD_DIM))
                                         for x, y in zip(m, l)], axis=1))
    for j in range(nsub):
        o_ref[0, j * hb:(j + 1) * hb, :] = outs[j]
        if not with_sink:
            lse_ref[0, j * hb:(j + 1) * hb, :] = lses[j]


def _band_attention(q, k, v, sink, batch, seq, dil, half, n_kv, group, tq=512):
    wq, wk = q.shape[1] // dil, k.shape[1] // dil
    hb = half
    sl = seq // dil
    tq = min(tq, sl)
    nq = sl // tq
    per = tq // hb
    nhb = sl // hb
    qv = q.reshape(batch, sl, dil * wq)
    kv = k.reshape(batch, sl, dil * wk)
    vv = v.reshape(batch, sl, dil * wk)
    main = lambda w: pl.BlockSpec((1, tq, w), lambda b, r, i: (b, i, r))
    prev = pl.BlockSpec((1, hb, wk), lambda b, r, i: (b, jnp.maximum(i * per - 1, 0), r))
    nxt = pl.BlockSpec((1, hb, wk), lambda b, r, i: (b, jnp.minimum((i + 1) * per, nhb - 1), r))
    with_sink = sink is not None
    in_specs = [main(wq), prev, main(wk), nxt, prev, main(wk), nxt]
    args = [qv, kv, kv, kv, vv, vv, vv]
    if with_sink:
        in_specs.append(pl.BlockSpec((1, wq // HEAD_DIM), lambda b, r, i: (0, 0)))
        args.append(sink)
        out_specs = main(wq)
        out_shape = jax.ShapeDtypeStruct(qv.shape, F32)
    else:
        out_specs = [main(wq), main(wq)]
        out_shape = [jax.ShapeDtypeStruct(qv.shape, F32)] * 2
    kern = functools.partial(_band_kernel, tq=tq, hb=hb, half=half, length=sl, n_kv=n_kv,
                             group=group, with_sink=with_sink)
    out = pl.pallas_call(
        kern,
        grid=(batch, dil, nq),
        in_specs=in_specs,
        out_specs=out_specs,
        out_shape=out_shape,
        compiler_params=_cparams(("parallel", "parallel", "parallel")),
        name="band_attn_sink" if with_sink else f"band_attn_d{dil}",
    )(*args)
    flat = lambda t: t.reshape(batch * sl, dil * wq)
    if with_sink:
        return flat(out)
    return flat(out[0]), flat(out[1])


def _rwkv_kernel(r_ref, k_ref, v_ref, sa_ref, sb_ref, pv_ref, w2_ref, a2_ref, g_ref,
                 y_ref, bonus_ref, state_ref, carry_ref, carryb_ref, *, n, cs, reverse):
    @pl.when(pl.program_id(1) == 0)
    def _():
        state_ref[...] = jnp.zeros_like(state_ref)
        carry_ref[...] = jnp.zeros_like(carry_ref)
        carryb_ref[...] = jnp.zeros_like(carryb_ref)

    last = 0 if reverse else n - 1
    r0, k0, v0, sb = r_ref[...], k_ref[...], v_ref[...], sb_ref[...]
    shift = lambda x, c: _scan_shift(x, c, n, reverse)
    r = r0 + (shift(r0, carry_ref[0:1, :]) - r0) * pv_ref[0:1, :]
    k = k0 + (shift(k0, carry_ref[1:2, :]) - k0) * pv_ref[1:2, :]
    v = v0 + (shift(v0, carry_ref[2:3, :]) - v0) * pv_ref[2:3, :]
    xlr = sa_ref[...] + shift(sb, carryb_ref[0:1, :])
    carry_ref[0:1, :] = r0[last:last + 1, :]
    carry_ref[1:2, :] = k0[last:last + 1, :]
    carry_ref[2:3, :] = v0[last:last + 1, :]
    carryb_ref[0:1, :] = sb[last:last + 1, :]

    wl = pv_ref[3:4, :] + jnp.dot(jnp.tanh(xlr).astype(BF16), w2_ref[...], preferred_element_type=F32)
    w_log = -_softplus(-wl) - 0.5
    ld = -jnp.exp(w_log)
    a = jax.nn.sigmoid(pv_ref[4:5, :] + jnp.dot(xlr.astype(BF16), a2_ref[...], preferred_element_type=F32))
    g = g_ref[...]
    kkr = k * pv_ref[5:6, :]
    kx = k * (1.0 + (a - 1.0) * pv_ref[6:7, :])
    sums = _seg_sum(jnp.concatenate([kkr * kkr, r * kx * pv_ref[7:8, :]], axis=0), g)
    kk = kkr / jnp.maximum(jnp.sqrt(sums[:n]), 1e-12)
    bonus_ref[...] = sums[n:] * v

    nsub = n // cs
    last_row = lambda gi: gi * cs if reverse else (gi + 1) * cs - 1
    row1 = lax.broadcasted_iota(jnp.int32, (n, 1), 0)
    c = _scan_cumsum(ld, n, reverse, seg=cs)
    c_last = c[last_row(0):last_row(0) + 1, :]
    for gi in range(1, nsub):
        c_last = jnp.where(row1 >= gi * cs, c[last_row(gi):last_row(gi) + 1, :], c_last)
    e_c = jnp.exp(c)
    e_cx = jnp.exp(c - ld)
    e_nc = jnp.exp(-c)
    e_end = jnp.exp(c_last - c)
    kb = kk * a
    rt = r * e_c
    at = -kk * e_cx
    bt = kb * e_nc
    kt = kx * e_nc
    bh = kb * e_end
    kh = kx * e_end

    rt_b, at_b, bt_b, kt_b, v_b = (t.astype(BF16) for t in (rt, at, bt, kt, v))
    bh_b, kh_b = bh.astype(BF16), kh.astype(BF16)

    row = lax.broadcasted_iota(jnp.int32, (2 * cs, 2 * cs), 0)
    col = lax.broadcasted_iota(jnp.int32, (2 * cs, 2 * cs), 1)
    rr, cc = row & (cs - 1), col & (cs - 1)
    strict = (cc > rr) if reverse else (cc < rr)
    mask = strict | ((cc == rr) & (row >= cs))
    eye = (lax.broadcasted_iota(jnp.int32, (cs, cs), 0) == lax.broadcasted_iota(jnp.int32, (cs, cs), 1)).astype(F32)

    heads = range(GROUP_WIDTH // HEAD_DIM)
    nh = len(heads)
    cat = lambda a, b: jnp.concatenate([a, b], axis=0)
    pairs = [(slice(gi * cs, (gi + 1) * cs), slice(h * HEAD_DIM, (h + 1) * HEAD_DIM))
             for gi in range(nsub) for h in heads]
    big = [jnp.where(mask, _dot_nt(cat(at_b[rs, s], rt_b[rs, s]), cat(bt_b[rs, s], kt_b[rs, s])), 0.0)
           for rs, s in pairs]
    akv = [_dot(b[:cs, cs:], v_b[rs, s]) for b, (rs, s) in zip(big, pairs)]
    a_ab = [b[:cs, :cs] for b in big]
    t_inv = [eye + x for x in a_ab]
    pw = [_dot(x, x) for x in a_ab]
    lvl = 4
    while lvl < cs:
        res = [_dot(cat(x, t), x) for x, t in zip(pw, t_inv)]
        t_inv = [t + x[cs:] for t, x in zip(t_inv, res)]
        pw = [x[:cs] for x in res]
        lvl *= 2
    t_inv = [t + _dot(t, x) for t, x in zip(t_inv, pw)]
    ta = [_dot(t, at_b[rs, s]) for t, (rs, s) in zip(t_inv, pairs)]
    tv = [_dot(t, x) for t, x in zip(t_inv, akv)]

    st = [state_ref[h] for h in heads]
    ys = [None] * nsub
    for gi in (range(nsub - 1, -1, -1) if reverse else range(nsub)):
        wc = jnp.exp(c[last_row(gi):last_row(gi) + 1, :])
        sl = slice(gi * nh, (gi + 1) * nh)
        u = [_dot_nt(x, s0) + z for x, s0, z in zip(ta[sl], st, tv[sl])]
        uv = [cat(x, v[rs, s]).astype(BF16) for x, (rs, s) in zip(u, pairs[sl])]
        ys[gi] = [_dot_nt(rt_b[rs, s], s0) + _dot(b[cs:, :], x)
                  for (rs, s), s0, b, x in zip(pairs[sl], st, big[sl], uv)]
        st = [s0 * wc[:, s] + _dot_tn(x, cat(bh_b[rs, s], kh_b[rs, s]))
              for s0, (rs, s), x in zip(st, pairs[sl], uv)]
    for gi in range(nsub):
        for h in heads:
            rs, s = pairs[gi * nh + h]
            y_ref[rs, s] = ys[gi][h]
    for h in heads:
        state_ref[h] = st[h]


def _rwkv(p, pvec, w2pad, a2pad, g512, batch, seq, reverse):
    n = p.shape[0]
    c = RWKV_BLOCK
    nc = seq // c
    if reverse:
        rowidx = lambda b, j: b * nc + (nc - 1 - j)
    else:
        rowidx = lambda b, j: b * nc + j
    wide = lambda cb: pl.BlockSpec((c, 512), lambda b, j, cb=cb: (rowidx(b, j), cb))
    narrow = lambda cb: pl.BlockSpec((c, 128), lambda b, j, cb=cb: (rowidx(b, j), cb))
    const = lambda shape: pl.BlockSpec(shape, lambda b, j: (0,) * len(shape))
    out = pl.BlockSpec((c, 512), lambda b, j: (rowidx(b, j), 0))
    return pl.pallas_call(
        functools.partial(_rwkv_kernel, n=c, cs=RWKV_CHUNK, reverse=reverse),
        grid=(batch, nc),
        in_specs=[wide(CB_BR), wide(CB_BK), wide(CB_BV), narrow(CB_SECA), narrow(CB_SECB),
                  const((8, 512)), const((128, 512)), const((128, 512)), const((512, 512))],
        out_specs=[out, out],
        out_shape=[jax.ShapeDtypeStruct((n, 512), F32)] * 2,
        scratch_shapes=[pltpu.VMEM((8, HEAD_DIM, HEAD_DIM), F32), pltpu.VMEM((8, 512), F32),
                        pltpu.VMEM((8, 128), F32)],
        compiler_params=_cparams(("parallel", "arbitrary")),
        name="rwkv_bwd" if reverse else "rwkv_fwd",
    )(p, p, p, p, p, pvec, w2pad, a2pad, g512)


def _hgrn_kernel(q_ref, v_ref, z_ref, lb_ref, o_ref, state_ref, *, n, cs, sub, reverse):
    @pl.when(pl.program_id(1) == 0)
    def _():
        state_ref[...] = jnp.zeros_like(state_ref)

    q, v, z = q_ref[...], v_ref[...], z_ref[...]
    log_sig = -_softplus(-z)
    x1 = lb_ref[0:1, :]
    x2 = lb_ref[1:2, :] + log_sig
    logf = jnp.maximum(x1, x2) + jnp.log(1.0 + jnp.exp(-jnp.abs(x1 - x2)))
    kf = 1.0 - jnp.exp(logf)
    nchunk, nsub = n // cs, cs // sub
    last_row = lambda gi: gi * cs if reverse else (gi + 1) * cs - 1
    row1 = lax.broadcasted_iota(jnp.int32, (n, 1), 0)
    b = _scan_cumsum(logf, n, reverse, seg=cs)
    b_last = b[last_row(0):last_row(0) + 1, :]
    for gi in range(1, nchunk):
        b_last = jnp.where(row1 >= gi * cs, b[last_row(gi):last_row(gi) + 1, :], b_last)
    qe = q * jnp.exp(b)
    ke = kf * jnp.exp(b_last - b)
    b2 = b * LOG2_E
    c2 = b2 - jnp.log(kf) * LOG2_E
    srow = lax.broadcasted_iota(jnp.int32, (sub, sub), 0)
    lane = lax.broadcasted_iota(jnp.int32, (sub, sub), 1)
    tri = (srow <= lane) if reverse else (srow >= lane)
    hsl = [slice(h * D_HEAD_K, (h + 1) * D_HEAD_K) for h in range(D_HEADS)]

    intra = {}
    for gi in range(nchunk):
        base = gi * cs
        for h, hs in enumerate(hsl):
            for m in range(nsub):
                ms = slice(base + m * sub, base + (m + 1) * sub)
                q_m, b_m = q[ms, hs], b[ms, hs]
                b2_m, c2_m = b2[ms, hs], c2[ms, hs]
                dmat = jnp.zeros((sub, sub), F32)
                for s in range(sub):
                    w = q_m * jnp.exp2(b2_m - c2_m[s:s + 1])
                    dmat = jnp.where(lane == s, jnp.sum(w, axis=-1, keepdims=True), dmat)
                dmat = jnp.where(tri, dmat, 0.0)
                o_m = _dot(dmat, v[ms, hs])
                es = slice(base + (m + 1) * sub, base + cs) if reverse else slice(base, base + m * sub)
                if es.stop > es.start:
                    first = base + ((m + 1) * sub - 1 if reverse else m * sub)
                    rho = b[first:first + 1, hs] - logf[first:first + 1, hs]
                    qt = q_m * jnp.exp(b_m - rho)
                    kt = kf[es, hs] * jnp.exp(rho - b[es, hs])
                    o_m = o_m + _dot(_dot_nt(qt, kt), v[es, hs])
                intra[gi, h, m] = o_m

    st = [state_ref[h] for h in range(D_HEADS)]
    inter = {}
    for gi in (range(nchunk - 1, -1, -1) if reverse else range(nchunk)):
        rs = slice(gi * cs, (gi + 1) * cs)
        wend = jnp.exp(b[last_row(gi):last_row(gi) + 1, :])
        for h, hs in enumerate(hsl):
            inter[gi, h] = _dot_nt(qe[rs, hs], st[h])
        st = [st[h] * wend[:, hs] + _dot_tn(v[rs, hs], ke[rs, hs]) for h, hs in enumerate(hsl)]
    for (gi, h, m), o_m in intra.items():
        lo = gi * cs + m * sub
        o_ref[lo:lo + sub, hsl[h]] = o_m + inter[gi, h][m * sub:(m + 1) * sub]
    for h in range(D_HEADS):
        state_ref[h] = st[h]


def _hgrn(p, lbvec, batch, seq, reverse):
    n = p.shape[0]
    c = HGRN_BLOCK
    nc = seq // c
    if reverse:
        rowidx = lambda b, j: b * nc + (nc - 1 - j)
    else:
        rowidx = lambda b, j: b * nc + j
    wide = lambda cb: pl.BlockSpec((c, 512), lambda b, j, cb=cb: (rowidx(b, j), cb))
    return pl.pallas_call(
        functools.partial(_hgrn_kernel, n=c, cs=HGRN_CHUNK, sub=HGRN_SUB, reverse=reverse),
        grid=(batch, nc),
        in_specs=[wide(CB_DQ), wide(CB_DI), wide(CB_DFB if reverse else CB_DFF),
                  pl.BlockSpec((8, 512), lambda b, j: (0, 0))],
        out_specs=pl.BlockSpec((c, 512), lambda b, j: (rowidx(b, j), 0)),
        out_shape=jax.ShapeDtypeStruct((n, 512), F32),
        scratch_shapes=[pltpu.VMEM((D_HEADS, D_HEAD_K, D_HEAD_K), F32)],
        compiler_params=_cparams(("parallel", "arbitrary")),
        name="hgrn_bwd" if reverse else "hgrn_fwd",
    )(p, p, p, lbvec)


def _mixpost_kernel(o1_ref, l1_ref, o2_ref, l2_ref, o3_ref, l3_ref,
                    y0_ref, y1_ref, bo0_ref, bo1_ref, g1_ref, oc_ref,
                    h0_ref, h1_ref, dg_ref, pv_ref, g_ref, g2_ref, mix_ref,
                    o2s_ref, l2s_ref, o3s_ref, l3s_ref, *, tm):
    for d, src, dst in ((4, o2_ref, o2s_ref), (4, l2_ref, l2s_ref), (16, o3_ref, o3s_ref), (16, l3_ref, l3s_ref)):
        for r in range(d):
            for c in range(4):
                lo = r * 512 + c * 128
                dst[c, pl.ds(r, tm // d, stride=d), :] = src[:, lo:lo + 128]
    wide = lambda ref: jnp.concatenate([ref[c] for c in range(4)], axis=1)
    l1, l2, l3 = l1_ref[...], wide(l2s_ref), wide(l3s_ref)
    mx = jnp.maximum(jnp.maximum(l1, l2), l3)
    w1, w2, w3 = jnp.exp(l1 - mx), jnp.exp(l2 - mx), jnp.exp(l3 - mx)
    out_a = (o1_ref[...] * w1 + wide(o2s_ref) * w2 + wide(o3s_ref) * w3) / (w1 + w2 + w3)
    mix_ref[:, 0:512] = out_a.astype(BF16)
    g = g_ref[...]
    y = y0_ref[...] + y1_ref[...]
    mu = _seg_sum(y, g) * (1.0 / HEAD_DIM)
    yc = y - mu
    var = _seg_sum(yc * yc, g) * (1.0 / HEAD_DIM)
    yn = yc * lax.rsqrt(var + LNX_EPS) * pv_ref[0:1, :] + pv_ref[1:2, :]
    gate = jnp.dot(jax.nn.sigmoid(g1_ref[...]).astype(BF16), g2_ref[...], preferred_element_type=F32)
    mix_ref[:, 512:1024] = ((yn + bo0_ref[...] + bo1_ref[...]) * gate).astype(BF16)
    mix_ref[:, 1024:1536] = oc_ref[...].astype(BF16)
    o = h0_ref[...] + h1_ref[...]
    dg = dg_ref[...]
    silu = dg * jax.nn.sigmoid(dg)
    for h in range(D_HEADS):
        hs = slice(h * D_HEAD_K, (h + 1) * D_HEAD_K)
        oh = o[:, hs]
        ms = jnp.mean(oh * oh, axis=-1, keepdims=True)
        res = oh * lax.rsqrt(ms + NORM_EPS) * pv_ref[2:3, hs] * silu[:, hs]
        mix_ref[:, 1536 + h * D_HEAD_K:1536 + (h + 1) * D_HEAD_K] = res.astype(BF16)


def _mixpost(a_parts, yb, bonus, p, oc, od, pvec, g512, g2pad, tm=256):
    n = p.shape[0]
    blk = pl.BlockSpec((tm, 512), lambda i: (i, 0))
    const = lambda shape: pl.BlockSpec(shape, lambda i: (0,) * len(shape))
    args = []
    for o, l in a_parts:
        args += [o, l]
    args += [yb[0], yb[1], bonus[0], bonus[1], p, oc, od[0], od[1], p, pvec, g512, g2pad]
    dil = lambda d: pl.BlockSpec((tm // d, d * 512), lambda i: (i, 0))
    in_specs = [blk, blk, dil(4), dil(4), dil(16), dil(16)] + [blk] * 4 + [
        pl.BlockSpec((tm, 128), lambda i: (i, CB_G1)), blk, blk, blk,
        pl.BlockSpec((tm, 512), lambda i: (i, CB_DG)),
        const((8, 512)), const((512, 512)), const((128, 512))]
    return pl.pallas_call(
        functools.partial(_mixpost_kernel, tm=tm),
        grid=(n // tm,),
        in_specs=in_specs,
        out_specs=pl.BlockSpec((tm, D_MODEL), lambda i: (i, 0)),
        out_shape=jax.ShapeDtypeStruct((n, D_MODEL), BF16),
        scratch_shapes=[pltpu.VMEM((4, tm, 128), F32)] * 4,
        compiler_params=_cparams(("parallel",)),
        name="mixpost",
    )(*args)


def _outproj_kernel(mix_ref, w_ref, x_ref, o_ref):
    o_ref[...] = x_ref[...] + jnp.dot(mix_ref[...], w_ref[...], preferred_element_type=F32)


def _outproj(mix, w, x2, tm=1024, tn=512):
    n, d = x2.shape
    return pl.pallas_call(
        _outproj_kernel,
        grid=(n // tm, d // tn),
        in_specs=[pl.BlockSpec((tm, mix.shape[1]), lambda i, j: (i, 0)),
                  pl.BlockSpec((mix.shape[1], tn), lambda i, j: (0, j)),
                  pl.BlockSpec((tm, tn), lambda i, j: (i, j))],
        out_specs=pl.BlockSpec((tm, tn), lambda i, j: (i, j)),
        out_shape=jax.ShapeDtypeStruct((n, d), F32),
        compiler_params=_cparams(("parallel", "arbitrary")),
        name="outproj",
    )(mix, w, x2)


def _rope_tables(seq):
    half = ROT_DIM // 2
    inv = ROPE_THETA ** (-jnp.arange(0, ROT_DIM, 2, dtype=F32) / ROT_DIM)
    ang = jnp.arange(seq, dtype=F32)[:, None] * inv[None]
    cos, sin = jnp.cos(ang), jnp.sin(ang)
    ones = jnp.ones((seq, HEAD_DIM - ROT_DIM), F32)
    zeros = jnp.zeros((seq, HEAD_DIM - ROT_DIM), F32)
    z8 = jnp.zeros((seq, half), F32)
    c = jnp.concatenate([cos, cos, ones], axis=1)
    s1 = jnp.concatenate([-sin, z8, zeros], axis=1)
    s2 = jnp.concatenate([z8, sin, zeros], axis=1)
    tile2 = lambda t: jnp.concatenate([t, t], axis=1)
    return tile2(c), tile2(s1), tile2(s2)


def _block_diag_ones(width, seg):
    idx = jnp.arange(width) // seg
    return (idx[:, None] == idx[None, :]).astype(BF16)


def _pad_rows(rows, width=512, total=8):
    rows = [jnp.pad(r.astype(F32).reshape(-1), (0, width - r.size)) for r in rows]
    rows += [jnp.zeros((width,), F32)] * (total - len(rows))
    return jnp.stack(rows)


def _w_ext(w_in, mu_wa, w1, a1, g1):
    cols_a, cols_b = [], []
    for d in range(2):
        for mu, w in ((mu_wa[d, 0], w1[d]), (mu_wa[d, 1], a1[d])):
            cols_a.append((1.0 - mu)[:, None] * w)
            cols_b.append(mu[:, None] * w)
    parts = [w_in[:, 0:3584], w_in[:, 3840:6400], w_in[:, 3584:3840]] + cols_a + cols_b
    parts.append(jnp.pad(g1, ((0, 0), (0, 128 - g1.shape[1]))))
    w = jnp.concatenate(parts, axis=1)
    return jnp.pad(w, ((0, 0), (0, D_IN_EXT - w.shape[1]))).astype(BF16)


def _pad_lowrank(w, row0):
    return jnp.pad(w, ((row0, 128 - row0 - w.shape[0]), (0, 0))).astype(BF16)


def kernel(x, ln_gain, w_in, w_out, w_ffn_in, w_ffn_out, qk_gain, sink, b_mu_rkv, b_mu_wa, b_w0, b_w1,
           b_w2, b_a0, b_a1, b_a2, b_k_k, b_k_a, b_r_k, b_lnx_gain, b_lnx_bias, b_g1, b_g2,
           d_lb_logits, d_norm_gain):
    batch, seq, d = x.shape
    x2 = x.reshape(batch * seq, d)
    rope_c, rope_s1, rope_s2 = _rope_tables(seq)
    g512 = _block_diag_ones(512, HEAD_DIM)
    pr = jax.nn.softmax(d_lb_logits.astype(F32), axis=0)
    cs = jnp.cumsum(pr, axis=0)
    lb_all = cs - cs[0:1]

    for l in range(DEPTH):
        x2 = _ffn(x2, ln_gain[l, 0][None], w_ffn_in[l, 0].astype(BF16), w_ffn_out[l, 0].astype(BF16))

        p = _inproj(x2, ln_gain[l, 1][None], _w_ext(w_in[l], b_mu_wa[l], b_w1[l], b_a1[l], b_g1[l]))

        tile8 = lambda t: jnp.tile(t, 8)
        gains = _pad_rows([tile8(qk_gain[l, 0, 0]), tile8(qk_gain[l, 0, 1]),
                           tile8(qk_gain[l, 1, 0]), jnp.tile(qk_gain[l, 1, 1], 2)])
        prepped = _prep(p, rope_c, rope_s1, rope_s2, gains, g512, seq)
        qc, kc, vc = prepped[9:]

        a_parts = []
        for pi, (window, dil) in enumerate(DILATED_PATTERNS):
            qa, ka, va = prepped[3 * pi:3 * pi + 3]
            a_parts.append(_band_attention(qa, ka, va, None, batch, seq, dil, window // (2 * dil),
                                           GROUP_WIDTH // HEAD_DIM, 1))
        oc = _band_attention(qc, kc, vc, sink[l][None].astype(F32), batch, seq, 1, C_HALF_WINDOW,
                             C_KV_HEADS, C_Q_HEADS // C_KV_HEADS)

        yb, bonus = [], []
        for dr in range(2):
            pvec = _pad_rows([b_mu_rkv[l, dr, 0], b_mu_rkv[l, dr, 1], b_mu_rkv[l, dr, 2], b_w0[l, dr],
                              b_a0[l, dr], b_k_k[l], b_k_a[l], b_r_k[l]])
            y, bo = _rwkv(p, pvec, _pad_lowrank(b_w2[l, dr], 64 * dr), _pad_lowrank(b_a2[l, dr], 64 * dr + 32),
                          g512, batch, seq, reverse=(dr == 1))
            yb.append(y)
            bonus.append(bo)

        od = []
        for dr in range(2):
            lb = lb_all[l, dr]
            od.append(_hgrn(p, _pad_rows([jnp.log(lb), jnp.log1p(-lb)]), batch, seq, reverse=(dr == 1)))

        pvec = _pad_rows([b_lnx_gain[l], b_lnx_bias[l], d_norm_gain[l]])
        mix = _mixpost(a_parts, yb, bonus, p, oc, od, pvec, g512, _pad_lowrank(b_g2[l], 0))
        x2 = _outproj(mix, w_out[l].astype(BF16), x2)

        x2 = _ffn(x2, ln_gain[l, 2][None], w_ffn_in[l, 1].astype(BF16), w_ffn_out[l, 1].astype(BF16))
    return x2.reshape(batch, seq, d)
```

```python
import functools

import jax
import jax.numpy as jnp
from jax import lax
from jax.experimental import pallas as pl
from jax.experimental.pallas import tpu as pltpu

F32 = jnp.float32
BF16 = jnp.bfloat16

D_MODEL = 2048
DEPTH = 4
HEAD_DIM = 64
GROUP_WIDTH = 512
DILATED_PATTERNS = ((128, 1), (512, 4), (2048, 16))
LNX_EPS = 64e-5
C_Q_HEADS = 8
C_KV_HEADS = 2
C_HALF_WINDOW = 128
D_HEAD_K = 128
D_HEADS = 4
D_FF = 5632
ROPE_THETA = 500000.0
ROT_DIM = 16
NORM_EPS = 1e-6
NEG_INF = -1e30
LOG2_E = 1.4426950408889634

D_IN_EXT = 7168
(CB_AQ, CB_AK, CB_AV, CB_BR, CB_BK, CB_BV, CB_CQ, CB_DQ, CB_DI, CB_DG, CB_DFF, CB_DFB) = range(12)
CB_CK, CB_CV, CB_SECA, CB_SECB, CB_G1 = 48, 49, 50, 51, 52

VMEM_LIMIT = 56 * 1024 * 1024
RWKV_CHUNK = 64
RWKV_BLOCK = 256
HGRN_CHUNK = 64
ATTN_SUB_ROWS = 256
HGRN_BLOCK = 256
HGRN_SUB = 16


def _cparams(sem):
    return pltpu.CompilerParams(dimension_semantics=sem, vmem_limit_bytes=VMEM_LIMIT)


def _seg_sum(x, g):
    hi = x.astype(BF16)
    lo = (x - hi.astype(F32)).astype(BF16)
    d = lambda a: jnp.dot(a, g, preferred_element_type=F32)
    return d(hi) + d(lo)


def _dot(a, b):
    return jnp.dot(a.astype(BF16), b.astype(BF16), preferred_element_type=F32)


def _dot_nt(a, b):
    return lax.dot_general(a.astype(BF16), b.astype(BF16), (((1,), (1,)), ((), ())),
                           preferred_element_type=F32)


def _dot_tn(a, b):
    return lax.dot_general(a.astype(BF16), b.astype(BF16), (((0,), (0,)), ((), ())),
                           preferred_element_type=F32)


def _softplus(u):
    return jnp.maximum(u, 0.0) + jnp.log(1.0 + jnp.exp(-jnp.abs(u)))


def _scan_cumsum(x, n, reverse, seg=None):
    seg = n if seg is None else seg
    row = lax.broadcasted_iota(jnp.int32, x.shape, 0) & (seg - 1)
    s = 1
    while s < seg:
        if reverse:
            x = x + jnp.where(row < seg - s, pltpu.roll(x, n - s, 0), 0.0)
        else:
            x = x + jnp.where(row >= s, pltpu.roll(x, s, 0), 0.0)
        s *= 2
    return x


def _scan_shift(x, carry_row, n, reverse):
    row = lax.broadcasted_iota(jnp.int32, x.shape, 0)
    if reverse:
        return jnp.where(row == n - 1, carry_row, pltpu.roll(x, n - 1, 0))
    return jnp.where(row == 0, carry_row, pltpu.roll(x, 1, 0))


def _ffn_kernel(x_ref, g_ref, wg_ref, wu_ref, wo_ref, o_ref, hn_ref):
    @pl.when(pl.program_id(1) == 0)
    def _():
        x = x_ref[...]
        ms = jnp.mean(x * x, axis=-1, keepdims=True)
        hn_ref[...] = (x * lax.rsqrt(ms + NORM_EPS) * g_ref[...]).astype(BF16)
        o_ref[...] = x

    h = hn_ref[...]
    gate = jnp.dot(h, wg_ref[...], preferred_element_type=F32)
    up = jnp.dot(h, wu_ref[...], preferred_element_type=F32)
    act = (0.5 * gate * jax.nn.sigmoid(gate) * up).astype(BF16)
    o_ref[...] += jnp.dot(act, wo_ref[...], preferred_element_type=F32)


def _ffn(x2, gain, w_in, w_out, tm=512, tf=512):
    n, d = x2.shape
    nf = D_FF // tf
    return pl.pallas_call(
        _ffn_kernel,
        grid=(n // tm, nf),
        in_specs=[
            pl.BlockSpec((tm, d), lambda i, j: (i, 0)),
            pl.BlockSpec((1, d), lambda i, j: (0, 0)),
            pl.BlockSpec((d, tf), lambda i, j: (0, j)),
            pl.BlockSpec((d, tf), lambda i, j: (0, j + nf)),
            pl.BlockSpec((tf, d), lambda i, j: (j, 0)),
        ],
        out_specs=pl.BlockSpec((tm, d), lambda i, j: (i, 0)),
        out_shape=jax.ShapeDtypeStruct((n, d), F32),
        scratch_shapes=[pltpu.VMEM((tm, d), BF16)],
        compiler_params=_cparams(("parallel", "arbitrary")),
        name="ffn",
    )(x2, gain, w_in, w_in, w_out)


def _inproj_kernel(x_ref, g_ref, w_ref, o_ref, hn_ref):
    @pl.when(pl.program_id(1) == 0)
    def _():
        x = x_ref[...]
        ms = jnp.mean(x * x, axis=-1, keepdims=True)
        hn_ref[...] = (x * lax.rsqrt(ms + NORM_EPS) * g_ref[...]).astype(BF16)

    o_ref[...] = jnp.dot(hn_ref[...], w_ref[...], preferred_element_type=F32)


def _inproj(x2, gain, w_ext, tm=1024, tn=1024):
    n, d = x2.shape
    nc = w_ext.shape[1]
    return pl.pallas_call(
        _inproj_kernel,
        grid=(n // tm, nc // tn),
        in_specs=[
            pl.BlockSpec((tm, d), lambda i, j: (i, 0)),
            pl.BlockSpec((1, d), lambda i, j: (0, 0)),
            pl.BlockSpec((d, tn), lambda i, j: (0, j)),
        ],
        out_specs=pl.BlockSpec((tm, tn), lambda i, j: (i, j)),
        out_shape=jax.ShapeDtypeStruct((n, nc), F32),
        scratch_shapes=[pltpu.VMEM((tm, d), BF16)],
        compiler_params=_cparams(("parallel", "arbitrary")),
        name="inproj",
    )(x2, gain, w_ext)


def _norm_rope(t, gain, g, c, s1, s2, scale):
    w = t.shape[-1]
    ss = _seg_sum(t * t, g)
    y = t * lax.rsqrt(ss * (1.0 / HEAD_DIM) + NORM_EPS) * gain
    half = ROT_DIM // 2
    out = y * c + pltpu.roll(y, w - half, 1) * s1 + pltpu.roll(y, half, 1) * s2
    return out * scale if scale != 1.0 else out


def _prep_kernel(aq_ref, ak_ref, av_ref, cq_ref, ck_ref, cv_ref, c_ref, s1_ref, s2_ref,
                 gain_ref, g_ref,
                 qa_ref, ka_ref, va_ref, qa4_ref, ka4_ref, va4_ref, qa16_ref, ka16_ref, va16_ref,
                 qc_ref, kc_ref, vc_ref, scr_ref, *, tm):
    c1, s11, s21 = c_ref[...], s1_ref[...], s2_ref[...]
    c4 = jnp.concatenate([c1] * 4, axis=1)
    s14 = jnp.concatenate([s11] * 4, axis=1)
    s24 = jnp.concatenate([s21] * 4, axis=1)
    g = g_ref[...]
    g1 = g_ref[0:128, 0:128]
    scale = HEAD_DIM ** -0.5

    def emit(y, nat_ref, dil_refs):
        nat_ref[...] = y.astype(BF16)
        for c in range(4):
            scr_ref[c] = y[:, c * 128:(c + 1) * 128]
        for d, ref in dil_refs:
            for r in range(d):
                for c in range(4):
                    lo = r * 512 + c * 128
                    ref[:, lo:lo + 128] = scr_ref[c, pl.ds(r, tm // d, stride=d), :].astype(BF16)

    emit(_norm_rope(aq_ref[...], gain_ref[0:1, :], g, c4, s14, s24, scale), qa_ref,
         ((4, qa4_ref), (16, qa16_ref)))
    emit(_norm_rope(ak_ref[...], gain_ref[1:2, :], g, c4, s14, s24, 1.0), ka_ref,
         ((4, ka4_ref), (16, ka16_ref)))
    emit(av_ref[...], va_ref, ((4, va4_ref), (16, va16_ref)))
    qc_ref[...] = _norm_rope(cq_ref[...], gain_ref[2:3, :], g, c4, s14, s24, scale).astype(BF16)
    kc_ref[...] = _norm_rope(ck_ref[...], gain_ref[3:4, 0:128], g1, c1, s11, s21, 1.0).astype(BF16)
    vc_ref[...] = cv_ref[...].astype(BF16)


def _prep(p, rope_c, rope_s1, rope_s2, gains, g512, seq, tm=512):
    n = p.shape[0]
    nseq = seq // tm
    wide = lambda cb: pl.BlockSpec((tm, 512), lambda i, cb=cb: (i, cb))
    narrow = lambda cb: pl.BlockSpec((tm, 128), lambda i, cb=cb: (i, cb))
    tab = pl.BlockSpec((tm, 128), lambda i: (i % nseq, 0))
    rowblk = lambda rows, width: pl.BlockSpec((rows, width), lambda i: (i, 0))
    shape = lambda rows, width: jax.ShapeDtypeStruct((rows, width), BF16)
    trio = lambda d: [rowblk(tm // d, d * 512)] * 3
    trio_shape = lambda d: [shape(n // d, d * 512)] * 3
    return pl.pallas_call(
        functools.partial(_prep_kernel, tm=tm),
        grid=(n // tm,),
        in_specs=[wide(CB_AQ), wide(CB_AK), wide(CB_AV), wide(CB_CQ), narrow(CB_CK), narrow(CB_CV),
                  tab, tab, tab,
                  pl.BlockSpec((8, 512), lambda i: (0, 0)),
                  pl.BlockSpec((512, 512), lambda i: (0, 0))],
        out_specs=trio(1) + trio(4) + trio(16) + [rowblk(tm, 512), rowblk(tm, 128), rowblk(tm, 128)],
        out_shape=trio_shape(1) + trio_shape(4) + trio_shape(16) + [shape(n, 512), shape(n, 128), shape(n, 128)],
        scratch_shapes=[pltpu.VMEM((4, tm, 128), F32)],
        compiler_params=_cparams(("parallel",)),
        name="prep_qk",
    )(p, p, p, p, p, p, rope_c, rope_s1, rope_s2, gains, g512)


def _band_kernel(*refs, tq, hb, sb, half, length, n_kv, group, with_sink):
    if with_sink:
        q_ref, kp_ref, km_ref, kn_ref, vp_ref, vm_ref, vn_ref, sink_ref, o_ref = refs
    else:
        q_ref, kp_ref, km_ref, kn_ref, vp_ref, vm_ref, vn_ref, o_ref, lse_ref = refs
    i = pl.program_id(2)
    nsub = tq // sb
    row = lax.broadcasted_iota(jnp.int32, (sb, sb + 2 * hb), 0)
    col = lax.broadcasted_iota(jnp.int32, (sb, sb + 2 * hb), 1)
    band = jnp.abs(row + hb - col) <= half
    valid = []
    for j in range(nsub):
        kpos = i * tq + j * sb - hb + col
        valid.append(band & (kpos >= 0) & (kpos < length))
    q_all = q_ref[0]
    sinks = sink_ref[...] if with_sink else None
    n_heads = n_kv * group
    hsl = [slice(h * HEAD_DIM, (h + 1) * HEAD_DIM) for h in range(n_heads)]
    k_heads = [jnp.concatenate([kp_ref[0, :, s], km_ref[0, :, s], kn_ref[0, :, s]], axis=0) for s in hsl[:n_kv]]
    v_heads = [jnp.concatenate([vp_ref[0, :, s], vm_ref[0, :, s], vn_ref[0, :, s]], axis=0) for s in hsl[:n_kv]]
    heads = range(n_heads)
    outs, lses = [], []
    for j in range(nsub):
        rows, win = slice(j * sb, (j + 1) * sb), slice(j * sb, (j + 1) * sb + 2 * hb)
        s = [lax.dot_general(q_all[rows, hsl[h]], k_heads[h // group][win], (((1,), (1,)), ((), ())),
                             preferred_element_type=F32) for h in heads]
        s = [jnp.where(valid[j], x, NEG_INF) for x in s]
        m = [jnp.max(x, axis=-1, keepdims=True) for x in s]
        p = [jnp.exp(x - y) for x, y in zip(s, m)]
        l = [jnp.sum(x, axis=-1, keepdims=True) for x in p]
        acc = [jnp.dot(p[h].astype(BF16), v_heads[h // group][win], preferred_element_type=F32) for h in heads]
        if with_sink:
            m_all = [jnp.maximum(m[h], sinks[0:1, h:h + 1]) for h in heads]
            sc = [jnp.exp(x - y) for x, y in zip(m, m_all)]
            den = [l[h] * sc[h] + jnp.exp(sinks[0:1, h:h + 1] - m_all[h]) for h in heads]
            outs.append(jnp.concatenate([acc[h] * (sc[h] / den[h]) for h in heads], axis=1))
        else:
            outs.append(jnp.concatenate([x / y for x, y in zip(acc, l)], axis=1))
            lses.append(jnp.concatenate([jnp.broadcast_to(x + jnp.log(y), (sb, HEAD_DIM))
                                         for x, y in zip(m, l)], axis=1))
    for j in range(nsub):
        o_ref[0, j * sb:(j + 1) * sb, :] = outs[j].astype(o_ref.dtype)
        if not with_sink:
            lse_ref[0, j * sb:(j + 1) * sb, :] = lses[j]


def _band_attention(q, k, v, sink, batch, seq, dil, half, n_kv, group, tq=512):
    wq, wk = q.shape[1] // dil, k.shape[1] // dil
    hb = half
    sl = seq // dil
    tq = min(tq, sl)
    nq = sl // tq
    per = tq // hb
    nhb = sl // hb
    qv = q.reshape(batch, sl, dil * wq)
    kv = k.reshape(batch, sl, dil * wk)
    vv = v.reshape(batch, sl, dil * wk)
    main = lambda w: pl.BlockSpec((1, tq, w), lambda b, r, i: (b, i, r))
    prev = pl.BlockSpec((1, hb, wk), lambda b, r, i: (b, jnp.maximum(i * per - 1, 0), r))
    nxt = pl.BlockSpec((1, hb, wk), lambda b, r, i: (b, jnp.minimum((i + 1) * per, nhb - 1), r))
    with_sink = sink is not None
    in_specs = [main(wq), prev, main(wk), nxt, prev, main(wk), nxt]
    args = [qv, kv, kv, kv, vv, vv, vv]
    if with_sink:
        in_specs.append(pl.BlockSpec((1, wq // HEAD_DIM), lambda b, r, i: (0, 0)))
        args.append(sink)
        out_specs = main(wq)
        out_shape = jax.ShapeDtypeStruct(qv.shape, BF16)
    else:
        out_specs = [main(wq), main(wq)]
        out_shape = [jax.ShapeDtypeStruct(qv.shape, BF16), jax.ShapeDtypeStruct(qv.shape, F32)]
    kern = functools.partial(_band_kernel, tq=tq, hb=hb, sb=min(tq, ATTN_SUB_ROWS), half=half, length=sl,
                             n_kv=n_kv, group=group, with_sink=with_sink)
    out = pl.pallas_call(
        kern,
        grid=(batch, dil, nq),
        in_specs=in_specs,
        out_specs=out_specs,
        out_shape=out_shape,
        compiler_params=_cparams(("parallel", "parallel", "parallel")),
        name="band_attn_sink" if with_sink else f"band_attn_d{dil}",
    )(*args)
    flat = lambda t: t.reshape(batch * sl, dil * wq)
    if with_sink:
        return flat(out)
    return flat(out[0]), flat(out[1])


def _rwkv_kernel(r_ref, k_ref, v_ref, sa_ref, sb_ref, pv_ref, w2_ref, a2_ref, g_ref,
                 y_ref, bonus_ref, state_ref, carry_ref, carryb_ref, *, n, cs, reverse):
    @pl.when(pl.program_id(1) == 0)
    def _():
        state_ref[...] = jnp.zeros_like(state_ref)
        carry_ref[...] = jnp.zeros_like(carry_ref)
        carryb_ref[...] = jnp.zeros_like(carryb_ref)

    last = 0 if reverse else n - 1
    r0, k0, v0, sb = r_ref[...], k_ref[...], v_ref[...], sb_ref[...]
    shift = lambda x, c: _scan_shift(x, c, n, reverse)
    r = r0 + (shift(r0, carry_ref[0:1, :]) - r0) * pv_ref[0:1, :]
    k = k0 + (shift(k0, carry_ref[1:2, :]) - k0) * pv_ref[1:2, :]
    v = v0 + (shift(v0, carry_ref[2:3, :]) - v0) * pv_ref[2:3, :]
    xlr = sa_ref[...] + shift(sb, carryb_ref[0:1, :])
    carry_ref[0:1, :] = r0[last:last + 1, :]
    carry_ref[1:2, :] = k0[last:last + 1, :]
    carry_ref[2:3, :] = v0[last:last + 1, :]
    carryb_ref[0:1, :] = sb[last:last + 1, :]

    wl = pv_ref[3:4, :] + jnp.dot(jnp.tanh(xlr).astype(BF16), w2_ref[...], preferred_element_type=F32)
    w_log = -_softplus(-wl) - 0.5
    ld = -jnp.exp(w_log)
    a = jax.nn.sigmoid(pv_ref[4:5, :] + jnp.dot(xlr.astype(BF16), a2_ref[...], preferred_element_type=F32))
    g = g_ref[...]
    kkr = k * pv_ref[5:6, :]
    kx = k * (1.0 + (a - 1.0) * pv_ref[6:7, :])
    sums = _seg_sum(jnp.concatenate([kkr * kkr, r * kx * pv_ref[7:8, :]], axis=0), g)
    kk = kkr / jnp.maximum(jnp.sqrt(sums[:n]), 1e-12)
    bonus_ref[...] = (sums[n:] * v).astype(bonus_ref.dtype)

    nsub = n // cs
    last_row = lambda gi: gi * cs if reverse else (gi + 1) * cs - 1
    row1 = lax.broadcasted_iota(jnp.int32, (n, 1), 0)
    c = _scan_cumsum(ld, n, reverse, seg=cs)
    c_last = c[last_row(0):last_row(0) + 1, :]
    for gi in range(1, nsub):
        c_last = jnp.where(row1 >= gi * cs, c[last_row(gi):last_row(gi) + 1, :], c_last)
    e_c = jnp.exp(c)
    e_cx = jnp.exp(c - ld)
    e_nc = jnp.exp(-c)
    e_end = jnp.exp(c_last - c)
    kb = kk * a
    rt = r * e_c
    at = -kk * e_cx
    bt = kb * e_nc
    kt = kx * e_nc
    bh = kb * e_end
    kh = kx * e_end

    rt_b, at_b, bt_b, kt_b, v_b = (t.astype(BF16) for t in (rt, at, bt, kt, v))
    bh_b, kh_b = bh.astype(BF16), kh.astype(BF16)

    row = lax.broadcasted_iota(jnp.int32, (2 * cs, 2 * cs), 0)
    col = lax.broadcasted_iota(jnp.int32, (2 * cs, 2 * cs), 1)
    rr, cc = row & (cs - 1), col & (cs - 1)
    strict = (cc > rr) if reverse else (cc < rr)
    mask = strict | ((cc == rr) & (row >= cs))
    eye = (lax.broadcasted_iota(jnp.int32, (cs, cs), 0) == lax.broadcasted_iota(jnp.int32, (cs, cs), 1)).astype(F32)

    heads = range(GROUP_WIDTH // HEAD_DIM)
    nh = len(heads)
    cat = lambda a, b: jnp.concatenate([a, b], axis=0)
    pairs = [(slice(gi * cs, (gi + 1) * cs), slice(h * HEAD_DIM, (h + 1) * HEAD_DIM))
             for gi in range(nsub) for h in heads]
    big = [jnp.where(mask, _dot_nt(cat(at_b[rs, s], rt_b[rs, s]), cat(bt_b[rs, s], kt_b[rs, s])), 0.0)
           for rs, s in pairs]
    akv = [_dot(b[:cs, cs:], v_b[rs, s]) for b, (rs, s) in zip(big, pairs)]
    a_ab = [b[:cs, :cs] for b in big]
    t_inv = [eye + x for x in a_ab]
    pw = [_dot(x, x) for x in a_ab]
    lvl = 4
    while lvl < cs:
        res = [_dot(cat(x, t), x) for x, t in zip(pw, t_inv)]
        t_inv = [t + x[cs:] for t, x in zip(t_inv, res)]
        pw = [x[:cs] for x in res]
        lvl *= 2
    t_inv = [t + _dot(t, x) for t, x in zip(t_inv, pw)]
    ta = [_dot(t, at_b[rs, s]) for t, (rs, s) in zip(t_inv, pairs)]
    tv = [_dot(t, x) for t, x in zip(t_inv, akv)]

    st = [state_ref[h] for h in heads]
    ys = [None] * nsub
    for gi in (range(nsub - 1, -1, -1) if reverse else range(nsub)):
        wc = jnp.exp(c[last_row(gi):last_row(gi) + 1, :])
        sl = slice(gi * nh, (gi + 1) * nh)
        u = [_dot_nt(x, s0) + z for x, s0, z in zip(ta[sl], st, tv[sl])]
        uv = [cat(x, v[rs, s]).astype(BF16) for x, (rs, s) in zip(u, pairs[sl])]
        ys[gi] = [_dot_nt(rt_b[rs, s], s0) + _dot(b[cs:, :], x)
                  for (rs, s), s0, b, x in zip(pairs[sl], st, big[sl], uv)]
        st = [s0 * wc[:, s] + _dot_tn(x, cat(bh_b[rs, s], kh_b[rs, s]))
              for s0, (rs, s), x in zip(st, pairs[sl], uv)]
    for gi in range(nsub):
        for h in heads:
            rs, s = pairs[gi * nh + h]
            y_ref[rs, s] = ys[gi][h]
    for h in heads:
        state_ref[h] = st[h]


def _rwkv(p, pvec, w2pad, a2pad, g512, batch, seq, reverse):
    n = p.shape[0]
    c = RWKV_BLOCK
    nc = seq // c
    if reverse:
        rowidx = lambda b, j: b * nc + (nc - 1 - j)
    else:
        rowidx = lambda b, j: b * nc + j
    wide = lambda cb: pl.BlockSpec((c, 512), lambda b, j, cb=cb: (rowidx(b, j), cb))
    narrow = lambda cb: pl.BlockSpec((c, 128), lambda b, j, cb=cb: (rowidx(b, j), cb))
    const = lambda shape: pl.BlockSpec(shape, lambda b, j: (0,) * len(shape))
    out = pl.BlockSpec((c, 512), lambda b, j: (rowidx(b, j), 0))
    return pl.pallas_call(
        functools.partial(_rwkv_kernel, n=c, cs=RWKV_CHUNK, reverse=reverse),
        grid=(batch, nc),
        in_specs=[wide(CB_BR), wide(CB_BK), wide(CB_BV), narrow(CB_SECA), narrow(CB_SECB),
                  const((8, 512)), const((128, 512)), const((128, 512)), const((512, 512))],
        out_specs=[out, out],
        out_shape=[jax.ShapeDtypeStruct((n, 512), F32), jax.ShapeDtypeStruct((n, 512), BF16)],
        scratch_shapes=[pltpu.VMEM((8, HEAD_DIM, HEAD_DIM), F32), pltpu.VMEM((8, 512), F32),
                        pltpu.VMEM((8, 128), F32)],
        compiler_params=_cparams(("parallel", "arbitrary")),
        name="rwkv_bwd" if reverse else "rwkv_fwd",
    )(p, p, p, p, p, pvec, w2pad, a2pad, g512)


def _hgrn_kernel(q_ref, v_ref, z_ref, lb_ref, o_ref, state_ref, *, n, cs, sub, reverse):
    @pl.when(pl.program_id(1) == 0)
    def _():
        state_ref[...] = jnp.zeros_like(state_ref)

    q, v, z = q_ref[...], v_ref[...], z_ref[...]
    log_sig = -_softplus(-z)
    x1 = lb_ref[0:1, :]
    x2 = lb_ref[1:2, :] + log_sig
    logf = jnp.maximum(x1, x2) + jnp.log(1.0 + jnp.exp(-jnp.abs(x1 - x2)))
    kf = 1.0 - jnp.exp(logf)
    nchunk, nsub = n // cs, cs // sub
    last_row = lambda gi: gi * cs if reverse else (gi + 1) * cs - 1
    row1 = lax.broadcasted_iota(jnp.int32, (n, 1), 0)
    b = _scan_cumsum(logf, n, reverse, seg=cs)
    b_last = b[last_row(0):last_row(0) + 1, :]
    for gi in range(1, nchunk):
        b_last = jnp.where(row1 >= gi * cs, b[last_row(gi):last_row(gi) + 1, :], b_last)
    qe = q * jnp.exp(b)
    ke = kf * jnp.exp(b_last - b)
    b2 = b * LOG2_E
    c2 = b2 - jnp.log(kf) * LOG2_E
    srow = lax.broadcasted_iota(jnp.int32, (sub, sub), 0)
    lane = lax.broadcasted_iota(jnp.int32, (sub, sub), 1)
    tri = (srow <= lane) if reverse else (srow >= lane)
    hsl = [slice(h * D_HEAD_K, (h + 1) * D_HEAD_K) for h in range(D_HEADS)]

    intra = {}
    for gi in range(nchunk):
        base = gi * cs
        for h, hs in enumerate(hsl):
            for m in range(nsub):
                ms = slice(base + m * sub, base + (m + 1) * sub)
                q_m, b_m = q[ms, hs], b[ms, hs]
                b2_m, c2_m = b2[ms, hs], c2[ms, hs]
                dmat = jnp.zeros((sub, sub), F32)
                for s in range(sub):
                    w = q_m * jnp.exp2(b2_m - c2_m[s:s + 1])
                    dmat = jnp.where(lane == s, jnp.sum(w, axis=-1, keepdims=True), dmat)
                dmat = jnp.where(tri, dmat, 0.0)
                o_m = _dot(dmat, v[ms, hs])
                es = slice(base + (m + 1) * sub, base + cs) if reverse else slice(base, base + m * sub)
                if es.stop > es.start:
                    first = base + ((m + 1) * sub - 1 if reverse else m * sub)
                    rho = b[first:first + 1, hs] - logf[first:first + 1, hs]
                    qt = q_m * jnp.exp(b_m - rho)
                    kt = kf[es, hs] * jnp.exp(rho - b[es, hs])
                    o_m = o_m + _dot(_dot_nt(qt, kt), v[es, hs])
                intra[gi, h, m] = o_m

    st = [state_ref[h] for h in range(D_HEADS)]
    inter = {}
    for gi in (range(nchunk - 1, -1, -1) if reverse else range(nchunk)):
        rs = slice(gi * cs, (gi + 1) * cs)
        wend = jnp.exp(b[last_row(gi):last_row(gi) + 1, :])
        for h, hs in enumerate(hsl):
            inter[gi, h] = _dot_nt(qe[rs, hs], st[h])
        st = [st[h] * wend[:, hs] + _dot_tn(v[rs, hs], ke[rs, hs]) for h, hs in enumerate(hsl)]
    for (gi, h, m), o_m in intra.items():
        lo = gi * cs + m * sub
        o_ref[lo:lo + sub, hsl[h]] = (o_m + inter[gi, h][m * sub:(m + 1) * sub]).astype(o_ref.dtype)
    for h in range(D_HEADS):
        state_ref[h] = st[h]


def _hgrn(p, lbvec, batch, seq, reverse):
    n = p.shape[0]
    c = HGRN_BLOCK
    nc = seq // c
    if reverse:
        rowidx = lambda b, j: b * nc + (nc - 1 - j)
    else:
        rowidx = lambda b, j: b * nc + j
    wide = lambda cb: pl.BlockSpec((c, 512), lambda b, j, cb=cb: (rowidx(b, j), cb))
    return pl.pallas_call(
        functools.partial(_hgrn_kernel, n=c, cs=HGRN_CHUNK, sub=HGRN_SUB, reverse=reverse),
        grid=(batch, nc),
        in_specs=[wide(CB_DQ), wide(CB_DI), wide(CB_DFB if reverse else CB_DFF),
                  pl.BlockSpec((8, 512), lambda b, j: (0, 0))],
        out_specs=pl.BlockSpec((c, 512), lambda b, j: (rowidx(b, j), 0)),
        out_shape=jax.ShapeDtypeStruct((n, 512), BF16),
        scratch_shapes=[pltpu.VMEM((D_HEADS, D_HEAD_K, D_HEAD_K), F32)],
        compiler_params=_cparams(("parallel", "arbitrary")),
        name="hgrn_bwd" if reverse else "hgrn_fwd",
    )(p, p, p, lbvec)


def _mixpost_kernel(o1_ref, l1_ref, o2_ref, l2_ref, o3_ref, l3_ref,
                    y0_ref, y1_ref, bo0_ref, bo1_ref, g1_ref, oc_ref,
                    h0_ref, h1_ref, dg_ref, pv_ref, g_ref, g2_ref, mix_ref,
                    o2s_ref, l2s_ref, o3s_ref, l3s_ref, *, tm):
    for d, src, dst in ((4, o2_ref, o2s_ref), (4, l2_ref, l2s_ref), (16, o3_ref, o3s_ref), (16, l3_ref, l3s_ref)):
        for r in range(d):
            for c in range(4):
                lo = r * 512 + c * 128
                dst[c, pl.ds(r, tm // d, stride=d), :] = src[:, lo:lo + 128].astype(F32)
    wide = lambda ref: jnp.concatenate([ref[c] for c in range(4)], axis=1)
    l1, l2, l3 = l1_ref[...], wide(l2s_ref), wide(l3s_ref)
    mx = jnp.maximum(jnp.maximum(l1, l2), l3)
    w1, w2, w3 = jnp.exp(l1 - mx), jnp.exp(l2 - mx), jnp.exp(l3 - mx)
    out_a = (o1_ref[...] * w1 + wide(o2s_ref) * w2 + wide(o3s_ref) * w3) / (w1 + w2 + w3)
    mix_ref[:, 0:512] = out_a.astype(BF16)
    g = g_ref[...]
    y = y0_ref[...] + y1_ref[...]
    mu = _seg_sum(y, g) * (1.0 / HEAD_DIM)
    yc = y - mu
    var = _seg_sum(yc * yc, g) * (1.0 / HEAD_DIM)
    yn = yc * lax.rsqrt(var + LNX_EPS) * pv_ref[0:1, :] + pv_ref[1:2, :]
    gate = jnp.dot(jax.nn.sigmoid(g1_ref[...]).astype(BF16), g2_ref[...], preferred_element_type=F32)
    mix_ref[:, 512:1024] = ((yn + bo0_ref[...] + bo1_ref[...]) * gate).astype(BF16)
    mix_ref[:, 1024:1536] = oc_ref[...]
    o = h0_ref[...].astype(F32) + h1_ref[...].astype(F32)
    dg = dg_ref[...]
    silu = dg * jax.nn.sigmoid(dg)
    for h in range(D_HEADS):
        hs = slice(h * D_HEAD_K, (h + 1) * D_HEAD_K)
        oh = o[:, hs]
        ms = jnp.mean(oh * oh, axis=-1, keepdims=True)
        res = oh * lax.rsqrt(ms + NORM_EPS) * pv_ref[2:3, hs] * silu[:, hs]
        mix_ref[:, 1536 + h * D_HEAD_K:1536 + (h + 1) * D_HEAD_K] = res.astype(BF16)


def _mixpost(a_parts, yb, bonus, p, oc, od, pvec, g512, g2pad, tm=256):
    n = p.shape[0]
    blk = pl.BlockSpec((tm, 512), lambda i: (i, 0))
    const = lambda shape: pl.BlockSpec(shape, lambda i: (0,) * len(shape))
    args = []
    for o, l in a_parts:
        args += [o, l]
    args += [yb[0], yb[1], bonus[0], bonus[1], p, oc, od[0], od[1], p, pvec, g512, g2pad]
    dil = lambda d: pl.BlockSpec((tm // d, d * 512), lambda i: (i, 0))
    in_specs = [blk, blk, dil(4), dil(4), dil(16), dil(16)] + [blk] * 4 + [
        pl.BlockSpec((tm, 128), lambda i: (i, CB_G1)), blk, blk, blk,
        pl.BlockSpec((tm, 512), lambda i: (i, CB_DG)),
        const((8, 512)), const((512, 512)), const((128, 512))]
    return pl.pallas_call(
        functools.partial(_mixpost_kernel, tm=tm),
        grid=(n // tm,),
        in_specs=in_specs,
        out_specs=pl.BlockSpec((tm, D_MODEL), lambda i: (i, 0)),
        out_shape=jax.ShapeDtypeStruct((n, D_MODEL), BF16),
        scratch_shapes=[pltpu.VMEM((4, tm, 128), F32)] * 4,
        compiler_params=_cparams(("parallel",)),
        name="mixpost",
    )(*args)


def _outproj_kernel(mix_ref, w_ref, x_ref, o_ref):
    o_ref[...] = x_ref[...] + jnp.dot(mix_ref[...], w_ref[...], preferred_element_type=F32)


def _outproj(mix, w, x2, tm=512, tn=2048):
    n, d = x2.shape
    return pl.pallas_call(
        _outproj_kernel,
        grid=(n // tm, d // tn),
        in_specs=[pl.BlockSpec((tm, mix.shape[1]), lambda i, j: (i, 0)),
                  pl.BlockSpec((mix.shape[1], tn), lambda i, j: (0, j)),
                  pl.BlockSpec((tm, tn), lambda i, j: (i, j))],
        out_specs=pl.BlockSpec((tm, tn), lambda i, j: (i, j)),
        out_shape=jax.ShapeDtypeStruct((n, d), F32),
        compiler_params=_cparams(("parallel", "arbitrary")),
        name="outproj",
    )(mix, w, x2)


def _rope_tables(seq):
    half = ROT_DIM // 2
    inv = ROPE_THETA ** (-jnp.arange(0, ROT_DIM, 2, dtype=F32) / ROT_DIM)
    ang = jnp.arange(seq, dtype=F32)[:, None] * inv[None]
    cos, sin = jnp.cos(ang), jnp.sin(ang)
    ones = jnp.ones((seq, HEAD_DIM - ROT_DIM), F32)
    zeros = jnp.zeros((seq, HEAD_DIM - ROT_DIM), F32)
    z8 = jnp.zeros((seq, half), F32)
    c = jnp.concatenate([cos, cos, ones], axis=1)
    s1 = jnp.concatenate([-sin, z8, zeros], axis=1)
    s2 = jnp.concatenate([z8, sin, zeros], axis=1)
    tile2 = lambda t: jnp.concatenate([t, t], axis=1)
    return tile2(c), tile2(s1), tile2(s2)


def _block_diag_ones(width, seg):
    idx = jnp.arange(width) // seg
    return (idx[:, None] == idx[None, :]).astype(BF16)


def _pad_rows(rows, width=512, total=8):
    rows = [jnp.pad(r.astype(F32).reshape(-1), (0, width - r.size)) for r in rows]
    rows += [jnp.zeros((width,), F32)] * (total - len(rows))
    return jnp.stack(rows)


def _w_ext(w_in, mu_wa, w1, a1, g1):
    cols_a, cols_b = [], []
    for d in range(2):
        for mu, w in ((mu_wa[d, 0], w1[d]), (mu_wa[d, 1], a1[d])):
            cols_a.append((1.0 - mu)[:, None] * w)
            cols_b.append(mu[:, None] * w)
    parts = [w_in[:, 0:3584], w_in[:, 3840:6400], w_in[:, 3584:3840]] + cols_a + cols_b
    parts.append(jnp.pad(g1, ((0, 0), (0, 128 - g1.shape[1]))))
    w = jnp.concatenate(parts, axis=1)
    return jnp.pad(w, ((0, 0), (0, D_IN_EXT - w.shape[1]))).astype(BF16)


def _pad_lowrank(w, row0):
    return jnp.pad(w, ((row0, 128 - row0 - w.shape[0]), (0, 0))).astype(BF16)


def kernel(x, ln_gain, w_in, w_out, w_ffn_in, w_ffn_out, qk_gain, sink, b_mu_rkv, b_mu_wa, b_w0, b_w1,
           b_w2, b_a0, b_a1, b_a2, b_k_k, b_k_a, b_r_k, b_lnx_gain, b_lnx_bias, b_g1, b_g2,
           d_lb_logits, d_norm_gain):
    batch, seq, d = x.shape
    x2 = x.reshape(batch * seq, d)
    rope_c, rope_s1, rope_s2 = _rope_tables(seq)
    g512 = _block_diag_ones(512, HEAD_DIM)
    pr = jax.nn.softmax(d_lb_logits.astype(F32), axis=0)
    cs = jnp.cumsum(pr, axis=0)
    lb_all = cs - cs[0:1]

    for l in range(DEPTH):
        x2 = _ffn(x2, ln_gain[l, 0][None], w_ffn_in[l, 0].astype(BF16), w_ffn_out[l, 0].astype(BF16))

        p = _inproj(x2, ln_gain[l, 1][None], _w_ext(w_in[l], b_mu_wa[l], b_w1[l], b_a1[l], b_g1[l]))

        tile8 = lambda t: jnp.tile(t, 8)
        gains = _pad_rows([tile8(qk_gain[l, 0, 0]), tile8(qk_gain[l, 0, 1]),
                           tile8(qk_gain[l, 1, 0]), jnp.tile(qk_gain[l, 1, 1], 2)])
        prepped = _prep(p, rope_c, rope_s1, rope_s2, gains, g512, seq)
        qc, kc, vc = prepped[9:]

        a_parts = []
        for pi, (window, dil) in enumerate(DILATED_PATTERNS):
            qa, ka, va = prepped[3 * pi:3 * pi + 3]
            a_parts.append(_band_attention(qa, ka, va, None, batch, seq, dil, window // (2 * dil),
                                           GROUP_WIDTH // HEAD_DIM, 1))
        oc = _band_attention(qc, kc, vc, sink[l][None].astype(F32), batch, seq, 1, C_HALF_WINDOW,
                             C_KV_HEADS, C_Q_HEADS // C_KV_HEADS)

        yb, bonus = [], []
        for dr in range(2):
            pvec = _pad_rows([b_mu_rkv[l, dr, 0], b_mu_rkv[l, dr, 1], b_mu_rkv[l, dr, 2], b_w0[l, dr],
                              b_a0[l, dr], b_k_k[l], b_k_a[l], b_r_k[l]])
            y, bo = _rwkv(p, pvec, _pad_lowrank(b_w2[l, dr], 64 * dr), _pad_lowrank(b_a2[l, dr], 64 * dr + 32),
                          g512, batch, seq, reverse=(dr == 1))
            yb.append(y)
            bonus.append(bo)

        od = []
        for dr in range(2):
            lb = lb_all[l, dr]
            od.append(_hgrn(p, _pad_rows([jnp.log(lb), jnp.log1p(-lb)]), batch, seq, reverse=(dr == 1)))

        pvec = _pad_rows([b_lnx_gain[l], b_lnx_bias[l], d_norm_gain[l]])
        mix = _mixpost(a_parts, yb, bonus, p, oc, od, pvec, g512, _pad_lowrank(b_g2[l], 0))
        x2 = _outproj(mix, w_out[l].astype(BF16), x2)

        x2 = _ffn(x2, ln_gain[l, 2][None], w_ffn_in[l, 1].astype(BF16), w_ffn_out[l, 1].astype(BF16))
    return x2.reshape(batch, seq, d)
```

```python
import functools

import jax
import jax.numpy as jnp
from jax import lax
from jax.experimental import pallas as pl
from jax.experimental.pallas import tpu as pltpu

F32 = jnp.float32
BF16 = jnp.bfloat16

D_MODEL = 2048
DEPTH = 4
HEAD_DIM = 64
GROUP_WIDTH = 512
DILATED_PATTERNS = ((128, 1), (512, 4), (2048, 16))
LNX_EPS = 64e-5
C_Q_HEADS = 8
C_KV_HEADS = 2
C_HALF_WINDOW = 128
D_HEAD_K = 128
D_HEADS = 4
D_FF = 5632
ROPE_THETA = 500000.0
ROT_DIM = 16
NORM_EPS = 1e-6
NEG_INF = -1e30
LOG2_E = 1.4426950408889634

D_IN_EXT = 7168
(CB_AQ, CB_AK, CB_AV, CB_BR, CB_BK, CB_BV, CB_CQ, CB_DQ, CB_DI, CB_DG, CB_DFF, CB_DFB) = range(12)
CB_CK, CB_CV, CB_SECA, CB_SECB, CB_G1 = 48, 49, 50, 51, 52

VMEM_LIMIT = 56 * 1024 * 1024
RWKV_CHUNK = 64
RWKV_BLOCK = 256
HGRN_CHUNK = 64
ATTN_SUB_ROWS = 256
HGRN_BLOCK = 256
HGRN_SUB = 16


def _cparams(sem):
    return pltpu.CompilerParams(dimension_semantics=sem, vmem_limit_bytes=VMEM_LIMIT)


def _seg_sum(x, g):
    hi = x.astype(BF16)
    lo = (x - hi.astype(F32)).astype(BF16)
    d = lambda a: jnp.dot(a, g, preferred_element_type=F32)
    return d(hi) + d(lo)


def _dot(a, b):
    return jnp.dot(a.astype(BF16), b.astype(BF16), preferred_element_type=F32)


def _dot_nt(a, b):
    return lax.dot_general(a.astype(BF16), b.astype(BF16), (((1,), (1,)), ((), ())),
                           preferred_element_type=F32)


def _dot_tn(a, b):
    return lax.dot_general(a.astype(BF16), b.astype(BF16), (((0,), (0,)), ((), ())),
                           preferred_element_type=F32)


def _softplus(u):
    return jnp.maximum(u, 0.0) + jnp.log(1.0 + jnp.exp(-jnp.abs(u)))


def _scan_cumsum(x, n, reverse, seg=None):
    seg = n if seg is None else seg
    row = lax.broadcasted_iota(jnp.int32, x.shape, 0) & (seg - 1)
    s = 1
    while s < seg:
        if reverse:
            x = x + jnp.where(row < seg - s, pltpu.roll(x, n - s, 0), 0.0)
        else:
            x = x + jnp.where(row >= s, pltpu.roll(x, s, 0), 0.0)
        s *= 2
    return x


def _scan_shift(x, carry_row, n, reverse):
    row = lax.broadcasted_iota(jnp.int32, x.shape, 0)
    if reverse:
        return jnp.where(row == n - 1, carry_row, pltpu.roll(x, n - 1, 0))
    return jnp.where(row == 0, carry_row, pltpu.roll(x, 1, 0))


def _ffn_kernel(x_ref, g_ref, wg_ref, wu_ref, wo_ref, o_ref, hn_ref):
    @pl.when(pl.program_id(1) == 0)
    def _():
        x = x_ref[...]
        ms = jnp.mean(x * x, axis=-1, keepdims=True)
        hn_ref[...] = (x * lax.rsqrt(ms + NORM_EPS) * g_ref[...]).astype(BF16)
        o_ref[...] = x

    h = hn_ref[...]
    gate = jnp.dot(h, wg_ref[...], preferred_element_type=F32)
    up = jnp.dot(h, wu_ref[...], preferred_element_type=F32)
    act = (0.5 * gate * jax.nn.sigmoid(gate) * up).astype(BF16)
    o_ref[...] += jnp.dot(act, wo_ref[...], preferred_element_type=F32)


def _ffn(x2, gain, w_in, w_out, tm=1024, tf=512):
    n, d = x2.shape
    nf = D_FF // tf
    return pl.pallas_call(
        _ffn_kernel,
        grid=(n // tm, nf),
        in_specs=[
            pl.BlockSpec((tm, d), lambda i, j: (i, 0)),
            pl.BlockSpec((1, d), lambda i, j: (0, 0)),
            pl.BlockSpec((d, tf), lambda i, j: (0, j)),
            pl.BlockSpec((d, tf), lambda i, j: (0, j + nf)),
            pl.BlockSpec((tf, d), lambda i, j: (j, 0)),
        ],
        out_specs=pl.BlockSpec((tm, d), lambda i, j: (i, 0)),
        out_shape=jax.ShapeDtypeStruct((n, d), F32),
        scratch_shapes=[pltpu.VMEM((tm, d), BF16)],
        compiler_params=_cparams(("parallel", "arbitrary")),
        name="ffn",
    )(x2, gain, w_in, w_in, w_out)


def _inproj_kernel(x_ref, g_ref, w_ref, o_ref, hn_ref):
    @pl.when(pl.program_id(1) == 0)
    def _():
        x = x_ref[...]
        ms = jnp.mean(x * x, axis=-1, keepdims=True)
        hn_ref[...] = (x * lax.rsqrt(ms + NORM_EPS) * g_ref[...]).astype(BF16)

    o_ref[...] = jnp.dot(hn_ref[...], w_ref[...], preferred_element_type=F32)


def _inproj(x2, gain, w_ext, tm=1024, tn=1024):
    n, d = x2.shape
    nc = w_ext.shape[1]
    return pl.pallas_call(
        _inproj_kernel,
        grid=(n // tm, nc // tn),
        in_specs=[
            pl.BlockSpec((tm, d), lambda i, j: (i, 0)),
            pl.BlockSpec((1, d), lambda i, j: (0, 0)),
            pl.BlockSpec((d, tn), lambda i, j: (0, j)),
        ],
        out_specs=pl.BlockSpec((tm, tn), lambda i, j: (i, j)),
        out_shape=jax.ShapeDtypeStruct((n, nc), F32),
        scratch_shapes=[pltpu.VMEM((tm, d), BF16)],
        compiler_params=_cparams(("parallel", "arbitrary")),
        name="inproj",
    )(x2, gain, w_ext)


def _norm_rope(t, gain, g, c, s1, s2, scale):
    w = t.shape[-1]
    ss = _seg_sum(t * t, g)
    y = t * lax.rsqrt(ss * (1.0 / HEAD_DIM) + NORM_EPS) * gain
    half = ROT_DIM // 2
    out = y * c + pltpu.roll(y, w - half, 1) * s1 + pltpu.roll(y, half, 1) * s2
    return out * scale if scale != 1.0 else out


def _prep_kernel(aq_ref, ak_ref, av_ref, cq_ref, ck_ref, cv_ref, c_ref, s1_ref, s2_ref,
                 gain_ref, g_ref,
                 qa_ref, ka_ref, va_ref, qa4_ref, ka4_ref, va4_ref, qa16_ref, ka16_ref, va16_ref,
                 qc_ref, kc_ref, vc_ref, scr_ref, *, tm):
    c1, s11, s21 = c_ref[...], s1_ref[...], s2_ref[...]
    c4 = jnp.concatenate([c1] * 4, axis=1)
    s14 = jnp.concatenate([s11] * 4, axis=1)
    s24 = jnp.concatenate([s21] * 4, axis=1)
    g = g_ref[...]
    g1 = g_ref[0:128, 0:128]
    scale = HEAD_DIM ** -0.5

    def emit(y, nat_ref, dil_refs):
        nat_ref[...] = y.astype(BF16)
        for c in range(4):
            scr_ref[c] = y[:, c * 128:(c + 1) * 128]
        for d, ref in dil_refs:
            for r in range(d):
                for c in range(4):
                    lo = r * 512 + c * 128
                    ref[:, lo:lo + 128] = scr_ref[c, pl.ds(r, tm // d, stride=d), :].astype(BF16)

    emit(_norm_rope(aq_ref[...], gain_ref[0:1, :], g, c4, s14, s24, scale), qa_ref,
         ((4, qa4_ref), (16, qa16_ref)))
    emit(_norm_rope(ak_ref[...], gain_ref[1:2, :], g, c4, s14, s24, 1.0), ka_ref,
         ((4, ka4_ref), (16, ka16_ref)))
    emit(av_ref[...], va_ref, ((4, va4_ref), (16, va16_ref)))
    qc_ref[...] = _norm_rope(cq_ref[...], gain_ref[2:3, :], g, c4, s14, s24, scale).astype(BF16)
    kc_ref[...] = _norm_rope(ck_ref[...], gain_ref[3:4, 0:128], g1, c1, s11, s21, 1.0).astype(BF16)
    vc_ref[...] = cv_ref[...].astype(BF16)


def _prep(p, rope_c, rope_s1, rope_s2, gains, g512, seq, tm=512):
    n = p.shape[0]
    nseq = seq // tm
    wide = lambda cb: pl.BlockSpec((tm, 512), lambda i, cb=cb: (i, cb))
    narrow = lambda cb: pl.BlockSpec((tm, 128), lambda i, cb=cb: (i, cb))
    tab = pl.BlockSpec((tm, 128), lambda i: (i % nseq, 0))
    rowblk = lambda rows, width: pl.BlockSpec((rows, width), lambda i: (i, 0))
    shape = lambda rows, width: jax.ShapeDtypeStruct((rows, width), BF16)
    trio = lambda d: [rowblk(tm // d, d * 512)] * 3
    trio_shape = lambda d: [shape(n // d, d * 512)] * 3
    return pl.pallas_call(
        functools.partial(_prep_kernel, tm=tm),
        grid=(n // tm,),
        in_specs=[wide(CB_AQ), wide(CB_AK), wide(CB_AV), wide(CB_CQ), narrow(CB_CK), narrow(CB_CV),
                  tab, tab, tab,
                  pl.BlockSpec((8, 512), lambda i: (0, 0)),
                  pl.BlockSpec((512, 512), lambda i: (0, 0))],
        out_specs=trio(1) + trio(4) + trio(16) + [rowblk(tm, 512), rowblk(tm, 128), rowblk(tm, 128)],
        out_shape=trio_shape(1) + trio_shape(4) + trio_shape(16) + [shape(n, 512), shape(n, 128), shape(n, 128)],
        scratch_shapes=[pltpu.VMEM((4, tm, 128), F32)],
        compiler_params=_cparams(("parallel",)),
        name="prep_qk",
    )(p, p, p, p, p, p, rope_c, rope_s1, rope_s2, gains, g512)


def _band_kernel(*refs, tq, hb, sb, half, length, n_kv, group, with_sink):
    if with_sink:
        q_ref, kp_ref, km_ref, kn_ref, vp_ref, vm_ref, vn_ref, sink_ref, o_ref = refs
    else:
        q_ref, kp_ref, km_ref, kn_ref, vp_ref, vm_ref, vn_ref, o_ref, lse_ref = refs
    i = pl.program_id(2)
    nsub = tq // sb
    gs, nwin = group * sb, sb + 2 * hb
    row = lax.broadcasted_iota(jnp.int32, (gs, nwin), 0) & (sb - 1)
    col = lax.broadcasted_iota(jnp.int32, (gs, nwin), 1)
    band = jnp.abs(row + hb - col) <= half
    valid = []
    for j in range(nsub):
        kpos = i * tq + j * sb - hb + col
        valid.append(band & (kpos >= 0) & (kpos < length))
    q_all = q_ref[0]
    n_heads = n_kv * group
    hsl = [slice(h * HEAD_DIM, (h + 1) * HEAD_DIM) for h in range(n_heads)]
    k_heads = [jnp.concatenate([kp_ref[0, :, s], km_ref[0, :, s], kn_ref[0, :, s]], axis=0) for s in hsl[:n_kv]]
    v_heads = [jnp.concatenate([vp_ref[0, :, s], vm_ref[0, :, s], vn_ref[0, :, s]], axis=0) for s in hsl[:n_kv]]
    kvs = range(n_kv)
    if with_sink:
        sinks = sink_ref[...]
        rowg = lax.broadcasted_iota(jnp.int32, (gs, 1), 0) // sb
        sink_col = []
        for kv in kvs:
            col_kv = jnp.zeros((gs, 1), F32)
            for g in range(group):
                h = kv * group + g
                col_kv = jnp.where(rowg == g, sinks[0:1, h:h + 1], col_kv)
            sink_col.append(col_kv)
    outs, lses = [], []
    for j in range(nsub):
        rows, win = slice(j * sb, (j + 1) * sb), slice(j * sb, (j + 1) * sb + 2 * hb)
        qs = [jnp.concatenate([q_all[rows, hsl[kv * group + g]] for g in range(group)], axis=0) for kv in kvs]
        s = [lax.dot_general(qs[kv], k_heads[kv][win], (((1,), (1,)), ((), ())),
                             preferred_element_type=F32) for kv in kvs]
        s = [jnp.where(valid[j], x, NEG_INF) for x in s]
        m = [jnp.max(x, axis=-1, keepdims=True) for x in s]
        p = [jnp.exp(x - y) for x, y in zip(s, m)]
        l = [jnp.sum(x, axis=-1, keepdims=True) for x in p]
        acc = [jnp.dot(p[kv].astype(BF16), v_heads[kv][win], preferred_element_type=F32) for kv in kvs]
        if with_sink:
            m_all = [jnp.maximum(x, y) for x, y in zip(m, sink_col)]
            sc = [jnp.exp(x - y) for x, y in zip(m, m_all)]
            den = [l[kv] * sc[kv] + jnp.exp(sink_col[kv] - m_all[kv]) for kv in kvs]
            o = [acc[kv] * (sc[kv] / den[kv]) for kv in kvs]
        else:
            o = [x / y for x, y in zip(acc, l)]
            lse = [x + jnp.log(y) for x, y in zip(m, l)]
            lses.append(jnp.concatenate(
                [jnp.broadcast_to(lse[h // group][(h % group) * sb:(h % group + 1) * sb], (sb, HEAD_DIM))
                 for h in range(n_heads)], axis=1))
        outs.append(jnp.concatenate([o[h // group][(h % group) * sb:(h % group + 1) * sb]
                                     for h in range(n_heads)], axis=1))
    for j in range(nsub):
        o_ref[0, j * sb:(j + 1) * sb, :] = outs[j].astype(o_ref.dtype)
        if not with_sink:
            lse_ref[0, j * sb:(j + 1) * sb, :] = lses[j]


def _band_attention(q, k, v, sink, batch, seq, dil, half, n_kv, group, tq=512):
    wq, wk = q.shape[1] // dil, k.shape[1] // dil
    hb = half
    sl = seq // dil
    tq = min(tq, sl)
    nq = sl // tq
    per = tq // hb
    nhb = sl // hb
    qv = q.reshape(batch, sl, dil * wq)
    kv = k.reshape(batch, sl, dil * wk)
    vv = v.reshape(batch, sl, dil * wk)
    main = lambda w: pl.BlockSpec((1, tq, w), lambda b, r, i: (b, i, r))
    prev = pl.BlockSpec((1, hb, wk), lambda b, r, i: (b, jnp.maximum(i * per - 1, 0), r))
    nxt = pl.BlockSpec((1, hb, wk), lambda b, r, i: (b, jnp.minimum((i + 1) * per, nhb - 1), r))
    with_sink = sink is not None
    in_specs = [main(wq), prev, main(wk), nxt, prev, main(wk), nxt]
    args = [qv, kv, kv, kv, vv, vv, vv]
    if with_sink:
        in_specs.append(pl.BlockSpec((1, wq // HEAD_DIM), lambda b, r, i: (0, 0)))
        args.append(sink)
        out_specs = main(wq)
        out_shape = jax.ShapeDtypeStruct(qv.shape, BF16)
    else:
        out_specs = [main(wq), main(wq)]
        out_shape = [jax.ShapeDtypeStruct(qv.shape, BF16), jax.ShapeDtypeStruct(qv.shape, F32)]
    kern = functools.partial(_band_kernel, tq=tq, hb=hb, sb=min(tq, max(hb, ATTN_SUB_ROWS // group)),
                             half=half, length=sl,
                             n_kv=n_kv, group=group, with_sink=with_sink)
    out = pl.pallas_call(
        kern,
        grid=(batch, dil, nq),
        in_specs=in_specs,
        out_specs=out_specs,
        out_shape=out_shape,
        compiler_params=_cparams(("parallel", "parallel", "parallel")),
        name="band_attn_sink" if with_sink else f"band_attn_d{dil}",
    )(*args)
    flat = lambda t: t.reshape(batch * sl, dil * wq)
    if with_sink:
        return flat(out)
    return flat(out[0]), flat(out[1])


def _rwkv_kernel(r_ref, k_ref, v_ref, sa_ref, sb_ref, pv_ref, w2_ref, a2_ref, g_ref,
                 y_ref, bonus_ref, state_ref, carry_ref, carryb_ref, *, n, cs, reverse):
    @pl.when(pl.program_id(1) == 0)
    def _():
        state_ref[...] = jnp.zeros_like(state_ref)
        carry_ref[...] = jnp.zeros_like(carry_ref)
        carryb_ref[...] = jnp.zeros_like(carryb_ref)

    last = 0 if reverse else n - 1
    r0, k0, v0, sb = r_ref[...], k_ref[...], v_ref[...], sb_ref[...]
    shift = lambda x, c: _scan_shift(x, c, n, reverse)
    r = r0 + (shift(r0, carry_ref[0:1, :]) - r0) * pv_ref[0:1, :]
    k = k0 + (shift(k0, carry_ref[1:2, :]) - k0) * pv_ref[1:2, :]
    v = v0 + (shift(v0, carry_ref[2:3, :]) - v0) * pv_ref[2:3, :]
    xlr = sa_ref[...] + shift(sb, carryb_ref[0:1, :])
    carry_ref[0:1, :] = r0[last:last + 1, :]
    carry_ref[1:2, :] = k0[last:last + 1, :]
    carry_ref[2:3, :] = v0[last:last + 1, :]
    carryb_ref[0:1, :] = sb[last:last + 1, :]

    wl = pv_ref[3:4, :] + jnp.dot(jnp.tanh(xlr).astype(BF16), w2_ref[...], preferred_element_type=F32)
    w_log = -_softplus(-wl) - 0.5
    ld = -jnp.exp(w_log)
    a = jax.nn.sigmoid(pv_ref[4:5, :] + jnp.dot(xlr.astype(BF16), a2_ref[...], preferred_element_type=F32))
    g = g_ref[...]
    kkr = k * pv_ref[5:6, :]
    kx = k * (1.0 + (a - 1.0) * pv_ref[6:7, :])
    sums = _seg_sum(jnp.concatenate([kkr * kkr, r * kx * pv_ref[7:8, :]], axis=0), g)
    kk = kkr / jnp.maximum(jnp.sqrt(sums[:n]), 1e-12)
    bonus_ref[...] = (sums[n:] * v).astype(bonus_ref.dtype)

    nsub = n // cs
    last_row = lambda gi: gi * cs if reverse else (gi + 1) * cs - 1
    row1 = lax.broadcasted_iota(jnp.int32, (n, 1), 0)
    c = _scan_cumsum(ld, n, reverse, seg=cs)
    c_last = c[last_row(0):last_row(0) + 1, :]
    for gi in range(1, nsub):
        c_last = jnp.where(row1 >= gi * cs, c[last_row(gi):last_row(gi) + 1, :], c_last)
    e_c = jnp.exp(c)
    e_cx = jnp.exp(c - ld)
    e_nc = jnp.exp(-c)
    e_end = jnp.exp(c_last - c)
    kb = kk * a
    rt = r * e_c
    at = -kk * e_cx
    bt = kb * e_nc
    kt = kx * e_nc
    bh = kb * e_end
    kh = kx * e_end

    rt_b, at_b, bt_b, kt_b, v_b = (t.astype(BF16) for t in (rt, at, bt, kt, v))
    bh_b, kh_b = bh.astype(BF16), kh.astype(BF16)

    row = lax.broadcasted_iota(jnp.int32, (2 * cs, 2 * cs), 0)
    col = lax.broadcasted_iota(jnp.int32, (2 * cs, 2 * cs), 1)
    rr, cc = row & (cs - 1), col & (cs - 1)
    strict = (cc > rr) if reverse else (cc < rr)
    mask = strict | ((cc == rr) & (row >= cs))
    eye = (lax.broadcasted_iota(jnp.int32, (cs, cs), 0) == lax.broadcasted_iota(jnp.int32, (cs, cs), 1)).astype(F32)

    heads = range(GROUP_WIDTH // HEAD_DIM)
    nh = len(heads)
    cat = lambda a, b: jnp.concatenate([a, b], axis=0)
    pairs = [(slice(gi * cs, (gi + 1) * cs), slice(h * HEAD_DIM, (h + 1) * HEAD_DIM))
             for gi in range(nsub) for h in heads]
    big = [jnp.where(mask, _dot_nt(cat(at_b[rs, s], rt_b[rs, s]), cat(bt_b[rs, s], kt_b[rs, s])), 0.0)
           for rs, s in pairs]
    akv = [_dot(b[:cs, cs:], v_b[rs, s]) for b, (rs, s) in zip(big, pairs)]
    a_ab = [b[:cs, :cs] for b in big]
    t_inv = [eye + x for x in a_ab]
    pw = [_dot(x, x) for x in a_ab]
    lvl = 4
    while lvl < cs:
        res = [_dot(cat(x, t), x) for x, t in zip(pw, t_inv)]
        t_inv = [t + x[cs:] for t, x in zip(t_inv, res)]
        pw = [x[:cs] for x in res]
        lvl *= 2
    t_inv = [t + _dot(t, x) for t, x in zip(t_inv, pw)]
    ta = [_dot(t, at_b[rs, s]) for t, (rs, s) in zip(t_inv, pairs)]
    tv = [_dot(t, x) for t, x in zip(t_inv, akv)]

    st = [state_ref[h] for h in heads]
    ys = [None] * nsub
    for gi in (range(nsub - 1, -1, -1) if reverse else range(nsub)):
        wc = jnp.exp(c[last_row(gi):last_row(gi) + 1, :])
        sl = slice(gi * nh, (gi + 1) * nh)
        u = [_dot_nt(x, s0) + z for x, s0, z in zip(ta[sl], st, tv[sl])]
        uv = [cat(x, v[rs, s]).astype(BF16) for x, (rs, s) in zip(u, pairs[sl])]
        ys[gi] = [_dot_nt(rt_b[rs, s], s0) + _dot(b[cs:, :], x)
                  for (rs, s), s0, b, x in zip(pairs[sl], st, big[sl], uv)]
        st = [s0 * wc[:, s] + _dot_tn(x, cat(bh_b[rs, s], kh_b[rs, s]))
              for s0, (rs, s), x in zip(st, pairs[sl], uv)]
    for gi in range(nsub):
        for h in heads:
            rs, s = pairs[gi * nh + h]
            y_ref[rs, s] = ys[gi][h]
    for h in heads:
        state_ref[h] = st[h]


def _rwkv(p, pvec, w2pad, a2pad, g512, batch, seq, reverse):
    n = p.shape[0]
    c = RWKV_BLOCK
    nc = seq // c
    if reverse:
        rowidx = lambda b, j: b * nc + (nc - 1 - j)
    else:
        rowidx = lambda b, j: b * nc + j
    wide = lambda cb: pl.BlockSpec((c, 512), lambda b, j, cb=cb: (rowidx(b, j), cb))
    narrow = lambda cb: pl.BlockSpec((c, 128), lambda b, j, cb=cb: (rowidx(b, j), cb))
    const = lambda shape: pl.BlockSpec(shape, lambda b, j: (0,) * len(shape))
    out = pl.BlockSpec((c, 512), lambda b, j: (rowidx(b, j), 0))
    return pl.pallas_call(
        functools.partial(_rwkv_kernel, n=c, cs=RWKV_CHUNK, reverse=reverse),
        grid=(batch, nc),
        in_specs=[wide(CB_BR), wide(CB_BK), wide(CB_BV), narrow(CB_SECA), narrow(CB_SECB),
                  const((8, 512)), const((128, 512)), const((128, 512)), const((512, 512))],
        out_specs=[out, out],
        out_shape=[jax.ShapeDtypeStruct((n, 512), F32), jax.ShapeDtypeStruct((n, 512), BF16)],
        scratch_shapes=[pltpu.VMEM((8, HEAD_DIM, HEAD_DIM), F32), pltpu.VMEM((8, 512), F32),
                        pltpu.VMEM((8, 128), F32)],
        compiler_params=_cparams(("parallel", "arbitrary")),
        name="rwkv_bwd" if reverse else "rwkv_fwd",
    )(p, p, p, p, p, pvec, w2pad, a2pad, g512)


def _hgrn_kernel(q_ref, v_ref, z_ref, lb_ref, o_ref, state_ref, *, n, cs, sub, reverse):
    @pl.when(pl.program_id(1) == 0)
    def _():
        state_ref[...] = jnp.zeros_like(state_ref)

    q, v, z = q_ref[...], v_ref[...], z_ref[...]
    log_sig = -_softplus(-z)
    x1 = lb_ref[0:1, :]
    x2 = lb_ref[1:2, :] + log_sig
    logf = jnp.maximum(x1, x2) + jnp.log(1.0 + jnp.exp(-jnp.abs(x1 - x2)))
    kf = 1.0 - jnp.exp(logf)
    nchunk, nsub = n // cs, cs // sub
    last_row = lambda gi: gi * cs if reverse else (gi + 1) * cs - 1
    row1 = lax.broadcasted_iota(jnp.int32, (n, 1), 0)
    b = _scan_cumsum(logf, n, reverse, seg=cs)
    b_last = b[last_row(0):last_row(0) + 1, :]
    for gi in range(1, nchunk):
        b_last = jnp.where(row1 >= gi * cs, b[last_row(gi):last_row(gi) + 1, :], b_last)
    qe = q * jnp.exp(b)
    ke = kf * jnp.exp(b_last - b)
    b2 = b * LOG2_E
    c2 = b2 - jnp.log(kf) * LOG2_E
    srow = lax.broadcasted_iota(jnp.int32, (sub, sub), 0)
    lane = lax.broadcasted_iota(jnp.int32, (sub, sub), 1)
    tri = (srow <= lane) if reverse else (srow >= lane)
    hsl = [slice(h * D_HEAD_K, (h + 1) * D_HEAD_K) for h in range(D_HEADS)]

    intra = {}
    for gi in range(nchunk):
        base = gi * cs
        for h, hs in enumerate(hsl):
            for m in range(nsub):
                ms = slice(base + m * sub, base + (m + 1) * sub)
                q_m, b_m = q[ms, hs], b[ms, hs]
                b2_m, c2_m = b2[ms, hs], c2[ms, hs]
                dmat = jnp.zeros((sub, sub), F32)
                for s in range(sub):
                    w = q_m * jnp.exp2(b2_m - c2_m[s:s + 1])
                    dmat = jnp.where(lane == s, jnp.sum(w, axis=-1, keepdims=True), dmat)
                dmat = jnp.where(tri, dmat, 0.0)
                o_m = _dot(dmat, v[ms, hs])
                es = slice(base + (m + 1) * sub, base + cs) if reverse else slice(base, base + m * sub)
                if es.stop > es.start:
                    first = base + ((m + 1) * sub - 1 if reverse else m * sub)
                    rho = b[first:first + 1, hs] - logf[first:first + 1, hs]
                    qt = q_m * jnp.exp(b_m - rho)
                    kt = kf[es, hs] * jnp.exp(rho - b[es, hs])
                    o_m = o_m + _dot(_dot_nt(qt, kt), v[es, hs])
                intra[gi, h, m] = o_m

    st = [state_ref[h] for h in range(D_HEADS)]
    inter = {}
    for gi in (range(nchunk - 1, -1, -1) if reverse else range(nchunk)):
        rs = slice(gi * cs, (gi + 1) * cs)
        wend = jnp.exp(b[last_row(gi):last_row(gi) + 1, :])
        for h, hs in enumerate(hsl):
            inter[gi, h] = _dot_nt(qe[rs, hs], st[h])
        st = [st[h] * wend[:, hs] + _dot_tn(v[rs, hs], ke[rs, hs]) for h, hs in enumerate(hsl)]
    for (gi, h, m), o_m in intra.items():
        lo = gi * cs + m * sub
        o_ref[lo:lo + sub, hsl[h]] = (o_m + inter[gi, h][m * sub:(m + 1) * sub]).astype(o_ref.dtype)
    for h in range(D_HEADS):
        state_ref[h] = st[h]


def _hgrn(p, lbvec, batch, seq, reverse):
    n = p.shape[0]
    c = HGRN_BLOCK
    nc = seq // c
    if reverse:
        rowidx = lambda b, j: b * nc + (nc - 1 - j)
    else:
        rowidx = lambda b, j: b * nc + j
    wide = lambda cb: pl.BlockSpec((c, 512), lambda b, j, cb=cb: (rowidx(b, j), cb))
    return pl.pallas_call(
        functools.partial(_hgrn_kernel, n=c, cs=HGRN_CHUNK, sub=HGRN_SUB, reverse=reverse),
        grid=(batch, nc),
        in_specs=[wide(CB_DQ), wide(CB_DI), wide(CB_DFB if reverse else CB_DFF),
                  pl.BlockSpec((8, 512), lambda b, j: (0, 0))],
        out_specs=pl.BlockSpec((c, 512), lambda b, j: (rowidx(b, j), 0)),
        out_shape=jax.ShapeDtypeStruct((n, 512), BF16),
        scratch_shapes=[pltpu.VMEM((D_HEADS, D_HEAD_K, D_HEAD_K), F32)],
        compiler_params=_cparams(("parallel", "arbitrary")),
        name="hgrn_bwd" if reverse else "hgrn_fwd",
    )(p, p, p, lbvec)


def _mixpost_kernel(o1_ref, l1_ref, o2_ref, l2_ref, o3_ref, l3_ref,
                    y0_ref, y1_ref, bo0_ref, bo1_ref, g1_ref, oc_ref,
                    h0_ref, h1_ref, dg_ref, pv_ref, g_ref, g2_ref, mix_ref,
                    o2s_ref, l2s_ref, o3s_ref, l3s_ref, *, tm):
    for d, src, dst in ((4, o2_ref, o2s_ref), (4, l2_ref, l2s_ref), (16, o3_ref, o3s_ref), (16, l3_ref, l3s_ref)):
        for r in range(d):
            for c in range(4):
                lo = r * 512 + c * 128
                dst[c, pl.ds(r, tm // d, stride=d), :] = src[:, lo:lo + 128].astype(F32)
    wide = lambda ref: jnp.concatenate([ref[c] for c in range(4)], axis=1)
    l1, l2, l3 = l1_ref[...], wide(l2s_ref), wide(l3s_ref)
    mx = jnp.maximum(jnp.maximum(l1, l2), l3)
    w1, w2, w3 = jnp.exp(l1 - mx), jnp.exp(l2 - mx), jnp.exp(l3 - mx)
    out_a = (o1_ref[...] * w1 + wide(o2s_ref) * w2 + wide(o3s_ref) * w3) / (w1 + w2 + w3)
    mix_ref[:, 0:512] = out_a.astype(BF16)
    g = g_ref[...]
    y = y0_ref[...] + y1_ref[...]
    mu = _seg_sum(y, g) * (1.0 / HEAD_DIM)
    yc = y - mu
    var = _seg_sum(yc * yc, g) * (1.0 / HEAD_DIM)
    yn = yc * lax.rsqrt(var + LNX_EPS) * pv_ref[0:1, :] + pv_ref[1:2, :]
    gate = jnp.dot(jax.nn.sigmoid(g1_ref[...]).astype(BF16), g2_ref[...], preferred_element_type=F32)
    mix_ref[:, 512:1024] = ((yn + bo0_ref[...] + bo1_ref[...]) * gate).astype(BF16)
    mix_ref[:, 1024:1536] = oc_ref[...]
    o = h0_ref[...].astype(F32) + h1_ref[...].astype(F32)
    dg = dg_ref[...]
    silu = dg * jax.nn.sigmoid(dg)
    for h in range(D_HEADS):
        hs = slice(h * D_HEAD_K, (h + 1) * D_HEAD_K)
        oh = o[:, hs]
        ms = jnp.mean(oh * oh, axis=-1, keepdims=True)
        res = oh * lax.rsqrt(ms + NORM_EPS) * pv_ref[2:3, hs] * silu[:, hs]
        mix_ref[:, 1536 + h * D_HEAD_K:1536 + (h + 1) * D_HEAD_K] = res.astype(BF16)


def _mixpost(a_parts, yb, bonus, p, oc, od, pvec, g512, g2pad, tm=256):
    n = p.shape[0]
    blk = pl.BlockSpec((tm, 512), lambda i: (i, 0))
    const = lambda shape: pl.BlockSpec(shape, lambda i: (0,) * len(shape))
    args = []
    for o, l in a_parts:
        args += [o, l]
    args += [yb[0], yb[1], bonus[0], bonus[1], p, oc, od[0], od[1], p, pvec, g512, g2pad]
    dil = lambda d: pl.BlockSpec((tm // d, d * 512), lambda i: (i, 0))
    in_specs = [blk, blk, dil(4), dil(4), dil(16), dil(16)] + [blk] * 4 + [
        pl.BlockSpec((tm, 128), lambda i: (i, CB_G1)), blk, blk, blk,
        pl.BlockSpec((tm, 512), lambda i: (i, CB_DG)),
        const((8, 512)), const((512, 512)), const((128, 512))]
    return pl.pallas_call(
        functools.partial(_mixpost_kernel, tm=tm),
        grid=(n // tm,),
        in_specs=in_specs,
        out_specs=pl.BlockSpec((tm, D_MODEL), lambda i: (i, 0)),
        out_shape=jax.ShapeDtypeStruct((n, D_MODEL), BF16),
        scratch_shapes=[pltpu.VMEM((4, tm, 128), F32)] * 4,
        compiler_params=_cparams(("parallel",)),
        name="mixpost",
    )(*args)


def _outproj_kernel(mix_ref, w_ref, x_ref, o_ref):
    o_ref[...] = x_ref[...] + jnp.dot(mix_ref[...], w_ref[...], preferred_element_type=F32)


def _outproj(mix, w, x2, tm=512, tn=2048):
    n, d = x2.shape
    return pl.pallas_call(
        _outproj_kernel,
        grid=(n // tm, d // tn),
        in_specs=[pl.BlockSpec((tm, mix.shape[1]), lambda i, j: (i, 0)),
                  pl.BlockSpec((mix.shape[1], tn), lambda i, j: (0, j)),
                  pl.BlockSpec((tm, tn), lambda i, j: (i, j))],
        out_specs=pl.BlockSpec((tm, tn), lambda i, j: (i, j)),
        out_shape=jax.ShapeDtypeStruct((n, d), F32),
        compiler_params=_cparams(("parallel", "arbitrary")),
        name="outproj",
    )(mix, w, x2)


def _rope_tables(seq):
    half = ROT_DIM // 2
    inv = ROPE_THETA ** (-jnp.arange(0, ROT_DIM, 2, dtype=F32) / ROT_DIM)
    ang = jnp.arange(seq, dtype=F32)[:, None] * inv[None]
    cos, sin = jnp.cos(ang), jnp.sin(ang)
    ones = jnp.ones((seq, HEAD_DIM - ROT_DIM), F32)
    zeros = jnp.zeros((seq, HEAD_DIM - ROT_DIM), F32)
    z8 = jnp.zeros((seq, half), F32)
    c = jnp.concatenate([cos, cos, ones], axis=1)
    s1 = jnp.concatenate([-sin, z8, zeros], axis=1)
    s2 = jnp.concatenate([z8, sin, zeros], axis=1)
    tile2 = lambda t: jnp.concatenate([t, t], axis=1)
    return tile2(c), tile2(s1), tile2(s2)


def _block_diag_ones(width, seg):
    idx = jnp.arange(width) // seg
    return (idx[:, None] == idx[None, :]).astype(BF16)


def _pad_rows(rows, width=512, total=8):
    rows = [jnp.pad(r.astype(F32).reshape(-1), (0, width - r.size)) for r in rows]
    rows += [jnp.zeros((width,), F32)] * (total - len(rows))
    return jnp.stack(rows)


def _w_ext(w_in, mu_wa, w1, a1, g1):
    cols_a, cols_b = [], []
    for d in range(2):
        for mu, w in ((mu_wa[d, 0], w1[d]), (mu_wa[d, 1], a1[d])):
            cols_a.append((1.0 - mu)[:, None] * w)
            cols_b.append(mu[:, None] * w)
    parts = [w_in[:, 0:3584], w_in[:, 3840:6400], w_in[:, 3584:3840]] + cols_a + cols_b
    parts.append(jnp.pad(g1, ((0, 0), (0, 128 - g1.shape[1]))))
    w = jnp.concatenate(parts, axis=1)
    return jnp.pad(w, ((0, 0), (0, D_IN_EXT - w.shape[1]))).astype(BF16)


def _pad_lowrank(w, row0):
    return jnp.pad(w, ((row0, 128 - row0 - w.shape[0]), (0, 0))).astype(BF16)


def kernel(x, ln_gain, w_in, w_out, w_ffn_in, w_ffn_out, qk_gain, sink, b_mu_rkv, b_mu_wa, b_w0, b_w1,
           b_w2, b_a0, b_a1, b_a2, b_k_k, b_k_a, b_r_k, b_lnx_gain, b_lnx_bias, b_g1, b_g2,
           d_lb_logits, d_norm_gain):
    batch, seq, d = x.shape
    x2 = x.reshape(batch * seq, d)
    rope_c, rope_s1, rope_s2 = _rope_tables(seq)
    g512 = _block_diag_ones(512, HEAD_DIM)
    pr = jax.nn.softmax(d_lb_logits.astype(F32), axis=0)
    cs = jnp.cumsum(pr, axis=0)
    lb_all = cs - cs[0:1]

    for l in range(DEPTH):
        x2 = _ffn(x2, ln_gain[l, 0][None], w_ffn_in[l, 0].astype(BF16), w_ffn_out[l, 0].astype(BF16))

        p = _inproj(x2, ln_gain[l, 1][None], _w_ext(w_in[l], b_mu_wa[l], b_w1[l], b_a1[l], b_g1[l]))

        tile8 = lambda t: jnp.tile(t, 8)
        gains = _pad_rows([tile8(qk_gain[l, 0, 0]), tile8(qk_gain[l, 0, 1]),
                           tile8(qk_gain[l, 1, 0]), jnp.tile(qk_gain[l, 1, 1], 2)])
        prepped = _prep(p, rope_c, rope_s1, rope_s2, gains, g512, seq)
        qc, kc, vc = prepped[9:]

        a_parts = []
        for pi, (window, dil) in enumerate(DILATED_PATTERNS):
            qa, ka, va = prepped[3 * pi:3 * pi + 3]
            a_parts.append(_band_attention(qa, ka, va, None, batch, seq, dil, window // (2 * dil),
                                           GROUP_WIDTH // HEAD_DIM, 1))
        oc = _band_attention(qc, kc, vc, sink[l][None].astype(F32), batch, seq, 1, C_HALF_WINDOW,
                             C_KV_HEADS, C_Q_HEADS // C_KV_HEADS)

        yb, bonus = [], []
        for dr in range(2):
            pvec = _pad_rows([b_mu_rkv[l, dr, 0], b_mu_rkv[l, dr, 1], b_mu_rkv[l, dr, 2], b_w0[l, dr],
                              b_a0[l, dr], b_k_k[l], b_k_a[l], b_r_k[l]])
            y, bo = _rwkv(p, pvec, _pad_lowrank(b_w2[l, dr], 64 * dr), _pad_lowrank(b_a2[l, dr], 64 * dr + 32),
                          g512, batch, seq, reverse=(dr == 1))
            yb.append(y)
            bonus.append(bo)

        od = []
        for dr in range(2):
            lb = lb_all[l, dr]
            od.append(_hgrn(p, _pad_rows([jnp.log(lb), jnp.log1p(-lb)]), batch, seq, reverse=(dr == 1)))

        pvec = _pad_rows([b_lnx_gain[l], b_lnx_bias[l], d_norm_gain[l]])
        mix = _mixpost(a_parts, yb, bonus, p, oc, od, pvec, g512, _pad_lowrank(b_g2[l], 0))
        x2 = _outproj(mix, w_out[l].astype(BF16), x2)

        x2 = _ffn(x2, ln_gain[l, 2][None], w_ffn_in[l, 1].astype(BF16), w_ffn_out[l, 1].astype(BF16))
    return x2.reshape(batch, seq, d)
```

```python
import functools

import jax
import jax.numpy as jnp
from jax import lax
from jax.experimental import pallas as pl
from jax.experimental.pallas import tpu as pltpu

F32 = jnp.float32
BF16 = jnp.bfloat16

D_MODEL = 2048
DEPTH = 4
HEAD_DIM = 64
GROUP_WIDTH = 512
DILATED_PATTERNS = ((128, 1), (512, 4), (2048, 16))
LNX_EPS = 64e-5
C_Q_HEADS = 8
C_KV_HEADS = 2
C_HALF_WINDOW = 128
D_HEAD_K = 128
D_HEADS = 4
D_FF = 5632
ROPE_THETA = 500000.0
ROT_DIM = 16
NORM_EPS = 1e-6
NEG_INF = -1e30
LOG2_E = 1.4426950408889634

D_IN_EXT = 6912
(CB_AQ, CB_AK, CB_AV, CB_BR, CB_BK, CB_BV, CB_CQ, CB_DQ, CB_DI, CB_DG, CB_DFF, CB_DFB) = range(12)
CB_CK, CB_CV, CB_SECA, CB_SECB, CB_G1 = 48, 49, 50, 51, 52

VMEM_LIMIT = 56 * 1024 * 1024
RWKV_CHUNK = 128
RWKV_BLOCK = 256
HGRN_CHUNK = 64
ATTN_SUB_ROWS = 256
HGRN_BLOCK = 256
HGRN_SUB = 16


def _cparams(sem):
    return pltpu.CompilerParams(dimension_semantics=sem, vmem_limit_bytes=VMEM_LIMIT)


def _seg_sum(x, g):
    hi = x.astype(BF16)
    lo = (x - hi.astype(F32)).astype(BF16)
    d = lambda a: jnp.dot(a, g, preferred_element_type=F32)
    return d(hi) + d(lo)


def _dot(a, b):
    return jnp.dot(a.astype(BF16), b.astype(BF16), preferred_element_type=F32)


def _dot_nt(a, b):
    return lax.dot_general(a.astype(BF16), b.astype(BF16), (((1,), (1,)), ((), ())),
                           preferred_element_type=F32)


def _dot_tn(a, b):
    return lax.dot_general(a.astype(BF16), b.astype(BF16), (((0,), (0,)), ((), ())),
                           preferred_element_type=F32)


def _softplus(u):
    return jnp.maximum(u, 0.0) + jnp.log(1.0 + jnp.exp(-jnp.abs(u)))


def _scan_cumsum(x, n, reverse, seg=None):
    seg = n if seg is None else seg
    row = lax.broadcasted_iota(jnp.int32, x.shape, 0) & (seg - 1)
    s = 1
    while s < seg:
        if reverse:
            x = x + jnp.where(row < seg - s, pltpu.roll(x, n - s, 0), 0.0)
        else:
            x = x + jnp.where(row >= s, pltpu.roll(x, s, 0), 0.0)
        s *= 2
    return x


def _scan_shift(x, carry_row, n, reverse):
    row = lax.broadcasted_iota(jnp.int32, x.shape, 0)
    if reverse:
        return jnp.where(row == n - 1, carry_row, pltpu.roll(x, n - 1, 0))
    return jnp.where(row == 0, carry_row, pltpu.roll(x, 1, 0))


def _ffn_kernel(x_ref, g_ref, wg_ref, wu_ref, wo_ref, o_ref, hn_ref):
    @pl.when(pl.program_id(1) == 0)
    def _():
        x = x_ref[...]
        ms = jnp.mean(x * x, axis=-1, keepdims=True)
        hn_ref[...] = (x * lax.rsqrt(ms + NORM_EPS) * g_ref[...]).astype(BF16)
        o_ref[...] = x

    h = hn_ref[...]
    gate = jnp.dot(h, wg_ref[...], preferred_element_type=F32)
    up = jnp.dot(h, wu_ref[...], preferred_element_type=F32)
    act = (0.5 * gate * jax.nn.sigmoid(gate) * up).astype(BF16)
    o_ref[...] += jnp.dot(act, wo_ref[...], preferred_element_type=F32)


def _ffn(x2, gain, w_in, w_out, tm=1024, tf=512):
    n, d = x2.shape
    nf = D_FF // tf
    return pl.pallas_call(
        _ffn_kernel,
        grid=(n // tm, nf),
        in_specs=[
            pl.BlockSpec((tm, d), lambda i, j: (i, 0)),
            pl.BlockSpec((1, d), lambda i, j: (0, 0)),
            pl.BlockSpec((d, tf), lambda i, j: (0, j)),
            pl.BlockSpec((d, tf), lambda i, j: (0, j + nf)),
            pl.BlockSpec((tf, d), lambda i, j: (j, 0)),
        ],
        out_specs=pl.BlockSpec((tm, d), lambda i, j: (i, 0)),
        out_shape=jax.ShapeDtypeStruct((n, d), F32),
        scratch_shapes=[pltpu.VMEM((tm, d), BF16)],
        compiler_params=_cparams(("parallel", "arbitrary")),
        name="ffn",
    )(x2, gain, w_in, w_in, w_out)


def _inproj_kernel(x_ref, g_ref, w_ref, o_ref, hn_ref):
    @pl.when(pl.program_id(1) == 0)
    def _():
        x = x_ref[...]
        ms = jnp.mean(x * x, axis=-1, keepdims=True)
        hn_ref[...] = (x * lax.rsqrt(ms + NORM_EPS) * g_ref[...]).astype(BF16)

    o_ref[...] = jnp.dot(hn_ref[...], w_ref[...], preferred_element_type=F32)


def _inproj(x2, gain, w_ext, tm=1024, tn=768):
    n, d = x2.shape
    nc = w_ext.shape[1]
    return pl.pallas_call(
        _inproj_kernel,
        grid=(n // tm, nc // tn),
        in_specs=[
            pl.BlockSpec((tm, d), lambda i, j: (i, 0)),
            pl.BlockSpec((1, d), lambda i, j: (0, 0)),
            pl.BlockSpec((d, tn), lambda i, j: (0, j)),
        ],
        out_specs=pl.BlockSpec((tm, tn), lambda i, j: (i, j)),
        out_shape=jax.ShapeDtypeStruct((n, nc), F32),
        scratch_shapes=[pltpu.VMEM((tm, d), BF16)],
        compiler_params=_cparams(("parallel", "arbitrary")),
        name="inproj",
    )(x2, gain, w_ext)


def _norm_rope(t, gain, g, c, s1, s2, scale):
    w = t.shape[-1]
    ss = _seg_sum(t * t, g)
    y = t * lax.rsqrt(ss * (1.0 / HEAD_DIM) + NORM_EPS) * gain
    half = ROT_DIM // 2
    out = y * c + pltpu.roll(y, w - half, 1) * s1 + pltpu.roll(y, half, 1) * s2
    return out * scale if scale != 1.0 else out


def _prep_kernel(aq_ref, ak_ref, av_ref, cq_ref, ck_ref, cv_ref, c_ref, s1_ref, s2_ref,
                 gain_ref, g_ref,
                 qa_ref, ka_ref, va_ref, qa4_ref, ka4_ref, va4_ref, qa16_ref, ka16_ref, va16_ref,
                 qc_ref, kc_ref, vc_ref, scr_ref, *, tm):
    c1, s11, s21 = c_ref[...], s1_ref[...], s2_ref[...]
    c4 = jnp.concatenate([c1] * 4, axis=1)
    s14 = jnp.concatenate([s11] * 4, axis=1)
    s24 = jnp.concatenate([s21] * 4, axis=1)
    g = g_ref[...]
    g1 = g_ref[0:128, 0:128]
    scale = HEAD_DIM ** -0.5

    def emit(y, nat_ref, dil_refs):
        nat_ref[...] = y.astype(BF16)
        for c in range(4):
            scr_ref[c] = y[:, c * 128:(c + 1) * 128]
        for d, ref in dil_refs:
            for r in range(d):
                for c in range(4):
                    lo = r * 512 + c * 128
                    ref[:, lo:lo + 128] = scr_ref[c, pl.ds(r, tm // d, stride=d), :].astype(BF16)

    emit(_norm_rope(aq_ref[...], gain_ref[0:1, :], g, c4, s14, s24, scale), qa_ref,
         ((4, qa4_ref), (16, qa16_ref)))
    emit(_norm_rope(ak_ref[...], gain_ref[1:2, :], g, c4, s14, s24, 1.0), ka_ref,
         ((4, ka4_ref), (16, ka16_ref)))
    emit(av_ref[...], va_ref, ((4, va4_ref), (16, va16_ref)))
    qc_ref[...] = _norm_rope(cq_ref[...], gain_ref[2:3, :], g, c4, s14, s24, scale).astype(BF16)
    kc_ref[...] = _norm_rope(ck_ref[...], gain_ref[3:4, 0:128], g1, c1, s11, s21, 1.0).astype(BF16)
    vc_ref[...] = cv_ref[...].astype(BF16)


def _prep(p, rope_c, rope_s1, rope_s2, gains, g512, seq, tm=512):
    n = p.shape[0]
    nseq = seq // tm
    wide = lambda cb: pl.BlockSpec((tm, 512), lambda i, cb=cb: (i, cb))
    narrow = lambda cb: pl.BlockSpec((tm, 128), lambda i, cb=cb: (i, cb))
    tab = pl.BlockSpec((tm, 128), lambda i: (i % nseq, 0))
    rowblk = lambda rows, width: pl.BlockSpec((rows, width), lambda i: (i, 0))
    shape = lambda rows, width: jax.ShapeDtypeStruct((rows, width), BF16)
    trio = lambda d: [rowblk(tm // d, d * 512)] * 3
    trio_shape = lambda d: [shape(n // d, d * 512)] * 3
    return pl.pallas_call(
        functools.partial(_prep_kernel, tm=tm),
        grid=(n // tm,),
        in_specs=[wide(CB_AQ), wide(CB_AK), wide(CB_AV), wide(CB_CQ), narrow(CB_CK), narrow(CB_CV),
                  tab, tab, tab,
                  pl.BlockSpec((8, 512), lambda i: (0, 0)),
                  pl.BlockSpec((512, 512), lambda i: (0, 0))],
        out_specs=trio(1) + trio(4) + trio(16) + [rowblk(tm, 512), rowblk(tm, 128), rowblk(tm, 128)],
        out_shape=trio_shape(1) + trio_shape(4) + trio_shape(16) + [shape(n, 512), shape(n, 128), shape(n, 128)],
        scratch_shapes=[pltpu.VMEM((4, tm, 128), F32)],
        compiler_params=_cparams(("parallel",)),
        name="prep_qk",
    )(p, p, p, p, p, p, rope_c, rope_s1, rope_s2, gains, g512)


def _band_kernel(*refs, tq, hb, sb, half, length, n_kv, group, with_sink):
    if with_sink:
        q_ref, kp_ref, km_ref, kn_ref, vp_ref, vm_ref, vn_ref, sink_ref, o_ref = refs
    else:
        q_ref, kp_ref, km_ref, kn_ref, vp_ref, vm_ref, vn_ref, o_ref, lse_ref = refs
    i = pl.program_id(2)
    nsub = tq // sb
    gs, nwin = group * sb, sb + 2 * hb
    row = lax.broadcasted_iota(jnp.int32, (gs, nwin), 0) & (sb - 1)
    col = lax.broadcasted_iota(jnp.int32, (gs, nwin), 1)
    band = jnp.abs(row + hb - col) <= half
    valid = []
    for j in range(nsub):
        kpos = i * tq + j * sb - hb + col
        valid.append(band & (kpos >= 0) & (kpos < length))
    q_all = q_ref[0]
    n_heads = n_kv * group
    hsl = [slice(h * HEAD_DIM, (h + 1) * HEAD_DIM) for h in range(n_heads)]
    k_heads = [jnp.concatenate([kp_ref[0, :, s], km_ref[0, :, s], kn_ref[0, :, s]], axis=0) for s in hsl[:n_kv]]
    v_heads = [jnp.concatenate([vp_ref[0, :, s], vm_ref[0, :, s], vn_ref[0, :, s]], axis=0) for s in hsl[:n_kv]]
    kvs = range(n_kv)
    if with_sink:
        sinks = sink_ref[...]
        rowg = lax.broadcasted_iota(jnp.int32, (gs, 1), 0) // sb
        sink_col = []
        for kv in kvs:
            col_kv = jnp.zeros((gs, 1), F32)
            for g in range(group):
                h = kv * group + g
                col_kv = jnp.where(rowg == g, sinks[0:1, h:h + 1], col_kv)
            sink_col.append(col_kv)
    outs, lses = [], []
    for j in range(nsub):
        rows, win = slice(j * sb, (j + 1) * sb), slice(j * sb, (j + 1) * sb + 2 * hb)
        qs = [jnp.concatenate([q_all[rows, hsl[kv * group + g]] for g in range(group)], axis=0) for kv in kvs]
        s = [lax.dot_general(qs[kv], k_heads[kv][win], (((1,), (1,)), ((), ())),
                             preferred_element_type=F32) for kv in kvs]
        s = [jnp.where(valid[j], x, NEG_INF) for x in s]
        m = [jnp.max(x, axis=-1, keepdims=True) for x in s]
        p = [jnp.exp(x - y) for x, y in zip(s, m)]
        l = [jnp.sum(x, axis=-1, keepdims=True) for x in p]
        acc = [jnp.dot(p[kv].astype(BF16), v_heads[kv][win], preferred_element_type=F32) for kv in kvs]
        if with_sink:
            m_all = [jnp.maximum(x, y) for x, y in zip(m, sink_col)]
            sc = [jnp.exp(x - y) for x, y in zip(m, m_all)]
            den = [l[kv] * sc[kv] + jnp.exp(sink_col[kv] - m_all[kv]) for kv in kvs]
            o = [acc[kv] * (sc[kv] / den[kv]) for kv in kvs]
        else:
            o = [x / y for x, y in zip(acc, l)]
            lse = [x + jnp.log(y) for x, y in zip(m, l)]
            lses.append(jnp.concatenate(
                [jnp.broadcast_to(lse[h // group][(h % group) * sb:(h % group + 1) * sb], (sb, HEAD_DIM))
                 for h in range(n_heads)], axis=1))
        outs.append(jnp.concatenate([o[h // group][(h % group) * sb:(h % group + 1) * sb]
                                     for h in range(n_heads)], axis=1))
    for j in range(nsub):
        o_ref[0, j * sb:(j + 1) * sb, :] = outs[j].astype(o_ref.dtype)
        if not with_sink:
            lse_ref[0, j * sb:(j + 1) * sb, :] = lses[j]


def _band_attention(q, k, v, sink, batch, seq, dil, half, n_kv, group, tq=512):
    wq, wk = q.shape[1] // dil, k.shape[1] // dil
    hb = half
    sl = seq // dil
    tq = min(tq, sl)
    nq = sl // tq
    per = tq // hb
    nhb = sl // hb
    qv = q.reshape(batch, sl, dil * wq)
    kv = k.reshape(batch, sl, dil * wk)
    vv = v.reshape(batch, sl, dil * wk)
    main = lambda w: pl.BlockSpec((1, tq, w), lambda b, r, i: (b, i, r))
    prev = pl.BlockSpec((1, hb, wk), lambda b, r, i: (b, jnp.maximum(i * per - 1, 0), r))
    nxt = pl.BlockSpec((1, hb, wk), lambda b, r, i: (b, jnp.minimum((i + 1) * per, nhb - 1), r))
    with_sink = sink is not None
    in_specs = [main(wq), prev, main(wk), nxt, prev, main(wk), nxt]
    args = [qv, kv, kv, kv, vv, vv, vv]
    if with_sink:
        in_specs.append(pl.BlockSpec((1, wq // HEAD_DIM), lambda b, r, i: (0, 0)))
        args.append(sink)
        out_specs = main(wq)
        out_shape = jax.ShapeDtypeStruct(qv.shape, BF16)
    else:
        out_specs = [main(wq), main(wq)]
        out_shape = [jax.ShapeDtypeStruct(qv.shape, BF16), jax.ShapeDtypeStruct(qv.shape, F32)]
    kern = functools.partial(_band_kernel, tq=tq, hb=hb, sb=min(tq, max(hb, ATTN_SUB_ROWS // group)),
                             half=half, length=sl,
                             n_kv=n_kv, group=group, with_sink=with_sink)
    out = pl.pallas_call(
        kern,
        grid=(batch, dil, nq),
        in_specs=in_specs,
        out_specs=out_specs,
        out_shape=out_shape,
        compiler_params=_cparams(("parallel", "parallel", "parallel")),
        name="band_attn_sink" if with_sink else f"band_attn_d{dil}",
    )(*args)
    flat = lambda t: t.reshape(batch * sl, dil * wq)
    if with_sink:
        return flat(out)
    return flat(out[0]), flat(out[1])


def _rwkv_kernel(r_ref, k_ref, v_ref, sa_ref, sb_ref, pv_ref, w2_ref, a2_ref, g_ref,
                 y_ref, bonus_ref, state_ref, carry_ref, carryb_ref, *, n, cs, reverse):
    @pl.when(pl.program_id(1) == 0)
    def _():
        state_ref[...] = jnp.zeros_like(state_ref)
        carry_ref[...] = jnp.zeros_like(carry_ref)
        carryb_ref[...] = jnp.zeros_like(carryb_ref)

    last = 0 if reverse else n - 1
    r0, k0, v0, sb = r_ref[...], k_ref[...], v_ref[...], sb_ref[...]
    shift = lambda x, c: _scan_shift(x, c, n, reverse)
    r = r0 + (shift(r0, carry_ref[0:1, :]) - r0) * pv_ref[0:1, :]
    k = k0 + (shift(k0, carry_ref[1:2, :]) - k0) * pv_ref[1:2, :]
    v = v0 + (shift(v0, carry_ref[2:3, :]) - v0) * pv_ref[2:3, :]
    xlr = sa_ref[...] + shift(sb, carryb_ref[0:1, :])
    carry_ref[0:1, :] = r0[last:last + 1, :]
    carry_ref[1:2, :] = k0[last:last + 1, :]
    carry_ref[2:3, :] = v0[last:last + 1, :]
    carryb_ref[0:1, :] = sb[last:last + 1, :]

    wl = pv_ref[3:4, :] + jnp.dot(jnp.tanh(xlr).astype(BF16), w2_ref[...], preferred_element_type=F32)
    w_log = -_softplus(-wl) - 0.5
    ld = -jnp.exp(w_log)
    a = jax.nn.sigmoid(pv_ref[4:5, :] + jnp.dot(xlr.astype(BF16), a2_ref[...], preferred_element_type=F32))
    g = g_ref[...]
    kkr = k * pv_ref[5:6, :]
    kx = k * (1.0 + (a - 1.0) * pv_ref[6:7, :])
    sums = _seg_sum(jnp.concatenate([kkr * kkr, r * kx * pv_ref[7:8, :]], axis=0), g)
    kk = kkr / jnp.maximum(jnp.sqrt(sums[:n]), 1e-12)
    bonus_ref[...] = (sums[n:] * v).astype(bonus_ref.dtype)

    nsub = n // cs
    last_row = lambda gi: gi * cs if reverse else (gi + 1) * cs - 1
    row1 = lax.broadcasted_iota(jnp.int32, (n, 1), 0)
    c = _scan_cumsum(ld, n, reverse, seg=cs)
    c_last = c[last_row(0):last_row(0) + 1, :]
    for gi in range(1, nsub):
        c_last = jnp.where(row1 >= gi * cs, c[last_row(gi):last_row(gi) + 1, :], c_last)
    e_c = jnp.exp(c)
    e_cx = jnp.exp(c - ld)
    e_nc = jnp.exp(-c)
    e_end = jnp.exp(c_last - c)
    kb = kk * a
    rt = r * e_c
    at = -kk * e_cx
    bt = kb * e_nc
    kt = kx * e_nc
    bh = kb * e_end
    kh = kx * e_end

    rt_b, at_b, bt_b, kt_b, v_b = (t.astype(BF16) for t in (rt, at, bt, kt, v))
    bh_b, kh_b = bh.astype(BF16), kh.astype(BF16)

    row = lax.broadcasted_iota(jnp.int32, (2 * cs, 2 * cs), 0)
    col = lax.broadcasted_iota(jnp.int32, (2 * cs, 2 * cs), 1)
    rr, cc = row & (cs - 1), col & (cs - 1)
    strict = (cc > rr) if reverse else (cc < rr)
    mask = strict | ((cc == rr) & (row >= cs))
    eye = (lax.broadcasted_iota(jnp.int32, (cs, cs), 0) == lax.broadcasted_iota(jnp.int32, (cs, cs), 1)).astype(F32)

    heads = range(GROUP_WIDTH // HEAD_DIM)
    nh = len(heads)
    cat = lambda a, b: jnp.concatenate([a, b], axis=0)
    pairs = [(slice(gi * cs, (gi + 1) * cs), slice(h * HEAD_DIM, (h + 1) * HEAD_DIM))
             for gi in range(nsub) for h in heads]
    big = [jnp.where(mask, _dot_nt(cat(at_b[rs, s], rt_b[rs, s]), cat(bt_b[rs, s], kt_b[rs, s])), 0.0)
           for rs, s in pairs]
    akv = [_dot(b[:cs, cs:], v_b[rs, s]) for b, (rs, s) in zip(big, pairs)]
    a_ab = [b[:cs, :cs] for b in big]
    t_inv = [eye + x for x in a_ab]
    pw = [_dot(x, x) for x in a_ab]
    lvl = 4
    while lvl < cs:
        res = [_dot(cat(x, t), x) for x, t in zip(pw, t_inv)]
        t_inv = [t + x[cs:] for t, x in zip(t_inv, res)]
        pw = [x[:cs] for x in res]
        lvl *= 2
    t_inv = [t + _dot(t, x) for t, x in zip(t_inv, pw)]
    ta = [_dot(t, at_b[rs, s]) for t, (rs, s) in zip(t_inv, pairs)]
    tv = [_dot(t, x) for t, x in zip(t_inv, akv)]

    st = [state_ref[h] for h in heads]
    ys = [None] * nsub
    for gi in (range(nsub - 1, -1, -1) if reverse else range(nsub)):
        wc = jnp.exp(c[last_row(gi):last_row(gi) + 1, :])
        sl = slice(gi * nh, (gi + 1) * nh)
        u = [_dot_nt(x, s0) + z for x, s0, z in zip(ta[sl], st, tv[sl])]
        uv = [cat(x, v[rs, s]).astype(BF16) for x, (rs, s) in zip(u, pairs[sl])]
        ys[gi] = [_dot_nt(rt_b[rs, s], s0) + _dot(b[cs:, :], x)
                  for (rs, s), s0, b, x in zip(pairs[sl], st, big[sl], uv)]
        st = [s0 * wc[:, s] + _dot_tn(x, cat(bh_b[rs, s], kh_b[rs, s]))
              for s0, (rs, s), x in zip(st, pairs[sl], uv)]
    for gi in range(nsub):
        for h in heads:
            rs, s = pairs[gi * nh + h]
            y_ref[rs, s] = ys[gi][h]
    for h in heads:
        state_ref[h] = st[h]


def _rwkv(p, pvec, w2pad, a2pad, g512, batch, seq, reverse):
    n = p.shape[0]
    c = RWKV_BLOCK
    nc = seq // c
    if reverse:
        rowidx = lambda b, j: b * nc + (nc - 1 - j)
    else:
        rowidx = lambda b, j: b * nc + j
    wide = lambda cb: pl.BlockSpec((c, 512), lambda b, j, cb=cb: (rowidx(b, j), cb))
    narrow = lambda cb: pl.BlockSpec((c, 128), lambda b, j, cb=cb: (rowidx(b, j), cb))
    const = lambda shape: pl.BlockSpec(shape, lambda b, j: (0,) * len(shape))
    out = pl.BlockSpec((c, 512), lambda b, j: (rowidx(b, j), 0))
    return pl.pallas_call(
        functools.partial(_rwkv_kernel, n=c, cs=RWKV_CHUNK, reverse=reverse),
        grid=(batch, nc),
        in_specs=[wide(CB_BR), wide(CB_BK), wide(CB_BV), narrow(CB_SECA), narrow(CB_SECB),
                  const((8, 512)), const((128, 512)), const((128, 512)), const((512, 512))],
        out_specs=[out, out],
        out_shape=[jax.ShapeDtypeStruct((n, 512), F32), jax.ShapeDtypeStruct((n, 512), BF16)],
        scratch_shapes=[pltpu.VMEM((8, HEAD_DIM, HEAD_DIM), F32), pltpu.VMEM((8, 512), F32),
                        pltpu.VMEM((8, 128), F32)],
        compiler_params=_cparams(("parallel", "arbitrary")),
        name="rwkv_bwd" if reverse else "rwkv_fwd",
    )(p, p, p, p, p, pvec, w2pad, a2pad, g512)


def _hgrn_kernel(q_ref, v_ref, z_ref, lb_ref, o_ref, state_ref, *, n, cs, sub, reverse):
    @pl.when(pl.program_id(1) == 0)
    def _():
        state_ref[...] = jnp.zeros_like(state_ref)

    q, v, z = q_ref[...], v_ref[...], z_ref[...]
    log_sig = -_softplus(-z)
    x1 = lb_ref[0:1, :]
    x2 = lb_ref[1:2, :] + log_sig
    logf = jnp.maximum(x1, x2) + jnp.log(1.0 + jnp.exp(-jnp.abs(x1 - x2)))
    kf = 1.0 - jnp.exp(logf)
    nchunk, nsub = n // cs, cs // sub
    last_row = lambda gi: gi * cs if reverse else (gi + 1) * cs - 1
    row1 = lax.broadcasted_iota(jnp.int32, (n, 1), 0)
    b = _scan_cumsum(logf, n, reverse, seg=cs)
    b_last = b[last_row(0):last_row(0) + 1, :]
    for gi in range(1, nchunk):
        b_last = jnp.where(row1 >= gi * cs, b[last_row(gi):last_row(gi) + 1, :], b_last)
    qe = q * jnp.exp(b)
    ke = kf * jnp.exp(b_last - b)
    b2 = b * LOG2_E
    c2 = b2 - jnp.log(kf) * LOG2_E
    srow = lax.broadcasted_iota(jnp.int32, (sub, sub), 0)
    lane = lax.broadcasted_iota(jnp.int32, (sub, sub), 1)
    tri = (srow <= lane) if reverse else (srow >= lane)
    hsl = [slice(h * D_HEAD_K, (h + 1) * D_HEAD_K) for h in range(D_HEADS)]

    intra = {}
    for gi in range(nchunk):
        base = gi * cs
        for h, hs in enumerate(hsl):
            for m in range(nsub):
                ms = slice(base + m * sub, base + (m + 1) * sub)
                q_m, b_m = q[ms, hs], b[ms, hs]
                b2_m, c2_m = b2[ms, hs], c2[ms, hs]
                dmat = jnp.zeros((sub, sub), F32)
                for s in range(sub):
                    w = q_m * jnp.exp2(b2_m - c2_m[s:s + 1])
                    dmat = jnp.where(lane == s, jnp.sum(w, axis=-1, keepdims=True), dmat)
                dmat = jnp.where(tri, dmat, 0.0)
                o_m = _dot(dmat, v[ms, hs])
                es = slice(base + (m + 1) * sub, base + cs) if reverse else slice(base, base + m * sub)
                if es.stop > es.start:
                    first = base + ((m + 1) * sub - 1 if reverse else m * sub)
                    rho = b[first:first + 1, hs] - logf[first:first + 1, hs]
                    qt = q_m * jnp.exp(b_m - rho)
                    kt = kf[es, hs] * jnp.exp(rho - b[es, hs])
                    o_m = o_m + _dot(_dot_nt(qt, kt), v[es, hs])
                intra[gi, h, m] = o_m

    st = [state_ref[h] for h in range(D_HEADS)]
    inter = {}
    for gi in (range(nchunk - 1, -1, -1) if reverse else range(nchunk)):
        rs = slice(gi * cs, (gi + 1) * cs)
        wend = jnp.exp(b[last_row(gi):last_row(gi) + 1, :])
        for h, hs in enumerate(hsl):
            inter[gi, h] = _dot_nt(qe[rs, hs], st[h])
        st = [st[h] * wend[:, hs] + _dot_tn(v[rs, hs], ke[rs, hs]) for h, hs in enumerate(hsl)]
    for (gi, h, m), o_m in intra.items():
        lo = gi * cs + m * sub
        o_ref[lo:lo + sub, hsl[h]] = (o_m + inter[gi, h][m * sub:(m + 1) * sub]).astype(o_ref.dtype)
    for h in range(D_HEADS):
        state_ref[h] = st[h]


def _hgrn(p, lbvec, batch, seq, reverse):
    n = p.shape[0]
    c = HGRN_BLOCK
    nc = seq // c
    if reverse:
        rowidx = lambda b, j: b * nc + (nc - 1 - j)
    else:
        rowidx = lambda b, j: b * nc + j
    wide = lambda cb: pl.BlockSpec((c, 512), lambda b, j, cb=cb: (rowidx(b, j), cb))
    return pl.pallas_call(
        functools.partial(_hgrn_kernel, n=c, cs=HGRN_CHUNK, sub=HGRN_SUB, reverse=reverse),
        grid=(batch, nc),
        in_specs=[wide(CB_DQ), wide(CB_DI), wide(CB_DFB if reverse else CB_DFF),
                  pl.BlockSpec((8, 512), lambda b, j: (0, 0))],
        out_specs=pl.BlockSpec((c, 512), lambda b, j: (rowidx(b, j), 0)),
        out_shape=jax.ShapeDtypeStruct((n, 512), BF16),
        scratch_shapes=[pltpu.VMEM((D_HEADS, D_HEAD_K, D_HEAD_K), F32)],
        compiler_params=_cparams(("parallel", "arbitrary")),
        name="hgrn_bwd" if reverse else "hgrn_fwd",
    )(p, p, p, lbvec)


def _mixpost_kernel(o1_ref, l1_ref, o2_ref, l2_ref, o3_ref, l3_ref,
                    y0_ref, y1_ref, bo0_ref, bo1_ref, g1_ref, oc_ref,
                    h0_ref, h1_ref, dg_ref, pv_ref, g_ref, g2_ref, mix_ref,
                    o2s_ref, l2s_ref, o3s_ref, l3s_ref, *, tm):
    for d, src, dst in ((4, o2_ref, o2s_ref), (4, l2_ref, l2s_ref), (16, o3_ref, o3s_ref), (16, l3_ref, l3s_ref)):
        for r in range(d):
            for c in range(4):
                lo = r * 512 + c * 128
                dst[c, pl.ds(r, tm // d, stride=d), :] = src[:, lo:lo + 128].astype(F32)
    wide = lambda ref: jnp.concatenate([ref[c] for c in range(4)], axis=1)
    l1, l2, l3 = l1_ref[...], wide(l2s_ref), wide(l3s_ref)
    mx = jnp.maximum(jnp.maximum(l1, l2), l3)
    w1, w2, w3 = jnp.exp(l1 - mx), jnp.exp(l2 - mx), jnp.exp(l3 - mx)
    out_a = (o1_ref[...] * w1 + wide(o2s_ref) * w2 + wide(o3s_ref) * w3) / (w1 + w2 + w3)
    mix_ref[:, 0:512] = out_a.astype(BF16)
    g = g_ref[...]
    y = y0_ref[...] + y1_ref[...]
    mu = _seg_sum(y, g) * (1.0 / HEAD_DIM)
    yc = y - mu
    var = _seg_sum(yc * yc, g) * (1.0 / HEAD_DIM)
    yn = yc * lax.rsqrt(var + LNX_EPS) * pv_ref[0:1, :] + pv_ref[1:2, :]
    gate = jnp.dot(jax.nn.sigmoid(g1_ref[...]).astype(BF16), g2_ref[...], preferred_element_type=F32)
    mix_ref[:, 512:1024] = ((yn + bo0_ref[...] + bo1_ref[...]) * gate).astype(BF16)
    mix_ref[:, 1024:1536] = oc_ref[...]
    o = h0_ref[...].astype(F32) + h1_ref[...].astype(F32)
    dg = dg_ref[...]
    silu = dg * jax.nn.sigmoid(dg)
    for h in range(D_HEADS):
        hs = slice(h * D_HEAD_K, (h + 1) * D_HEAD_K)
        oh = o[:, hs]
        ms = jnp.mean(oh * oh, axis=-1, keepdims=True)
        res = oh * lax.rsqrt(ms + NORM_EPS) * pv_ref[2:3, hs] * silu[:, hs]
        mix_ref[:, 1536 + h * D_HEAD_K:1536 + (h + 1) * D_HEAD_K] = res.astype(BF16)


def _mixpost(a_parts, yb, bonus, p, oc, od, pvec, g512, g2pad, tm=256):
    n = p.shape[0]
    blk = pl.BlockSpec((tm, 512), lambda i: (i, 0))
    const = lambda shape: pl.BlockSpec(shape, lambda i: (0,) * len(shape))
    args = []
    for o, l in a_parts:
        args += [o, l]
    args += [yb[0], yb[1], bonus[0], bonus[1], p, oc, od[0], od[1], p, pvec, g512, g2pad]
    dil = lambda d: pl.BlockSpec((tm // d, d * 512), lambda i: (i, 0))
    in_specs = [blk, blk, dil(4), dil(4), dil(16), dil(16)] + [blk] * 4 + [
        pl.BlockSpec((tm, 128), lambda i: (i, CB_G1)), blk, blk, blk,
        pl.BlockSpec((tm, 512), lambda i: (i, CB_DG)),
        const((8, 512)), const((512, 512)), const((128, 512))]
    return pl.pallas_call(
        functools.partial(_mixpost_kernel, tm=tm),
        grid=(n // tm,),
        in_specs=in_specs,
        out_specs=pl.BlockSpec((tm, D_MODEL), lambda i: (i, 0)),
        out_shape=jax.ShapeDtypeStruct((n, D_MODEL), BF16),
        scratch_shapes=[pltpu.VMEM((4, tm, 128), F32)] * 4,
        compiler_params=_cparams(("parallel",)),
        name="mixpost",
    )(*args)


def _outproj_kernel(mix_ref, w_ref, x_ref, o_ref):
    o_ref[...] = x_ref[...] + jnp.dot(mix_ref[...], w_ref[...], preferred_element_type=F32)


def _outproj(mix, w, x2, tm=512, tn=2048):
    n, d = x2.shape
    return pl.pallas_call(
        _outproj_kernel,
        grid=(n // tm, d // tn),
        in_specs=[pl.BlockSpec((tm, mix.shape[1]), lambda i, j: (i, 0)),
                  pl.BlockSpec((mix.shape[1], tn), lambda i, j: (0, j)),
                  pl.BlockSpec((tm, tn), lambda i, j: (i, j))],
        out_specs=pl.BlockSpec((tm, tn), lambda i, j: (i, j)),
        out_shape=jax.ShapeDtypeStruct((n, d), F32),
        compiler_params=_cparams(("parallel", "arbitrary")),
        name="outproj",
    )(mix, w, x2)


def _rope_tables(seq):
    half = ROT_DIM // 2
    inv = ROPE_THETA ** (-jnp.arange(0, ROT_DIM, 2, dtype=F32) / ROT_DIM)
    ang = jnp.arange(seq, dtype=F32)[:, None] * inv[None]
    cos, sin = jnp.cos(ang), jnp.sin(ang)
    ones = jnp.ones((seq, HEAD_DIM - ROT_DIM), F32)
    zeros = jnp.zeros((seq, HEAD_DIM - ROT_DIM), F32)
    z8 = jnp.zeros((seq, half), F32)
    c = jnp.concatenate([cos, cos, ones], axis=1)
    s1 = jnp.concatenate([-sin, z8, zeros], axis=1)
    s2 = jnp.concatenate([z8, sin, zeros], axis=1)
    tile2 = lambda t: jnp.concatenate([t, t], axis=1)
    return tile2(c), tile2(s1), tile2(s2)


def _block_diag_ones(width, seg):
    idx = jnp.arange(width) // seg
    return (idx[:, None] == idx[None, :]).astype(BF16)


def _pad_rows(rows, width=512, total=8):
    rows = [jnp.pad(r.astype(F32).reshape(-1), (0, width - r.size)) for r in rows]
    rows += [jnp.zeros((width,), F32)] * (total - len(rows))
    return jnp.stack(rows)


def _w_ext(w_in, mu_wa, w1, a1, g1):
    cols_a, cols_b = [], []
    for d in range(2):
        for mu, w in ((mu_wa[d, 0], w1[d]), (mu_wa[d, 1], a1[d])):
            cols_a.append((1.0 - mu)[:, None] * w)
            cols_b.append(mu[:, None] * w)
    parts = [w_in[:, 0:3584], w_in[:, 3840:6400], w_in[:, 3584:3840]] + cols_a + cols_b
    parts.append(jnp.pad(g1, ((0, 0), (0, 128 - g1.shape[1]))))
    w = jnp.concatenate(parts, axis=1)
    return jnp.pad(w, ((0, 0), (0, D_IN_EXT - w.shape[1]))).astype(BF16)


def _pad_lowrank(w, row0):
    return jnp.pad(w, ((row0, 128 - row0 - w.shape[0]), (0, 0))).astype(BF16)


def kernel(x, ln_gain, w_in, w_out, w_ffn_in, w_ffn_out, qk_gain, sink, b_mu_rkv, b_mu_wa, b_w0, b_w1,
           b_w2, b_a0, b_a1, b_a2, b_k_k, b_k_a, b_r_k, b_lnx_gain, b_lnx_bias, b_g1, b_g2,
           d_lb_logits, d_norm_gain):
    batch, seq, d = x.shape
    x2 = x.reshape(batch * seq, d)
    rope_c, rope_s1, rope_s2 = _rope_tables(seq)
    g512 = _block_diag_ones(512, HEAD_DIM)
    pr = jax.nn.softmax(d_lb_logits.astype(F32), axis=0)
    cs = jnp.cumsum(pr, axis=0)
    lb_all = cs - cs[0:1]

    for l in range(DEPTH):
        x2 = _ffn(x2, ln_gain[l, 0][None], w_ffn_in[l, 0].astype(BF16), w_ffn_out[l, 0].astype(BF16))

        p = _inproj(x2, ln_gain[l, 1][None], _w_ext(w_in[l], b_mu_wa[l], b_w1[l], b_a1[l], b_g1[l]))

        tile8 = lambda t: jnp.tile(t, 8)
        gains = _pad_rows([tile8(qk_gain[l, 0, 0]), tile8(qk_gain[l, 0, 1]),
                           tile8(qk_gain[l, 1, 0]), jnp.tile(qk_gain[l, 1, 1], 2)])
        prepped = _prep(p, rope_c, rope_s1, rope_s2, gains, g512, seq)
        qc, kc, vc = prepped[9:]

        a_parts = []
        for pi, (window, dil) in enumerate(DILATED_PATTERNS):
            qa, ka, va = prepped[3 * pi:3 * pi + 3]
            a_parts.append(_band_attention(qa, ka, va, None, batch, seq, dil, window // (2 * dil),
                                           GROUP_WIDTH // HEAD_DIM, 1))
        oc = _band_attention(qc, kc, vc, sink[l][None].astype(F32), batch, seq, 1, C_HALF_WINDOW,
                             C_KV_HEADS, C_Q_HEADS // C_KV_HEADS)

        yb, bonus = [], []
        for dr in range(2):
            pvec = _pad_rows([b_mu_rkv[l, dr, 0], b_mu_rkv[l, dr, 1], b_mu_rkv[l, dr, 2], b_w0[l, dr],
                              b_a0[l, dr], b_k_k[l], b_k_a[l], b_r_k[l]])
            y, bo = _rwkv(p, pvec, _pad_lowrank(b_w2[l, dr], 64 * dr), _pad_lowrank(b_a2[l, dr], 64 * dr + 32),
                          g512, batch, seq, reverse=(dr == 1))
            yb.append(y)
            bonus.append(bo)

        od = []
        for dr in range(2):
            lb = lb_all[l, dr]
            od.append(_hgrn(p, _pad_rows([jnp.log(lb), jnp.log1p(-lb)]), batch, seq, reverse=(dr == 1)))

        pvec = _pad_rows([b_lnx_gain[l], b_lnx_bias[l], d_norm_gain[l]])
        mix = _mixpost(a_parts, yb, bonus, p, oc, od, pvec, g512, _pad_lowrank(b_g2[l], 0))
        x2 = _outproj(mix, w_out[l].astype(BF16), x2)

        x2 = _ffn(x2, ln_gain[l, 2][None], w_ffn_in[l, 1].astype(BF16), w_ffn_out[l, 1].astype(BF16))
    return x2.reshape(batch, seq, d)
```

```python
import functools

import jax
import jax.numpy as jnp
from jax import lax
from jax.experimental import pallas as pl
from jax.experimental.pallas import tpu as pltpu

F32 = jnp.float32
BF16 = jnp.bfloat16

D_MODEL = 2048
DEPTH = 4
HEAD_DIM = 64
GROUP_WIDTH = 512
DILATED_PATTERNS = ((128, 1), (512, 4), (2048, 16))
LNX_EPS = 64e-5
C_Q_HEADS = 8
C_KV_HEADS = 2
C_HALF_WINDOW = 128
D_HEAD_K = 128
D_HEADS = 4
D_FF = 5632
ROPE_THETA = 500000.0
ROT_DIM = 16
NORM_EPS = 1e-6
NEG_INF = -1e30
LOG2_E = 1.4426950408889634

D_IN_EXT = 6912
(CB_AQ, CB_AK, CB_AV, CB_BR, CB_BK, CB_BV, CB_CQ, CB_DQ, CB_DI, CB_DG, CB_DFF, CB_DFB) = range(12)
CB_CK, CB_CV, CB_SECA, CB_SECB, CB_G1 = 48, 49, 50, 51, 52

VMEM_LIMIT = 56 * 1024 * 1024
RWKV_CHUNK = 128
RWKV_BLOCK = 256
HGRN_CHUNK = 64
ATTN_SUB_ROWS = 256
HGRN_BLOCK = 256
HGRN_SUB = 16


def _cparams(sem):
    return pltpu.CompilerParams(dimension_semantics=sem, vmem_limit_bytes=VMEM_LIMIT)


def _seg_sum(x, g):
    hi = x.astype(BF16)
    lo = (x - hi.astype(F32)).astype(BF16)
    d = lambda a: jnp.dot(a, g, preferred_element_type=F32)
    return d(hi) + d(lo)


def _dot(a, b):
    return jnp.dot(a.astype(BF16), b.astype(BF16), preferred_element_type=F32)


def _dot_nt(a, b):
    return lax.dot_general(a.astype(BF16), b.astype(BF16), (((1,), (1,)), ((), ())),
                           preferred_element_type=F32)


def _dot_tn(a, b):
    return lax.dot_general(a.astype(BF16), b.astype(BF16), (((0,), (0,)), ((), ())),
                           preferred_element_type=F32)


def _softplus(u):
    return jnp.maximum(u, 0.0) + jnp.log(1.0 + jnp.exp(-jnp.abs(u)))


def _scan_cumsum(x, n, reverse, seg=None):
    seg = n if seg is None else seg
    row = lax.broadcasted_iota(jnp.int32, x.shape, 0) & (seg - 1)
    s = 1
    while s < seg:
        if reverse:
            x = x + jnp.where(row < seg - s, pltpu.roll(x, n - s, 0), 0.0)
        else:
            x = x + jnp.where(row >= s, pltpu.roll(x, s, 0), 0.0)
        s *= 2
    return x


def _scan_shift(x, carry_row, n, reverse):
    row = lax.broadcasted_iota(jnp.int32, x.shape, 0)
    if reverse:
        return jnp.where(row == n - 1, carry_row, pltpu.roll(x, n - 1, 0))
    return jnp.where(row == 0, carry_row, pltpu.roll(x, 1, 0))


def _ffn_kernel(x_ref, g_ref, wg_ref, wu_ref, wo_ref, o_ref, hn_ref):
    @pl.when(pl.program_id(1) == 0)
    def _():
        x = x_ref[...]
        ms = jnp.mean(x * x, axis=-1, keepdims=True)
        hn_ref[...] = (x * lax.rsqrt(ms + NORM_EPS) * g_ref[...]).astype(BF16)
        o_ref[...] = x

    h = hn_ref[...]
    gate = jnp.dot(h, wg_ref[...], preferred_element_type=F32)
    up = jnp.dot(h, wu_ref[...], preferred_element_type=F32)
    act = (0.5 * gate * jax.nn.sigmoid(gate) * up).astype(BF16)
    o_ref[...] += jnp.dot(act, wo_ref[...], preferred_element_type=F32)


def _ffn(x2, gain, w_in, w_out, layer, which, tm=1024, tf=512):
    n, d = x2.shape
    nf = D_FF // tf
    return pl.pallas_call(
        _ffn_kernel,
        grid=(n // tm, nf),
        in_specs=[
            pl.BlockSpec((tm, d), lambda i, j: (i, 0)),
            pl.BlockSpec((1, d), lambda i, j: (0, 0)),
            pl.BlockSpec((None, None, d, tf), lambda i, j: (layer, which, 0, j)),
            pl.BlockSpec((None, None, d, tf), lambda i, j: (layer, which, 0, j + nf)),
            pl.BlockSpec((None, None, tf, d), lambda i, j: (layer, which, j, 0)),
        ],
        out_specs=pl.BlockSpec((tm, d), lambda i, j: (i, 0)),
        out_shape=jax.ShapeDtypeStruct((n, d), F32),
        scratch_shapes=[pltpu.VMEM((tm, d), BF16)],
        compiler_params=_cparams(("parallel", "arbitrary")),
        name="ffn",
    )(x2, gain, w_in, w_in, w_out)


def _inproj_kernel(x_ref, g_ref, w_ref, o_ref, hn_ref):
    @pl.when(pl.program_id(1) == 0)
    def _():
        x = x_ref[...]
        ms = jnp.mean(x * x, axis=-1, keepdims=True)
        hn_ref[...] = (x * lax.rsqrt(ms + NORM_EPS) * g_ref[...]).astype(BF16)

    o_ref[...] = jnp.dot(hn_ref[...], w_ref[...], preferred_element_type=F32)


def _inproj(x2, gain, w_ext, tm=512, tn=2304):
    n, d = x2.shape
    nc = w_ext.shape[1]
    return pl.pallas_call(
        _inproj_kernel,
        grid=(n // tm, nc // tn),
        in_specs=[
            pl.BlockSpec((tm, d), lambda i, j: (i, 0)),
            pl.BlockSpec((1, d), lambda i, j: (0, 0)),
            pl.BlockSpec((d, tn), lambda i, j: (0, j)),
        ],
        out_specs=pl.BlockSpec((tm, tn), lambda i, j: (i, j)),
        out_shape=jax.ShapeDtypeStruct((n, nc), F32),
        scratch_shapes=[pltpu.VMEM((tm, d), BF16)],
        compiler_params=_cparams(("parallel", "arbitrary")),
        name="inproj",
    )(x2, gain, w_ext)


def _norm_rope(t, gain, g, c, s1, s2, scale):
    w = t.shape[-1]
    ss = _seg_sum(t * t, g)
    y = t * lax.rsqrt(ss * (1.0 / HEAD_DIM) + NORM_EPS) * gain
    half = ROT_DIM // 2
    out = y * c + pltpu.roll(y, w - half, 1) * s1 + pltpu.roll(y, half, 1) * s2
    return out * scale if scale != 1.0 else out


def _prep_kernel(aq_ref, ak_ref, av_ref, cq_ref, ck_ref, cv_ref, c_ref, s1_ref, s2_ref,
                 gain_ref, g_ref,
                 qa_ref, ka_ref, va_ref, qa4_ref, ka4_ref, va4_ref, qa16_ref, ka16_ref, va16_ref,
                 qc_ref, kc_ref, vc_ref, scr_ref, *, tm):
    c1, s11, s21 = c_ref[...], s1_ref[...], s2_ref[...]
    c4 = jnp.concatenate([c1] * 4, axis=1)
    s14 = jnp.concatenate([s11] * 4, axis=1)
    s24 = jnp.concatenate([s21] * 4, axis=1)
    g = g_ref[...]
    g1 = g_ref[0:128, 0:128]
    scale = HEAD_DIM ** -0.5

    def emit(y, nat_ref, dil_refs):
        nat_ref[...] = y.astype(BF16)
        for c in range(4):
            scr_ref[c] = y[:, c * 128:(c + 1) * 128]
        for d, ref in dil_refs:
            for r in range(d):
                for c in range(4):
                    lo = r * 512 + c * 128
                    ref[:, lo:lo + 128] = scr_ref[c, pl.ds(r, tm // d, stride=d), :].astype(BF16)

    emit(_norm_rope(aq_ref[...], gain_ref[0:1, :], g, c4, s14, s24, scale), qa_ref,
         ((4, qa4_ref), (16, qa16_ref)))
    emit(_norm_rope(ak_ref[...], gain_ref[1:2, :], g, c4, s14, s24, 1.0), ka_ref,
         ((4, ka4_ref), (16, ka16_ref)))
    emit(av_ref[...], va_ref, ((4, va4_ref), (16, va16_ref)))
    qc_ref[...] = _norm_rope(cq_ref[...], gain_ref[2:3, :], g, c4, s14, s24, scale).astype(BF16)
    kc_ref[...] = _norm_rope(ck_ref[...], gain_ref[3:4, 0:128], g1, c1, s11, s21, 1.0).astype(BF16)
    vc_ref[...] = cv_ref[...].astype(BF16)


def _prep(p, rope_c, rope_s1, rope_s2, gains, g512, seq, tm=512):
    n = p.shape[0]
    nseq = seq // tm
    wide = lambda cb: pl.BlockSpec((tm, 512), lambda i, cb=cb: (i, cb))
    narrow = lambda cb: pl.BlockSpec((tm, 128), lambda i, cb=cb: (i, cb))
    tab = pl.BlockSpec((tm, 128), lambda i: (i % nseq, 0))
    rowblk = lambda rows, width: pl.BlockSpec((rows, width), lambda i: (i, 0))
    shape = lambda rows, width: jax.ShapeDtypeStruct((rows, width), BF16)
    trio = lambda d: [rowblk(tm // d, d * 512)] * 3
    trio_shape = lambda d: [shape(n // d, d * 512)] * 3
    return pl.pallas_call(
        functools.partial(_prep_kernel, tm=tm),
        grid=(n // tm,),
        in_specs=[wide(CB_AQ), wide(CB_AK), wide(CB_AV), wide(CB_CQ), narrow(CB_CK), narrow(CB_CV),
                  tab, tab, tab,
                  pl.BlockSpec((8, 512), lambda i: (0, 0)),
                  pl.BlockSpec((512, 512), lambda i: (0, 0))],
        out_specs=trio(1) + trio(4) + trio(16) + [rowblk(tm, 512), rowblk(tm, 128), rowblk(tm, 128)],
        out_shape=trio_shape(1) + trio_shape(4) + trio_shape(16) + [shape(n, 512), shape(n, 128), shape(n, 128)],
        scratch_shapes=[pltpu.VMEM((4, tm, 128), F32)],
        compiler_params=_cparams(("parallel",)),
        name="prep_qk",
    )(p, p, p, p, p, p, rope_c, rope_s1, rope_s2, gains, g512)


def _band_kernel(*refs, tq, hb, sb, half, length, n_kv, group, with_sink):
    if with_sink:
        q_ref, kp_ref, km_ref, kn_ref, vp_ref, vm_ref, vn_ref, sink_ref, o_ref = refs
    else:
        q_ref, kp_ref, km_ref, kn_ref, vp_ref, vm_ref, vn_ref, o_ref, lse_ref = refs
    i = pl.program_id(2)
    nsub = tq // sb
    gs, nwin = group * sb, sb + 2 * hb
    row = lax.broadcasted_iota(jnp.int32, (gs, nwin), 0) & (sb - 1)
    col = lax.broadcasted_iota(jnp.int32, (gs, nwin), 1)
    band = jnp.abs(row + hb - col) <= half
    valid = []
    for j in range(nsub):
        kpos = i * tq + j * sb - hb + col
        valid.append(band & (kpos >= 0) & (kpos < length))
    q_all = q_ref[0]
    n_heads = n_kv * group
    hsl = [slice(h * HEAD_DIM, (h + 1) * HEAD_DIM) for h in range(n_heads)]
    k_heads = [jnp.concatenate([kp_ref[0, :, s], km_ref[0, :, s], kn_ref[0, :, s]], axis=0) for s in hsl[:n_kv]]
    v_heads = [jnp.concatenate([vp_ref[0, :, s], vm_ref[0, :, s], vn_ref[0, :, s]], axis=0) for s in hsl[:n_kv]]
    kvs = range(n_kv)
    if with_sink:
        sinks = sink_ref[...]
        rowg = lax.broadcasted_iota(jnp.int32, (gs, 1), 0) // sb
        sink_col = []
        for kv in kvs:
            col_kv = jnp.zeros((gs, 1), F32)
            for g in range(group):
                h = kv * group + g
                col_kv = jnp.where(rowg == g, sinks[0:1, h:h + 1], col_kv)
            sink_col.append(col_kv)
    outs, lses = [], []
    for j in range(nsub):
        rows, win = slice(j * sb, (j + 1) * sb), slice(j * sb, (j + 1) * sb + 2 * hb)
        qs = [jnp.concatenate([q_all[rows, hsl[kv * group + g]] for g in range(group)], axis=0) for kv in kvs]
        s = [lax.dot_general(qs[kv], k_heads[kv][win], (((1,), (1,)), ((), ())),
                             preferred_element_type=F32) for kv in kvs]
        s = [jnp.where(valid[j], x, NEG_INF) for x in s]
        m = [jnp.max(x, axis=-1, keepdims=True) for x in s]
        p = [jnp.exp(x - y) for x, y in zip(s, m)]
        l = [jnp.sum(x, axis=-1, keepdims=True) for x in p]
        acc = [jnp.dot(p[kv].astype(BF16), v_heads[kv][win], preferred_element_type=F32) for kv in kvs]
        if with_sink:
            m_all = [jnp.maximum(x, y) for x, y in zip(m, sink_col)]
            sc = [jnp.exp(x - y) for x, y in zip(m, m_all)]
            den = [l[kv] * sc[kv] + jnp.exp(sink_col[kv] - m_all[kv]) for kv in kvs]
            o = [acc[kv] * (sc[kv] / den[kv]) for kv in kvs]
        else:
            o = [x / y for x, y in zip(acc, l)]
            lse = [x + jnp.log(y) for x, y in zip(m, l)]
            lses.append(jnp.concatenate(
                [jnp.broadcast_to(lse[h // group][(h % group) * sb:(h % group + 1) * sb], (sb, HEAD_DIM))
                 for h in range(n_heads)], axis=1))
        outs.append(jnp.concatenate([o[h // group][(h % group) * sb:(h % group + 1) * sb]
                                     for h in range(n_heads)], axis=1))
    for j in range(nsub):
        o_ref[0, j * sb:(j + 1) * sb, :] = outs[j].astype(o_ref.dtype)
        if not with_sink:
            lse_ref[0, j * sb:(j + 1) * sb, :] = lses[j]


def _band_attention(q, k, v, sink, batch, seq, dil, half, n_kv, group, tq=512):
    wq, wk = q.shape[1] // dil, k.shape[1] // dil
    hb = half
    sl = seq // dil
    tq = min(tq, sl)
    nq = sl // tq
    per = tq // hb
    nhb = sl // hb
    qv = q.reshape(batch, sl, dil * wq)
    kv = k.reshape(batch, sl, dil * wk)
    vv = v.reshape(batch, sl, dil * wk)
    main = lambda w: pl.BlockSpec((1, tq, w), lambda b, r, i: (b, i, r))
    prev = pl.BlockSpec((1, hb, wk), lambda b, r, i: (b, jnp.maximum(i * per - 1, 0), r))
    nxt = pl.BlockSpec((1, hb, wk), lambda b, r, i: (b, jnp.minimum((i + 1) * per, nhb - 1), r))
    with_sink = sink is not None
    in_specs = [main(wq), prev, main(wk), nxt, prev, main(wk), nxt]
    args = [qv, kv, kv, kv, vv, vv, vv]
    if with_sink:
        in_specs.append(pl.BlockSpec((1, wq // HEAD_DIM), lambda b, r, i: (0, 0)))
        args.append(sink)
        out_specs = main(wq)
        out_shape = jax.ShapeDtypeStruct(qv.shape, BF16)
    else:
        out_specs = [main(wq), main(wq)]
        out_shape = [jax.ShapeDtypeStruct(qv.shape, BF16), jax.ShapeDtypeStruct(qv.shape, F32)]
    kern = functools.partial(_band_kernel, tq=tq, hb=hb, sb=min(tq, max(hb, ATTN_SUB_ROWS // group)),
                             half=half, length=sl,
                             n_kv=n_kv, group=group, with_sink=with_sink)
    out = pl.pallas_call(
        kern,
        grid=(batch, dil, nq),
        in_specs=in_specs,
        out_specs=out_specs,
        out_shape=out_shape,
        compiler_params=_cparams(("parallel", "parallel", "parallel")),
        name="band_attn_sink" if with_sink else f"band_attn_d{dil}",
    )(*args)
    flat = lambda t: t.reshape(batch * sl, dil * wq)
    if with_sink:
        return flat(out)
    return flat(out[0]), flat(out[1])


def _rwkv_kernel(r_ref, k_ref, v_ref, sa_ref, sb_ref, pv_ref, w2_ref, a2_ref, g_ref,
                 y_ref, bonus_ref, state_ref, carry_ref, carryb_ref, *, n, cs, reverse):
    @pl.when(pl.program_id(1) == 0)
    def _():
        state_ref[...] = jnp.zeros_like(state_ref)
        carry_ref[...] = jnp.zeros_like(carry_ref)
        carryb_ref[...] = jnp.zeros_like(carryb_ref)

    last = 0 if reverse else n - 1
    r0, k0, v0, sb = r_ref[...], k_ref[...], v_ref[...], sb_ref[...]
    shift = lambda x, c: _scan_shift(x, c, n, reverse)
    r = r0 + (shift(r0, carry_ref[0:1, :]) - r0) * pv_ref[0:1, :]
    k = k0 + (shift(k0, carry_ref[1:2, :]) - k0) * pv_ref[1:2, :]
    v = v0 + (shift(v0, carry_ref[2:3, :]) - v0) * pv_ref[2:3, :]
    xlr = sa_ref[...] + shift(sb, carryb_ref[0:1, :])
    carry_ref[0:1, :] = r0[last:last + 1, :]
    carry_ref[1:2, :] = k0[last:last + 1, :]
    carry_ref[2:3, :] = v0[last:last + 1, :]
    carryb_ref[0:1, :] = sb[last:last + 1, :]

    wl = pv_ref[3:4, :] + jnp.dot(jnp.tanh(xlr).astype(BF16), w2_ref[...], preferred_element_type=F32)
    w_log = -_softplus(-wl) - 0.5
    ld = -jnp.exp(w_log)
    a = jax.nn.sigmoid(pv_ref[4:5, :] + jnp.dot(xlr.astype(BF16), a2_ref[...], preferred_element_type=F32))
    g = g_ref[...]
    kkr = k * pv_ref[5:6, :]
    kx = k * (1.0 + (a - 1.0) * pv_ref[6:7, :])
    sums = _seg_sum(jnp.concatenate([kkr * kkr, r * kx * pv_ref[7:8, :]], axis=0), g)
    kk = kkr / jnp.maximum(jnp.sqrt(sums[:n]), 1e-12)
    bonus_ref[...] = (sums[n:] * v).astype(bonus_ref.dtype)

    nsub = n // cs
    last_row = lambda gi: gi * cs if reverse else (gi + 1) * cs - 1
    row1 = lax.broadcasted_iota(jnp.int32, (n, 1), 0)
    c = _scan_cumsum(ld, n, reverse, seg=cs)
    c_last = c[last_row(0):last_row(0) + 1, :]
    for gi in range(1, nsub):
        c_last = jnp.where(row1 >= gi * cs, c[last_row(gi):last_row(gi) + 1, :], c_last)
    e_c = jnp.exp(c)
    e_cx = jnp.exp(c - ld)
    e_nc = jnp.exp(-c)
    e_end = jnp.exp(c_last - c)
    kb = kk * a
    rt = r * e_c
    at = -kk * e_cx
    bt = kb * e_nc
    kt = kx * e_nc
    bh = kb * e_end
    kh = kx * e_end

    rt_b, at_b, bt_b, kt_b, v_b = (t.astype(BF16) for t in (rt, at, bt, kt, v))
    bh_b, kh_b = bh.astype(BF16), kh.astype(BF16)

    row = lax.broadcasted_iota(jnp.int32, (2 * cs, 2 * cs), 0)
    col = lax.broadcasted_iota(jnp.int32, (2 * cs, 2 * cs), 1)
    rr, cc = row & (cs - 1), col & (cs - 1)
    strict = (cc > rr) if reverse else (cc < rr)
    mask = strict | ((cc == rr) & (row >= cs))
    eye = (lax.broadcasted_iota(jnp.int32, (cs, cs), 0) == lax.broadcasted_iota(jnp.int32, (cs, cs), 1)).astype(F32)

    heads = range(GROUP_WIDTH // HEAD_DIM)
    nh = len(heads)
    cat = lambda a, b: jnp.concatenate([a, b], axis=0)
    pairs = [(slice(gi * cs, (gi + 1) * cs), slice(h * HEAD_DIM, (h + 1) * HEAD_DIM))
             for gi in range(nsub) for h in heads]
    big = [jnp.where(mask, _dot_nt(cat(at_b[rs, s], rt_b[rs, s]), cat(bt_b[rs, s], kt_b[rs, s])), 0.0)
           for rs, s in pairs]
    akv = [_dot(b[:cs, cs:], v_b[rs, s]) for b, (rs, s) in zip(big, pairs)]
    a_ab = [b[:cs, :cs] for b in big]
    t_inv = [eye + x for x in a_ab]
    pw = [_dot(x, x) for x in a_ab]
    lvl = 4
    while lvl < cs:
        res = [_dot(cat(x, t), x) for x, t in zip(pw, t_inv)]
        t_inv = [t + x[cs:] for t, x in zip(t_inv, res)]
        pw = [x[:cs] for x in res]
        lvl *= 2
    t_inv = [t + _dot(t, x) for t, x in zip(t_inv, pw)]
    ta = [_dot(t, at_b[rs, s]) for t, (rs, s) in zip(t_inv, pairs)]
    tv = [_dot(t, x) for t, x in zip(t_inv, akv)]

    st = [state_ref[h] for h in heads]
    ys = [None] * nsub
    for gi in (range(nsub - 1, -1, -1) if reverse else range(nsub)):
        wc = jnp.exp(c[last_row(gi):last_row(gi) + 1, :])
        sl = slice(gi * nh, (gi + 1) * nh)
        u = [_dot_nt(x, s0) + z for x, s0, z in zip(ta[sl], st, tv[sl])]
        uv = [cat(x, v[rs, s]).astype(BF16) for x, (rs, s) in zip(u, pairs[sl])]
        ys[gi] = [_dot_nt(rt_b[rs, s], s0) + _dot(b[cs:, :], x)
                  for (rs, s), s0, b, x in zip(pairs[sl], st, big[sl], uv)]
        st = [s0 * wc[:, s] + _dot_tn(x, cat(bh_b[rs, s], kh_b[rs, s]))
              for s0, (rs, s), x in zip(st, pairs[sl], uv)]
    for gi in range(nsub):
        for h in heads:
            rs, s = pairs[gi * nh + h]
            y_ref[rs, s] = ys[gi][h]
    for h in heads:
        state_ref[h] = st[h]


def _rwkv(p, pvec, w2pad, a2pad, g512, batch, seq, reverse):
    n = p.shape[0]
    c = RWKV_BLOCK
    nc = seq // c
    if reverse:
        rowidx = lambda b, j: b * nc + (nc - 1 - j)
    else:
        rowidx = lambda b, j: b * nc + j
    wide = lambda cb: pl.BlockSpec((c, 512), lambda b, j, cb=cb: (rowidx(b, j), cb))
    narrow = lambda cb: pl.BlockSpec((c, 128), lambda b, j, cb=cb: (rowidx(b, j), cb))
    const = lambda shape: pl.BlockSpec(shape, lambda b, j: (0,) * len(shape))
    out = pl.BlockSpec((c, 512), lambda b, j: (rowidx(b, j), 0))
    return pl.pallas_call(
        functools.partial(_rwkv_kernel, n=c, cs=RWKV_CHUNK, reverse=reverse),
        grid=(batch, nc),
        in_specs=[wide(CB_BR), wide(CB_BK), wide(CB_BV), narrow(CB_SECA), narrow(CB_SECB),
                  const((8, 512)), const((128, 512)), const((128, 512)), const((512, 512))],
        out_specs=[out, out],
        out_shape=[jax.ShapeDtypeStruct((n, 512), F32), jax.ShapeDtypeStruct((n, 512), BF16)],
        scratch_shapes=[pltpu.VMEM((8, HEAD_DIM, HEAD_DIM), F32), pltpu.VMEM((8, 512), F32),
                        pltpu.VMEM((8, 128), F32)],
        compiler_params=_cparams(("parallel", "arbitrary")),
        name="rwkv_bwd" if reverse else "rwkv_fwd",
    )(p, p, p, p, p, pvec, w2pad, a2pad, g512)


def _hgrn_kernel(q_ref, v_ref, z_ref, lb_ref, o_ref, state_ref, *, n, cs, sub, reverse):
    @pl.when(pl.program_id(1) == 0)
    def _():
        state_ref[...] = jnp.zeros_like(state_ref)

    q, v, z = q_ref[...], v_ref[...], z_ref[...]
    log_sig = -_softplus(-z)
    x1 = lb_ref[0:1, :]
    x2 = lb_ref[1:2, :] + log_sig
    logf = jnp.maximum(x1, x2) + jnp.log(1.0 + jnp.exp(-jnp.abs(x1 - x2)))
    kf = 1.0 - jnp.exp(logf)
    nchunk, nsub = n // cs, cs // sub
    last_row = lambda gi: gi * cs if reverse else (gi + 1) * cs - 1
    row1 = lax.broadcasted_iota(jnp.int32, (n, 1), 0)
    b = _scan_cumsum(logf, n, reverse, seg=cs)
    b_last = b[last_row(0):last_row(0) + 1, :]
    for gi in range(1, nchunk):
        b_last = jnp.where(row1 >= gi * cs, b[last_row(gi):last_row(gi) + 1, :], b_last)
    qe = q * jnp.exp(b)
    ke = kf * jnp.exp(b_last - b)
    b2 = b * LOG2_E
    c2 = b2 - jnp.log(kf) * LOG2_E
    srow = lax.broadcasted_iota(jnp.int32, (sub, sub), 0)
    lane = lax.broadcasted_iota(jnp.int32, (sub, sub), 1)
    tri = (srow <= lane) if reverse else (srow >= lane)
    hsl = [slice(h * D_HEAD_K, (h + 1) * D_HEAD_K) for h in range(D_HEADS)]

    intra = {}
    for gi in range(nchunk):
        base = gi * cs
        for h, hs in enumerate(hsl):
            for m in range(nsub):
                ms = slice(base + m * sub, base + (m + 1) * sub)
                q_m, b_m = q[ms, hs], b[ms, hs]
                b2_m, c2_m = b2[ms, hs], c2[ms, hs]
                dmat = jnp.zeros((sub, sub), F32)
                for s in range(sub):
                    w = q_m * jnp.exp2(b2_m - c2_m[s:s + 1])
                    dmat = jnp.where(lane == s, jnp.sum(w, axis=-1, keepdims=True), dmat)
                dmat = jnp.where(tri, dmat, 0.0)
                o_m = _dot(dmat, v[ms, hs])
                es = slice(base + (m + 1) * sub, base + cs) if reverse else slice(base, base + m * sub)
                if es.stop > es.start:
                    first = base + ((m + 1) * sub - 1 if reverse else m * sub)
                    rho = b[first:first + 1, hs] - logf[first:first + 1, hs]
                    qt = q_m * jnp.exp(b_m - rho)
                    kt = kf[es, hs] * jnp.exp(rho - b[es, hs])
                    o_m = o_m + _dot(_dot_nt(qt, kt), v[es, hs])
                intra[gi, h, m] = o_m

    st = [state_ref[h] for h in range(D_HEADS)]
    inter = {}
    for gi in (range(nchunk - 1, -1, -1) if reverse else range(nchunk)):
        rs = slice(gi * cs, (gi + 1) * cs)
        wend = jnp.exp(b[last_row(gi):last_row(gi) + 1, :])
        for h, hs in enumerate(hsl):
            inter[gi, h] = _dot_nt(qe[rs, hs], st[h])
        st = [st[h] * wend[:, hs] + _dot_tn(v[rs, hs], ke[rs, hs]) for h, hs in enumerate(hsl)]
    for (gi, h, m), o_m in intra.items():
        lo = gi * cs + m * sub
        o_ref[lo:lo + sub, hsl[h]] = (o_m + inter[gi, h][m * sub:(m + 1) * sub]).astype(o_ref.dtype)
    for h in range(D_HEADS):
        state_ref[h] = st[h]


def _hgrn(p, lbvec, batch, seq, reverse):
    n = p.shape[0]
    c = HGRN_BLOCK
    nc = seq // c
    if reverse:
        rowidx = lambda b, j: b * nc + (nc - 1 - j)
    else:
        rowidx = lambda b, j: b * nc + j
    wide = lambda cb: pl.BlockSpec((c, 512), lambda b, j, cb=cb: (rowidx(b, j), cb))
    return pl.pallas_call(
        functools.partial(_hgrn_kernel, n=c, cs=HGRN_CHUNK, sub=HGRN_SUB, reverse=reverse),
        grid=(batch, nc),
        in_specs=[wide(CB_DQ), wide(CB_DI), wide(CB_DFB if reverse else CB_DFF),
                  pl.BlockSpec((8, 512), lambda b, j: (0, 0))],
        out_specs=pl.BlockSpec((c, 512), lambda b, j: (rowidx(b, j), 0)),
        out_shape=jax.ShapeDtypeStruct((n, 512), BF16),
        scratch_shapes=[pltpu.VMEM((D_HEADS, D_HEAD_K, D_HEAD_K), F32)],
        compiler_params=_cparams(("parallel", "arbitrary")),
        name="hgrn_bwd" if reverse else "hgrn_fwd",
    )(p, p, p, lbvec)


def _mixpost_kernel(o1_ref, l1_ref, o2_ref, l2_ref, o3_ref, l3_ref,
                    y0_ref, y1_ref, bo0_ref, bo1_ref, g1_ref, oc_ref,
                    h0_ref, h1_ref, dg_ref, pv_ref, g_ref, g2_ref, mix_ref,
                    o2s_ref, l2s_ref, o3s_ref, l3s_ref, *, tm):
    for d, src, dst in ((4, o2_ref, o2s_ref), (4, l2_ref, l2s_ref), (16, o3_ref, o3s_ref), (16, l3_ref, l3s_ref)):
        for r in range(d):
            for c in range(4):
                lo = r * 512 + c * 128
                dst[c, pl.ds(r, tm // d, stride=d), :] = src[:, lo:lo + 128].astype(F32)
    wide = lambda ref: jnp.concatenate([ref[c] for c in range(4)], axis=1)
    l1, l2, l3 = l1_ref[...], wide(l2s_ref), wide(l3s_ref)
    mx = jnp.maximum(jnp.maximum(l1, l2), l3)
    w1, w2, w3 = jnp.exp(l1 - mx), jnp.exp(l2 - mx), jnp.exp(l3 - mx)
    out_a = (o1_ref[...] * w1 + wide(o2s_ref) * w2 + wide(o3s_ref) * w3) / (w1 + w2 + w3)
    mix_ref[:, 0:512] = out_a.astype(BF16)
    g = g_ref[...]
    y = y0_ref[...] + y1_ref[...]
    mu = _seg_sum(y, g) * (1.0 / HEAD_DIM)
    yc = y - mu
    var = _seg_sum(yc * yc, g) * (1.0 / HEAD_DIM)
    yn = yc * lax.rsqrt(var + LNX_EPS) * pv_ref[0:1, :] + pv_ref[1:2, :]
    gate = jnp.dot(jax.nn.sigmoid(g1_ref[...]).astype(BF16), g2_ref[...], preferred_element_type=F32)
    mix_ref[:, 512:1024] = ((yn + bo0_ref[...] + bo1_ref[...]) * gate).astype(BF16)
    mix_ref[:, 1024:1536] = oc_ref[...]
    o = h0_ref[...].astype(F32) + h1_ref[...].astype(F32)
    dg = dg_ref[...]
    silu = dg * jax.nn.sigmoid(dg)
    for h in range(D_HEADS):
        hs = slice(h * D_HEAD_K, (h + 1) * D_HEAD_K)
        oh = o[:, hs]
        ms = jnp.mean(oh * oh, axis=-1, keepdims=True)
        res = oh * lax.rsqrt(ms + NORM_EPS) * pv_ref[2:3, hs] * silu[:, hs]
        mix_ref[:, 1536 + h * D_HEAD_K:1536 + (h + 1) * D_HEAD_K] = res.astype(BF16)


def _mixpost(a_parts, yb, bonus, p, oc, od, pvec, g512, g2pad, tm=256):
    n = p.shape[0]
    blk = pl.BlockSpec((tm, 512), lambda i: (i, 0))
    const = lambda shape: pl.BlockSpec(shape, lambda i: (0,) * len(shape))
    args = []
    for o, l in a_parts:
        args += [o, l]
    args += [yb[0], yb[1], bonus[0], bonus[1], p, oc, od[0], od[1], p, pvec, g512, g2pad]
    dil = lambda d: pl.BlockSpec((tm // d, d * 512), lambda i: (i, 0))
    in_specs = [blk, blk, dil(4), dil(4), dil(16), dil(16)] + [blk] * 4 + [
        pl.BlockSpec((tm, 128), lambda i: (i, CB_G1)), blk, blk, blk,
        pl.BlockSpec((tm, 512), lambda i: (i, CB_DG)),
        const((8, 512)), const((512, 512)), const((128, 512))]
    return pl.pallas_call(
        functools.partial(_mixpost_kernel, tm=tm),
        grid=(n // tm,),
        in_specs=in_specs,
        out_specs=pl.BlockSpec((tm, D_MODEL), lambda i: (i, 0)),
        out_shape=jax.ShapeDtypeStruct((n, D_MODEL), BF16),
        scratch_shapes=[pltpu.VMEM((4, tm, 128), F32)] * 4,
        compiler_params=_cparams(("parallel",)),
        name="mixpost",
    )(*args)


def _outproj_kernel(mix_ref, w_ref, x_ref, o_ref):
    o_ref[...] = x_ref[...] + jnp.dot(mix_ref[...], w_ref[...], preferred_element_type=F32)


def _outproj(mix, w, x2, tm=512, tn=2048):
    n, d = x2.shape
    return pl.pallas_call(
        _outproj_kernel,
        grid=(n // tm, d // tn),
        in_specs=[pl.BlockSpec((tm, mix.shape[1]), lambda i, j: (i, 0)),
                  pl.BlockSpec((mix.shape[1], tn), lambda i, j: (0, j)),
                  pl.BlockSpec((tm, tn), lambda i, j: (i, j))],
        out_specs=pl.BlockSpec((tm, tn), lambda i, j: (i, j)),
        out_shape=jax.ShapeDtypeStruct((n, d), F32),
        compiler_params=_cparams(("parallel", "arbitrary")),
        name="outproj",
    )(mix, w, x2)


def _rope_tables(seq):
    half = ROT_DIM // 2
    inv = ROPE_THETA ** (-jnp.arange(0, ROT_DIM, 2, dtype=F32) / ROT_DIM)
    ang = jnp.arange(seq, dtype=F32)[:, None] * inv[None]
    cos, sin = jnp.cos(ang), jnp.sin(ang)
    ones = jnp.ones((seq, HEAD_DIM - ROT_DIM), F32)
    zeros = jnp.zeros((seq, HEAD_DIM - ROT_DIM), F32)
    z8 = jnp.zeros((seq, half), F32)
    c = jnp.concatenate([cos, cos, ones], axis=1)
    s1 = jnp.concatenate([-sin, z8, zeros], axis=1)
    s2 = jnp.concatenate([z8, sin, zeros], axis=1)
    tile2 = lambda t: jnp.concatenate([t, t], axis=1)
    return tile2(c), tile2(s1), tile2(s2)


def _block_diag_ones(width, seg):
    idx = jnp.arange(width) // seg
    return (idx[:, None] == idx[None, :]).astype(BF16)


def _pad_rows(rows, width=512, total=8):
    rows = [jnp.pad(r.astype(F32).reshape(-1), (0, width - r.size)) for r in rows]
    rows += [jnp.zeros((width,), F32)] * (total - len(rows))
    return jnp.stack(rows)


def _w_ext(w_in, mu_wa, w1, a1, g1):
    cols_a, cols_b = [], []
    for d in range(2):
        for mu, w in ((mu_wa[d, 0], w1[d]), (mu_wa[d, 1], a1[d])):
            cols_a.append((1.0 - mu)[:, None] * w)
            cols_b.append(mu[:, None] * w)
    parts = [w_in[:, 0:3584], w_in[:, 3840:6400], w_in[:, 3584:3840]] + cols_a + cols_b
    parts.append(jnp.pad(g1, ((0, 0), (0, 128 - g1.shape[1]))))
    w = jnp.concatenate(parts, axis=1)
    return jnp.pad(w, ((0, 0), (0, D_IN_EXT - w.shape[1]))).astype(BF16)


def _pad_lowrank(w, row0):
    return jnp.pad(w, ((row0, 128 - row0 - w.shape[0]), (0, 0))).astype(BF16)


def kernel(x, ln_gain, w_in, w_out, w_ffn_in, w_ffn_out, qk_gain, sink, b_mu_rkv, b_mu_wa, b_w0, b_w1,
           b_w2, b_a0, b_a1, b_a2, b_k_k, b_k_a, b_r_k, b_lnx_gain, b_lnx_bias, b_g1, b_g2,
           d_lb_logits, d_norm_gain):
    batch, seq, d = x.shape
    x2 = x.reshape(batch * seq, d)
    rope_c, rope_s1, rope_s2 = _rope_tables(seq)
    g512 = _block_diag_ones(512, HEAD_DIM)
    pr = jax.nn.softmax(d_lb_logits.astype(F32), axis=0)
    cs = jnp.cumsum(pr, axis=0)
    lb_all = cs - cs[0:1]

    ffn_in, ffn_out = w_ffn_in.astype(BF16), w_ffn_out.astype(BF16)

    for l in range(DEPTH):
        x2 = _ffn(x2, ln_gain[l, 0][None], ffn_in, ffn_out, l, 0)

        p = _inproj(x2, ln_gain[l, 1][None], _w_ext(w_in[l], b_mu_wa[l], b_w1[l], b_a1[l], b_g1[l]))

        tile8 = lambda t: jnp.tile(t, 8)
        gains = _pad_rows([tile8(qk_gain[l, 0, 0]), tile8(qk_gain[l, 0, 1]),
                           tile8(qk_gain[l, 1, 0]), jnp.tile(qk_gain[l, 1, 1], 2)])
        prepped = _prep(p, rope_c, rope_s1, rope_s2, gains, g512, seq)
        qc, kc, vc = prepped[9:]

        a_parts = []
        for pi, (window, dil) in enumerate(DILATED_PATTERNS):
            qa, ka, va = prepped[3 * pi:3 * pi + 3]
            a_parts.append(_band_attention(qa, ka, va, None, batch, seq, dil, window // (2 * dil),
                                           GROUP_WIDTH // HEAD_DIM, 1))
        oc = _band_attention(qc, kc, vc, sink[l][None].astype(F32), batch, seq, 1, C_HALF_WINDOW,
                             C_KV_HEADS, C_Q_HEADS // C_KV_HEADS)

        yb, bonus = [], []
        for dr in range(2):
            pvec = _pad_rows([b_mu_rkv[l, dr, 0], b_mu_rkv[l, dr, 1], b_mu_rkv[l, dr, 2], b_w0[l, dr],
                              b_a0[l, dr], b_k_k[l], b_k_a[l], b_r_k[l]])
            y, bo = _rwkv(p, pvec, _pad_lowrank(b_w2[l, dr], 64 * dr), _pad_lowrank(b_a2[l, dr], 64 * dr + 32),
                          g512, batch, seq, reverse=(dr == 1))
            yb.append(y)
            bonus.append(bo)

        od = []
        for dr in range(2):
            lb = lb_all[l, dr]
            od.append(_hgrn(p, _pad_rows([jnp.log(lb), jnp.log1p(-lb)]), batch, seq, reverse=(dr == 1)))

        pvec = _pad_rows([b_lnx_gain[l], b_lnx_bias[l], d_norm_gain[l]])
        mix = _mixpost(a_parts, yb, bonus, p, oc, od, pvec, g512, _pad_lowrank(b_g2[l], 0))
        x2 = _outproj(mix, w_out[l].astype(BF16), x2)

        x2 = _ffn(x2, ln_gain[l, 2][None], ffn_in, ffn_out, l, 1)
    return x2.reshape(batch, seq, d)
```

```python
import functools

import jax
import jax.numpy as jnp
from jax import lax
from jax.experimental import pallas as pl
from jax.experimental.pallas import tpu as pltpu

F32 = jnp.float32
BF16 = jnp.bfloat16

D_MODEL = 2048
DEPTH = 4
HEAD_DIM = 64
GROUP_WIDTH = 512
DILATED_PATTERNS = ((128, 1), (512, 4), (2048, 16))
LNX_EPS = 64e-5
C_Q_HEADS = 8
C_KV_HEADS = 2
C_HALF_WINDOW = 128
D_HEAD_K = 128
D_HEADS = 4
D_FF = 5632
ROPE_THETA = 500000.0
ROT_DIM = 16
NORM_EPS = 1e-6
NEG_INF = -1e30
LOG2_E = 1.4426950408889634

D_IN_EXT = 6912
(CB_AQ, CB_AK, CB_AV, CB_BR, CB_BK, CB_BV, CB_CQ, CB_DQ, CB_DI, CB_DG, CB_DFF, CB_DFB) = range(12)
CB_CK, CB_CV, CB_SECA, CB_SECB, CB_G1 = 48, 49, 50, 51, 52

VMEM_LIMIT = 56 * 1024 * 1024
RWKV_CHUNK = 128
RWKV_BLOCK = 256
HGRN_CHUNK = 64
LSE_WIDTH = 128
ATTN_SUB_ROWS = 256
HGRN_BLOCK = 256
HGRN_SUB = 16


def _cparams(sem):
    return pltpu.CompilerParams(dimension_semantics=sem, vmem_limit_bytes=VMEM_LIMIT)


def _seg_sum(x, g):
    hi = x.astype(BF16)
    lo = (x - hi.astype(F32)).astype(BF16)
    d = lambda a: jnp.dot(a, g, preferred_element_type=F32)
    return d(hi) + d(lo)


def _dot(a, b):
    return jnp.dot(a.astype(BF16), b.astype(BF16), preferred_element_type=F32)


def _dot_nt(a, b):
    return lax.dot_general(a.astype(BF16), b.astype(BF16), (((1,), (1,)), ((), ())),
                           preferred_element_type=F32)


def _dot_tn(a, b):
    return lax.dot_general(a.astype(BF16), b.astype(BF16), (((0,), (0,)), ((), ())),
                           preferred_element_type=F32)


def _softplus(u):
    return jnp.maximum(u, 0.0) + jnp.log(1.0 + jnp.exp(-jnp.abs(u)))


def _scan_cumsum(x, n, reverse, seg=None):
    seg = n if seg is None else seg
    row = lax.broadcasted_iota(jnp.int32, x.shape, 0) & (seg - 1)
    s = 1
    while s < seg:
        if reverse:
            x = x + jnp.where(row < seg - s, pltpu.roll(x, n - s, 0), 0.0)
        else:
            x = x + jnp.where(row >= s, pltpu.roll(x, s, 0), 0.0)
        s *= 2
    return x


def _scan_shift(x, carry_row, n, reverse):
    row = lax.broadcasted_iota(jnp.int32, x.shape, 0)
    if reverse:
        return jnp.where(row == n - 1, carry_row, pltpu.roll(x, n - 1, 0))
    return jnp.where(row == 0, carry_row, pltpu.roll(x, 1, 0))


def _ffn_kernel(x_ref, g_ref, wg_ref, wu_ref, wo_ref, o_ref, hn_ref):
    @pl.when(pl.program_id(1) == 0)
    def _():
        x = x_ref[...]
        ms = jnp.mean(x * x, axis=-1, keepdims=True)
        hn_ref[...] = (x * lax.rsqrt(ms + NORM_EPS) * g_ref[...]).astype(BF16)
        o_ref[...] = x

    h = hn_ref[...]
    gate = jnp.dot(h, wg_ref[...], preferred_element_type=F32)
    up = jnp.dot(h, wu_ref[...], preferred_element_type=F32)
    act = (0.5 * gate * jax.nn.sigmoid(gate) * up).astype(BF16)
    o_ref[...] += jnp.dot(act, wo_ref[...], preferred_element_type=F32)


def _ffn(x2, gain, w_in, w_out, layer, which, tm=1024, tf=512):
    n, d = x2.shape
    nf = D_FF // tf
    return pl.pallas_call(
        _ffn_kernel,
        grid=(n // tm, nf),
        in_specs=[
            pl.BlockSpec((tm, d), lambda i, j: (i, 0)),
            pl.BlockSpec((1, d), lambda i, j: (0, 0)),
            pl.BlockSpec((None, None, d, tf), lambda i, j: (layer, which, 0, j)),
            pl.BlockSpec((None, None, d, tf), lambda i, j: (layer, which, 0, j + nf)),
            pl.BlockSpec((None, None, tf, d), lambda i, j: (layer, which, j, 0)),
        ],
        out_specs=pl.BlockSpec((tm, d), lambda i, j: (i, 0)),
        out_shape=jax.ShapeDtypeStruct((n, d), F32),
        scratch_shapes=[pltpu.VMEM((tm, d), BF16)],
        compiler_params=_cparams(("parallel", "arbitrary")),
        name="ffn",
    )(x2, gain, w_in, w_in, w_out)


def _inproj_kernel(x_ref, g_ref, w_ref, o_ref, hn_ref):
    @pl.when(pl.program_id(1) == 0)
    def _():
        x = x_ref[...]
        ms = jnp.mean(x * x, axis=-1, keepdims=True)
        hn_ref[...] = (x * lax.rsqrt(ms + NORM_EPS) * g_ref[...]).astype(BF16)

    o_ref[...] = jnp.dot(hn_ref[...], w_ref[...], preferred_element_type=F32)


def _inproj(x2, gain, w_ext, tm=512, tn=2304):
    n, d = x2.shape
    nc = w_ext.shape[1]
    return pl.pallas_call(
        _inproj_kernel,
        grid=(n // tm, nc // tn),
        in_specs=[
            pl.BlockSpec((tm, d), lambda i, j: (i, 0)),
            pl.BlockSpec((1, d), lambda i, j: (0, 0)),
            pl.BlockSpec((d, tn), lambda i, j: (0, j)),
        ],
        out_specs=pl.BlockSpec((tm, tn), lambda i, j: (i, j)),
        out_shape=jax.ShapeDtypeStruct((n, nc), F32),
        scratch_shapes=[pltpu.VMEM((tm, d), BF16)],
        compiler_params=_cparams(("parallel", "arbitrary")),
        name="inproj",
    )(x2, gain, w_ext)


def _norm_rope(t, gain, g, c, s1, s2, scale):
    w = t.shape[-1]
    ss = _seg_sum(t * t, g)
    y = t * lax.rsqrt(ss * (1.0 / HEAD_DIM) + NORM_EPS) * gain
    half = ROT_DIM // 2
    out = y * c + pltpu.roll(y, w - half, 1) * s1 + pltpu.roll(y, half, 1) * s2
    return out * scale if scale != 1.0 else out


def _prep_kernel(aq_ref, ak_ref, av_ref, cq_ref, ck_ref, cv_ref, c_ref, s1_ref, s2_ref,
                 gain_ref, g_ref,
                 qa_ref, ka_ref, va_ref, qa4_ref, ka4_ref, va4_ref, qa16_ref, ka16_ref, va16_ref,
                 qc_ref, kc_ref, vc_ref, scr_ref, *, tm):
    c1, s11, s21 = c_ref[...], s1_ref[...], s2_ref[...]
    c4 = jnp.concatenate([c1] * 4, axis=1)
    s14 = jnp.concatenate([s11] * 4, axis=1)
    s24 = jnp.concatenate([s21] * 4, axis=1)
    g = g_ref[...]
    g1 = g_ref[0:128, 0:128]
    scale = HEAD_DIM ** -0.5

    def emit(y, nat_ref, dil_refs):
        nat_ref[...] = y.astype(BF16)
        for c in range(4):
            scr_ref[c] = y[:, c * 128:(c + 1) * 128]
        for d, ref in dil_refs:
            for r in range(d):
                for c in range(4):
                    lo = r * 512 + c * 128
                    ref[:, lo:lo + 128] = scr_ref[c, pl.ds(r, tm // d, stride=d), :].astype(BF16)

    emit(_norm_rope(aq_ref[...], gain_ref[0:1, :], g, c4, s14, s24, scale), qa_ref,
         ((4, qa4_ref), (16, qa16_ref)))
    emit(_norm_rope(ak_ref[...], gain_ref[1:2, :], g, c4, s14, s24, 1.0), ka_ref,
         ((4, ka4_ref), (16, ka16_ref)))
    emit(av_ref[...], va_ref, ((4, va4_ref), (16, va16_ref)))
    qc_ref[...] = _norm_rope(cq_ref[...], gain_ref[2:3, :], g, c4, s14, s24, scale).astype(BF16)
    kc_ref[...] = _norm_rope(ck_ref[...], gain_ref[3:4, 0:128], g1, c1, s11, s21, 1.0).astype(BF16)
    vc_ref[...] = cv_ref[...].astype(BF16)


def _prep(p, rope_c, rope_s1, rope_s2, gains, g512, seq, tm=512):
    n = p.shape[0]
    nseq = seq // tm
    wide = lambda cb: pl.BlockSpec((tm, 512), lambda i, cb=cb: (i, cb))
    narrow = lambda cb: pl.BlockSpec((tm, 128), lambda i, cb=cb: (i, cb))
    tab = pl.BlockSpec((tm, 128), lambda i: (i % nseq, 0))
    rowblk = lambda rows, width: pl.BlockSpec((rows, width), lambda i: (i, 0))
    shape = lambda rows, width: jax.ShapeDtypeStruct((rows, width), BF16)
    trio = lambda d: [rowblk(tm // d, d * 512)] * 3
    trio_shape = lambda d: [shape(n // d, d * 512)] * 3
    return pl.pallas_call(
        functools.partial(_prep_kernel, tm=tm),
        grid=(n // tm,),
        in_specs=[wide(CB_AQ), wide(CB_AK), wide(CB_AV), wide(CB_CQ), narrow(CB_CK), narrow(CB_CV),
                  tab, tab, tab,
                  pl.BlockSpec((8, 512), lambda i: (0, 0)),
                  pl.BlockSpec((512, 512), lambda i: (0, 0))],
        out_specs=trio(1) + trio(4) + trio(16) + [rowblk(tm, 512), rowblk(tm, 128), rowblk(tm, 128)],
        out_shape=trio_shape(1) + trio_shape(4) + trio_shape(16) + [shape(n, 512), shape(n, 128), shape(n, 128)],
        scratch_shapes=[pltpu.VMEM((4, tm, 128), F32)],
        compiler_params=_cparams(("parallel",)),
        name="prep_qk",
    )(p, p, p, p, p, p, rope_c, rope_s1, rope_s2, gains, g512)


def _band_kernel(*refs, tq, hb, sb, half, length, n_kv, group, with_sink):
    if with_sink:
        q_ref, kp_ref, km_ref, kn_ref, vp_ref, vm_ref, vn_ref, sink_ref, o_ref = refs
    else:
        q_ref, kp_ref, km_ref, kn_ref, vp_ref, vm_ref, vn_ref, o_ref, lse_ref = refs
    i = pl.program_id(2)
    nsub = tq // sb
    gs, nwin = group * sb, sb + 2 * hb
    row = lax.broadcasted_iota(jnp.int32, (gs, nwin), 0) & (sb - 1)
    col = lax.broadcasted_iota(jnp.int32, (gs, nwin), 1)
    band = jnp.abs(row + hb - col) <= half
    valid = []
    for j in range(nsub):
        kpos = i * tq + j * sb - hb + col
        valid.append(band & (kpos >= 0) & (kpos < length))
    q_all = q_ref[0]
    n_heads = n_kv * group
    hsl = [slice(h * HEAD_DIM, (h + 1) * HEAD_DIM) for h in range(n_heads)]
    k_heads = [jnp.concatenate([kp_ref[0, :, s], km_ref[0, :, s], kn_ref[0, :, s]], axis=0) for s in hsl[:n_kv]]
    v_heads = [jnp.concatenate([vp_ref[0, :, s], vm_ref[0, :, s], vn_ref[0, :, s]], axis=0) for s in hsl[:n_kv]]
    kvs = range(n_kv)
    if with_sink:
        sinks = sink_ref[...]
        rowg = lax.broadcasted_iota(jnp.int32, (gs, 1), 0) // sb
        sink_col = []
        for kv in kvs:
            col_kv = jnp.zeros((gs, 1), F32)
            for g in range(group):
                h = kv * group + g
                col_kv = jnp.where(rowg == g, sinks[0:1, h:h + 1], col_kv)
            sink_col.append(col_kv)
    lse_lane_head = lax.broadcasted_iota(jnp.int32, (sb, LSE_WIDTH), 1) // (LSE_WIDTH // n_heads)
    outs, lses = [], []
    for j in range(nsub):
        rows, win = slice(j * sb, (j + 1) * sb), slice(j * sb, (j + 1) * sb + 2 * hb)
        qs = [jnp.concatenate([q_all[rows, hsl[kv * group + g]] for g in range(group)], axis=0) for kv in kvs]
        s = [lax.dot_general(qs[kv], k_heads[kv][win], (((1,), (1,)), ((), ())),
                             preferred_element_type=F32) for kv in kvs]
        s = [jnp.where(valid[j], x, NEG_INF) for x in s]
        m = [jnp.max(x, axis=-1, keepdims=True) for x in s]
        p = [jnp.exp(x - y) for x, y in zip(s, m)]
        l = [jnp.sum(x, axis=-1, keepdims=True) for x in p]
        acc = [jnp.dot(p[kv].astype(BF16), v_heads[kv][win], preferred_element_type=F32) for kv in kvs]
        if with_sink:
            m_all = [jnp.maximum(x, y) for x, y in zip(m, sink_col)]
            sc = [jnp.exp(x - y) for x, y in zip(m, m_all)]
            den = [l[kv] * sc[kv] + jnp.exp(sink_col[kv] - m_all[kv]) for kv in kvs]
            o = [acc[kv] * (sc[kv] / den[kv]) for kv in kvs]
        else:
            o = [x / y for x, y in zip(acc, l)]
            lse = [x + jnp.log(y) for x, y in zip(m, l)]
            tile = jnp.zeros((sb, LSE_WIDTH), F32)
            for h in range(n_heads):
                tile = jnp.where(lse_lane_head == h, lse[h // group][(h % group) * sb:(h % group + 1) * sb], tile)
            lses.append(tile)
        outs.append(jnp.concatenate([o[h // group][(h % group) * sb:(h % group + 1) * sb]
                                     for h in range(n_heads)], axis=1))
    for j in range(nsub):
        o_ref[0, j * sb:(j + 1) * sb, :] = outs[j].astype(o_ref.dtype)
        if not with_sink:
            lse_ref[0, j * sb:(j + 1) * sb, :] = lses[j]


def _band_attention(q, k, v, sink, batch, seq, dil, half, n_kv, group, tq=512):
    wq, wk = q.shape[1] // dil, k.shape[1] // dil
    hb = half
    sl = seq // dil
    tq = min(tq, sl)
    nq = sl // tq
    per = tq // hb
    nhb = sl // hb
    qv = q.reshape(batch, sl, dil * wq)
    kv = k.reshape(batch, sl, dil * wk)
    vv = v.reshape(batch, sl, dil * wk)
    main = lambda w: pl.BlockSpec((1, tq, w), lambda b, r, i: (b, i, r))
    prev = pl.BlockSpec((1, hb, wk), lambda b, r, i: (b, jnp.maximum(i * per - 1, 0), r))
    nxt = pl.BlockSpec((1, hb, wk), lambda b, r, i: (b, jnp.minimum((i + 1) * per, nhb - 1), r))
    with_sink = sink is not None
    in_specs = [main(wq), prev, main(wk), nxt, prev, main(wk), nxt]
    args = [qv, kv, kv, kv, vv, vv, vv]
    if with_sink:
        in_specs.append(pl.BlockSpec((1, wq // HEAD_DIM), lambda b, r, i: (0, 0)))
        args.append(sink)
        out_specs = main(wq)
        out_shape = jax.ShapeDtypeStruct(qv.shape, BF16)
    else:
        out_specs = [main(wq), main(LSE_WIDTH)]
        out_shape = [jax.ShapeDtypeStruct(qv.shape, BF16),
                     jax.ShapeDtypeStruct((batch, sl, dil * LSE_WIDTH), F32)]
    kern = functools.partial(_band_kernel, tq=tq, hb=hb, sb=min(tq, max(hb, ATTN_SUB_ROWS // group)),
                             half=half, length=sl,
                             n_kv=n_kv, group=group, with_sink=with_sink)
    out = pl.pallas_call(
        kern,
        grid=(batch, dil, nq),
        in_specs=in_specs,
        out_specs=out_specs,
        out_shape=out_shape,
        compiler_params=_cparams(("parallel", "parallel", "parallel")),
        name="band_attn_sink" if with_sink else f"band_attn_d{dil}",
    )(*args)
    flat = lambda t: t.reshape(batch * sl, t.shape[-1])
    if with_sink:
        return flat(out)
    return flat(out[0]), flat(out[1])


def _rwkv_kernel(r_ref, k_ref, v_ref, sa_ref, sb_ref, pv_ref, w2_ref, a2_ref, g_ref,
                 y_ref, bonus_ref, state_ref, carry_ref, carryb_ref, *, n, cs, reverse):
    @pl.when(pl.program_id(1) == 0)
    def _():
        state_ref[...] = jnp.zeros_like(state_ref)
        carry_ref[...] = jnp.zeros_like(carry_ref)
        carryb_ref[...] = jnp.zeros_like(carryb_ref)

    last = 0 if reverse else n - 1
    r0, k0, v0, sb = r_ref[...], k_ref[...], v_ref[...], sb_ref[...]
    shift = lambda x, c: _scan_shift(x, c, n, reverse)
    r = r0 + (shift(r0, carry_ref[0:1, :]) - r0) * pv_ref[0:1, :]
    k = k0 + (shift(k0, carry_ref[1:2, :]) - k0) * pv_ref[1:2, :]
    v = v0 + (shift(v0, carry_ref[2:3, :]) - v0) * pv_ref[2:3, :]
    xlr = sa_ref[...] + shift(sb, carryb_ref[0:1, :])
    carry_ref[0:1, :] = r0[last:last + 1, :]
    carry_ref[1:2, :] = k0[last:last + 1, :]
    carry_ref[2:3, :] = v0[last:last + 1, :]
    carryb_ref[0:1, :] = sb[last:last + 1, :]

    wl = pv_ref[3:4, :] + jnp.dot(jnp.tanh(xlr).astype(BF16), w2_ref[...], preferred_element_type=F32)
    w_log = -_softplus(-wl) - 0.5
    ld = -jnp.exp(w_log)
    a = jax.nn.sigmoid(pv_ref[4:5, :] + jnp.dot(xlr.astype(BF16), a2_ref[...], preferred_element_type=F32))
    g = g_ref[...]
    kkr = k * pv_ref[5:6, :]
    kx = k * (1.0 + (a - 1.0) * pv_ref[6:7, :])
    sums = _seg_sum(jnp.concatenate([kkr * kkr, r * kx * pv_ref[7:8, :]], axis=0), g)
    kk = kkr / jnp.maximum(jnp.sqrt(sums[:n]), 1e-12)
    bonus_ref[...] = (sums[n:] * v).astype(bonus_ref.dtype)

    nsub = n // cs
    last_row = lambda gi: gi * cs if reverse else (gi + 1) * cs - 1
    row1 = lax.broadcasted_iota(jnp.int32, (n, 1), 0)
    c = _scan_cumsum(ld, n, reverse, seg=cs)
    c_last = c[last_row(0):last_row(0) + 1, :]
    for gi in range(1, nsub):
        c_last = jnp.where(row1 >= gi * cs, c[last_row(gi):last_row(gi) + 1, :], c_last)
    e_c = jnp.exp(c)
    e_cx = jnp.exp(c - ld)
    e_nc = jnp.exp(-c)
    e_end = jnp.exp(c_last - c)
    kb = kk * a
    rt = r * e_c
    at = -kk * e_cx
    bt = kb * e_nc
    kt = kx * e_nc
    bh = kb * e_end
    kh = kx * e_end

    rt_b, at_b, bt_b, kt_b, v_b = (t.astype(BF16) for t in (rt, at, bt, kt, v))
    bh_b, kh_b = bh.astype(BF16), kh.astype(BF16)

    row = lax.broadcasted_iota(jnp.int32, (2 * cs, 2 * cs), 0)
    col = lax.broadcasted_iota(jnp.int32, (2 * cs, 2 * cs), 1)
    rr, cc = row & (cs - 1), col & (cs - 1)
    strict = (cc > rr) if reverse else (cc < rr)
    mask = strict | ((cc == rr) & (row >= cs))
    eye = (lax.broadcasted_iota(jnp.int32, (cs, cs), 0) == lax.broadcasted_iota(jnp.int32, (cs, cs), 1)).astype(F32)

    heads = range(GROUP_WIDTH // HEAD_DIM)
    nh = len(heads)
    cat = lambda a, b: jnp.concatenate([a, b], axis=0)
    pairs = [(slice(gi * cs, (gi + 1) * cs), slice(h * HEAD_DIM, (h + 1) * HEAD_DIM))
             for gi in range(nsub) for h in heads]
    big = [jnp.where(mask, _dot_nt(cat(at_b[rs, s], rt_b[rs, s]), cat(bt_b[rs, s], kt_b[rs, s])), 0.0)
           for rs, s in pairs]
    akv = [_dot(b[:cs, cs:], v_b[rs, s]) for b, (rs, s) in zip(big, pairs)]
    a_ab = [b[:cs, :cs] for b in big]
    t_inv = [eye + x for x in a_ab]
    pw = [_dot(x, x) for x in a_ab]
    lvl = 4
    while lvl < cs:
        res = [_dot(cat(x, t), x) for x, t in zip(pw, t_inv)]
        t_inv = [t + x[cs:] for t, x in zip(t_inv, res)]
        pw = [x[:cs] for x in res]
        lvl *= 2
    t_inv = [t + _dot(t, x) for t, x in zip(t_inv, pw)]
    ta = [_dot(t, at_b[rs, s]) for t, (rs, s) in zip(t_inv, pairs)]
    tv = [_dot(t, x) for t, x in zip(t_inv, akv)]

    st = [state_ref[h] for h in heads]
    ys = [None] * nsub
    for gi in (range(nsub - 1, -1, -1) if reverse else range(nsub)):
        wc = jnp.exp(c[last_row(gi):last_row(gi) + 1, :])
        sl = slice(gi * nh, (gi + 1) * nh)
        u = [_dot_nt(x, s0) + z for x, s0, z in zip(ta[sl], st, tv[sl])]
        uv = [cat(x, v[rs, s]).astype(BF16) for x, (rs, s) in zip(u, pairs[sl])]
        ys[gi] = [_dot_nt(rt_b[rs, s], s0) + _dot(b[cs:, :], x)
                  for (rs, s), s0, b, x in zip(pairs[sl], st, big[sl], uv)]
        st = [s0 * wc[:, s] + _dot_tn(x, cat(bh_b[rs, s], kh_b[rs, s]))
              for s0, (rs, s), x in zip(st, pairs[sl], uv)]
    for gi in range(nsub):
        for h in heads:
            rs, s = pairs[gi * nh + h]
            y_ref[rs, s] = ys[gi][h]
    for h in heads:
        state_ref[h] = st[h]


def _rwkv(p, pvec, w2pad, a2pad, g512, batch, seq, reverse):
    n = p.shape[0]
    c = RWKV_BLOCK
    nc = seq // c
    if reverse:
        rowidx = lambda b, j: b * nc + (nc - 1 - j)
    else:
        rowidx = lambda b, j: b * nc + j
    wide = lambda cb: pl.BlockSpec((c, 512), lambda b, j, cb=cb: (rowidx(b, j), cb))
    narrow = lambda cb: pl.BlockSpec((c, 128), lambda b, j, cb=cb: (rowidx(b, j), cb))
    const = lambda shape: pl.BlockSpec(shape, lambda b, j: (0,) * len(shape))
    out = pl.BlockSpec((c, 512), lambda b, j: (rowidx(b, j), 0))
    return pl.pallas_call(
        functools.partial(_rwkv_kernel, n=c, cs=RWKV_CHUNK, reverse=reverse),
        grid=(batch, nc),
        in_specs=[wide(CB_BR), wide(CB_BK), wide(CB_BV), narrow(CB_SECA), narrow(CB_SECB),
                  const((8, 512)), const((128, 512)), const((128, 512)), const((512, 512))],
        out_specs=[out, out],
        out_shape=[jax.ShapeDtypeStruct((n, 512), F32), jax.ShapeDtypeStruct((n, 512), BF16)],
        scratch_shapes=[pltpu.VMEM((8, HEAD_DIM, HEAD_DIM), F32), pltpu.VMEM((8, 512), F32),
                        pltpu.VMEM((8, 128), F32)],
        compiler_params=_cparams(("parallel", "arbitrary")),
        name="rwkv_bwd" if reverse else "rwkv_fwd",
    )(p, p, p, p, p, pvec, w2pad, a2pad, g512)


def _hgrn_kernel(q_ref, v_ref, z_ref, lb_ref, o_ref, state_ref, *, n, cs, sub, reverse):
    @pl.when(pl.program_id(1) == 0)
    def _():
        state_ref[...] = jnp.zeros_like(state_ref)

    q, v, z = q_ref[...], v_ref[...], z_ref[...]
    log_sig = -_softplus(-z)
    x1 = lb_ref[0:1, :]
    x2 = lb_ref[1:2, :] + log_sig
    logf = jnp.maximum(x1, x2) + jnp.log(1.0 + jnp.exp(-jnp.abs(x1 - x2)))
    kf = 1.0 - jnp.exp(logf)
    nchunk, nsub = n // cs, cs // sub
    last_row = lambda gi: gi * cs if reverse else (gi + 1) * cs - 1
    row1 = lax.broadcasted_iota(jnp.int32, (n, 1), 0)
    b = _scan_cumsum(logf, n, reverse, seg=cs)
    b_last = b[last_row(0):last_row(0) + 1, :]
    for gi in range(1, nchunk):
        b_last = jnp.where(row1 >= gi * cs, b[last_row(gi):last_row(gi) + 1, :], b_last)
    qe = q * jnp.exp(b)
    ke = kf * jnp.exp(b_last - b)
    b2 = b * LOG2_E
    c2 = b2 - jnp.log(kf) * LOG2_E
    srow = lax.broadcasted_iota(jnp.int32, (sub, sub), 0)
    lane = lax.broadcasted_iota(jnp.int32, (sub, sub), 1)
    tri = (srow <= lane) if reverse else (srow >= lane)
    hsl = [slice(h * D_HEAD_K, (h + 1) * D_HEAD_K) for h in range(D_HEADS)]

    intra = {}
    for gi in range(nchunk):
        base = gi * cs
        for h, hs in enumerate(hsl):
            for m in range(nsub):
                ms = slice(base + m * sub, base + (m + 1) * sub)
                q_m, b_m = q[ms, hs], b[ms, hs]
                b2_m, c2_m = b2[ms, hs], c2[ms, hs]
                dmat = jnp.zeros((sub, sub), F32)
                for s in range(sub):
                    w = q_m * jnp.exp2(b2_m - c2_m[s:s + 1])
                    dmat = jnp.where(lane == s, jnp.sum(w, axis=-1, keepdims=True), dmat)
                dmat = jnp.where(tri, dmat, 0.0)
                o_m = _dot(dmat, v[ms, hs])
                es = slice(base + (m + 1) * sub, base + cs) if reverse else slice(base, base + m * sub)
                if es.stop > es.start:
                    first = base + ((m + 1) * sub - 1 if reverse else m * sub)
                    rho = b[first:first + 1, hs] - logf[first:first + 1, hs]
                    qt = q_m * jnp.exp(b_m - rho)
                    kt = kf[es, hs] * jnp.exp(rho - b[es, hs])
                    o_m = o_m + _dot(_dot_nt(qt, kt), v[es, hs])
                intra[gi, h, m] = o_m

    st = [state_ref[h] for h in range(D_HEADS)]
    inter = {}
    for gi in (range(nchunk - 1, -1, -1) if reverse else range(nchunk)):
        rs = slice(gi * cs, (gi + 1) * cs)
        wend = jnp.exp(b[last_row(gi):last_row(gi) + 1, :])
        for h, hs in enumerate(hsl):
            inter[gi, h] = _dot_nt(qe[rs, hs], st[h])
        st = [st[h] * wend[:, hs] + _dot_tn(v[rs, hs], ke[rs, hs]) for h, hs in enumerate(hsl)]
    for (gi, h, m), o_m in intra.items():
        lo = gi * cs + m * sub
        o_ref[lo:lo + sub, hsl[h]] = (o_m + inter[gi, h][m * sub:(m + 1) * sub]).astype(o_ref.dtype)
    for h in range(D_HEADS):
        state_ref[h] = st[h]


def _hgrn(p, lbvec, batch, seq, reverse):
    n = p.shape[0]
    c = HGRN_BLOCK
    nc = seq // c
    if reverse:
        rowidx = lambda b, j: b * nc + (nc - 1 - j)
    else:
        rowidx = lambda b, j: b * nc + j
    wide = lambda cb: pl.BlockSpec((c, 512), lambda b, j, cb=cb: (rowidx(b, j), cb))
    return pl.pallas_call(
        functools.partial(_hgrn_kernel, n=c, cs=HGRN_CHUNK, sub=HGRN_SUB, reverse=reverse),
        grid=(batch, nc),
        in_specs=[wide(CB_DQ), wide(CB_DI), wide(CB_DFB if reverse else CB_DFF),
                  pl.BlockSpec((8, 512), lambda b, j: (0, 0))],
        out_specs=pl.BlockSpec((c, 512), lambda b, j: (rowidx(b, j), 0)),
        out_shape=jax.ShapeDtypeStruct((n, 512), BF16),
        scratch_shapes=[pltpu.VMEM((D_HEADS, D_HEAD_K, D_HEAD_K), F32)],
        compiler_params=_cparams(("parallel", "arbitrary")),
        name="hgrn_bwd" if reverse else "hgrn_fwd",
    )(p, p, p, lbvec)


def _mixpost_kernel(o1_ref, l1_ref, o2_ref, l2_ref, o3_ref, l3_ref,
                    y0_ref, y1_ref, bo0_ref, bo1_ref, g1_ref, oc_ref,
                    h0_ref, h1_ref, dg_ref, pv_ref, g_ref, g2_ref, mix_ref,
                    o2s_ref, l2s_ref, o3s_ref, l3s_ref, *, tm):
    for d, src, dst in ((4, o2_ref, o2s_ref), (16, o3_ref, o3s_ref)):
        for r in range(d):
            for c in range(4):
                lo = r * 512 + c * 128
                dst[c, pl.ds(r, tm // d, stride=d), :] = src[:, lo:lo + 128].astype(F32)
    for d, src, dst in ((4, l2_ref, l2s_ref), (16, l3_ref, l3s_ref)):
        for r in range(d):
            dst[0, pl.ds(r, tm // d, stride=d), :] = src[:, r * LSE_WIDTH:(r + 1) * LSE_WIDTH]
    wide = lambda ref: jnp.concatenate([ref[c] for c in range(4)], axis=1)
    l1, l2, l3 = l1_ref[...], l2s_ref[0], l3s_ref[0]
    mx = jnp.maximum(jnp.maximum(l1, l2), l3)
    w1, w2, w3 = jnp.exp(l1 - mx), jnp.exp(l2 - mx), jnp.exp(l3 - mx)
    inv = 1.0 / (w1 + w2 + w3)
    src_lane = lax.broadcasted_iota(jnp.int32, (LSE_WIDTH, GROUP_WIDTH), 0)
    dst_head = lax.broadcasted_iota(jnp.int32, (LSE_WIDTH, GROUP_WIDTH), 1) // HEAD_DIM
    spread = (src_lane == dst_head * (LSE_WIDTH * HEAD_DIM // GROUP_WIDTH)).astype(BF16)
    e1, e2, e3 = (_seg_sum(w * inv, spread) for w in (w1, w2, w3))
    out_a = o1_ref[...] * e1 + wide(o2s_ref) * e2 + wide(o3s_ref) * e3
    mix_ref[:, 0:512] = out_a.astype(BF16)
    g = g_ref[...]
    y = y0_ref[...] + y1_ref[...]
    mu = _seg_sum(y, g) * (1.0 / HEAD_DIM)
    yc = y - mu
    var = _seg_sum(yc * yc, g) * (1.0 / HEAD_DIM)
    yn = yc * lax.rsqrt(var + LNX_EPS) * pv_ref[0:1, :] + pv_ref[1:2, :]
    gate = jnp.dot(jax.nn.sigmoid(g1_ref[...]).astype(BF16), g2_ref[...], preferred_element_type=F32)
    mix_ref[:, 512:1024] = ((yn + bo0_ref[...] + bo1_ref[...]) * gate).astype(BF16)
    mix_ref[:, 1024:1536] = oc_ref[...]
    o = h0_ref[...].astype(F32) + h1_ref[...].astype(F32)
    dg = dg_ref[...]
    silu = dg * jax.nn.sigmoid(dg)
    for h in range(D_HEADS):
        hs = slice(h * D_HEAD_K, (h + 1) * D_HEAD_K)
        oh = o[:, hs]
        ms = jnp.mean(oh * oh, axis=-1, keepdims=True)
        res = oh * lax.rsqrt(ms + NORM_EPS) * pv_ref[2:3, hs] * silu[:, hs]
        mix_ref[:, 1536 + h * D_HEAD_K:1536 + (h + 1) * D_HEAD_K] = res.astype(BF16)


def _mixpost(a_parts, yb, bonus, p, oc, od, pvec, g512, g2pad, tm=256):
    n = p.shape[0]
    blk = pl.BlockSpec((tm, 512), lambda i: (i, 0))
    const = lambda shape: pl.BlockSpec(shape, lambda i: (0,) * len(shape))
    args = []
    for o, l in a_parts:
        args += [o, l]
    args += [yb[0], yb[1], bonus[0], bonus[1], p, oc, od[0], od[1], p, pvec, g512, g2pad]
    dil = lambda d, w=512: pl.BlockSpec((tm // d, d * w), lambda i: (i, 0))
    in_specs = [blk, dil(1, LSE_WIDTH), dil(4), dil(4, LSE_WIDTH), dil(16), dil(16, LSE_WIDTH)] + [blk] * 4 + [
        pl.BlockSpec((tm, 128), lambda i: (i, CB_G1)), blk, blk, blk,
        pl.BlockSpec((tm, 512), lambda i: (i, CB_DG)),
        const((8, 512)), const((512, 512)), const((128, 512))]
    return pl.pallas_call(
        functools.partial(_mixpost_kernel, tm=tm),
        grid=(n // tm,),
        in_specs=in_specs,
        out_specs=pl.BlockSpec((tm, D_MODEL), lambda i: (i, 0)),
        out_shape=jax.ShapeDtypeStruct((n, D_MODEL), BF16),
        scratch_shapes=[pltpu.VMEM((4, tm, 128), F32), pltpu.VMEM((1, tm, LSE_WIDTH), F32)] * 2,
        compiler_params=_cparams(("parallel",)),
        name="mixpost",
    )(*args)


def _outproj_kernel(mix_ref, w_ref, x_ref, o_ref):
    o_ref[...] = x_ref[...] + jnp.dot(mix_ref[...], w_ref[...], preferred_element_type=F32)


def _outproj(mix, w, x2, tm=512, tn=2048):
    n, d = x2.shape
    return pl.pallas_call(
        _outproj_kernel,
        grid=(n // tm, d // tn),
        in_specs=[pl.BlockSpec((tm, mix.shape[1]), lambda i, j: (i, 0)),
                  pl.BlockSpec((mix.shape[1], tn), lambda i, j: (0, j)),
                  pl.BlockSpec((tm, tn), lambda i, j: (i, j))],
        out_specs=pl.BlockSpec((tm, tn), lambda i, j: (i, j)),
        out_shape=jax.ShapeDtypeStruct((n, d), F32),
        compiler_params=_cparams(("parallel", "arbitrary")),
        name="outproj",
    )(mix, w, x2)


def _rope_tables(seq):
    half = ROT_DIM // 2
    inv = ROPE_THETA ** (-jnp.arange(0, ROT_DIM, 2, dtype=F32) / ROT_DIM)
    ang = jnp.arange(seq, dtype=F32)[:, None] * inv[None]
    cos, sin = jnp.cos(ang), jnp.sin(ang)
    ones = jnp.ones((seq, HEAD_DIM - ROT_DIM), F32)
    zeros = jnp.zeros((seq, HEAD_DIM - ROT_DIM), F32)
    z8 = jnp.zeros((seq, half), F32)
    c = jnp.concatenate([cos, cos, ones], axis=1)
    s1 = jnp.concatenate([-sin, z8, zeros], axis=1)
    s2 = jnp.concatenate([z8, sin, zeros], axis=1)
    tile2 = lambda t: jnp.concatenate([t, t], axis=1)
    return tile2(c), tile2(s1), tile2(s2)


def _block_diag_ones(width, seg):
    idx = jnp.arange(width) // seg
    return (idx[:, None] == idx[None, :]).astype(BF16)


def _pad_rows(rows, width=512, total=8):
    rows = [jnp.pad(r.astype(F32).reshape(-1), (0, width - r.size)) for r in rows]
    rows += [jnp.zeros((width,), F32)] * (total - len(rows))
    return jnp.stack(rows)


def _w_ext(w_in, mu_wa, w1, a1, g1):
    cols_a, cols_b = [], []
    for d in range(2):
        for mu, w in ((mu_wa[d, 0], w1[d]), (mu_wa[d, 1], a1[d])):
            cols_a.append((1.0 - mu)[:, None] * w)
            cols_b.append(mu[:, None] * w)
    parts = [w_in[:, 0:3584], w_in[:, 3840:6400], w_in[:, 3584:3840]] + cols_a + cols_b
    parts.append(jnp.pad(g1, ((0, 0), (0, 128 - g1.shape[1]))))
    w = jnp.concatenate(parts, axis=1)
    return jnp.pad(w, ((0, 0), (0, D_IN_EXT - w.shape[1]))).astype(BF16)


def _pad_lowrank(w, row0):
    return jnp.pad(w, ((row0, 128 - row0 - w.shape[0]), (0, 0))).astype(BF16)


def kernel(x, ln_gain, w_in, w_out, w_ffn_in, w_ffn_out, qk_gain, sink, b_mu_rkv, b_mu_wa, b_w0, b_w1,
           b_w2, b_a0, b_a1, b_a2, b_k_k, b_k_a, b_r_k, b_lnx_gain, b_lnx_bias, b_g1, b_g2,
           d_lb_logits, d_norm_gain):
    batch, seq, d = x.shape
    x2 = x.reshape(batch * seq, d)
    rope_c, rope_s1, rope_s2 = _rope_tables(seq)
    g512 = _block_diag_ones(512, HEAD_DIM)
    pr = jax.nn.softmax(d_lb_logits.astype(F32), axis=0)
    cs = jnp.cumsum(pr, axis=0)
    lb_all = cs - cs[0:1]

    ffn_in, ffn_out = w_ffn_in.astype(BF16), w_ffn_out.astype(BF16)

    for l in range(DEPTH):
        x2 = _ffn(x2, ln_gain[l, 0][None], ffn_in, ffn_out, l, 0)

        p = _inproj(x2, ln_gain[l, 1][None], _w_ext(w_in[l], b_mu_wa[l], b_w1[l], b_a1[l], b_g1[l]))

        tile8 = lambda t: jnp.tile(t, 8)
        gains = _pad_rows([tile8(qk_gain[l, 0, 0]), tile8(qk_gain[l, 0, 1]),
                           tile8(qk_gain[l, 1, 0]), jnp.tile(qk_gain[l, 1, 1], 2)])
        prepped = _prep(p, rope_c, rope_s1, rope_s2, gains, g512, seq)
        qc, kc, vc = prepped[9:]

        a_parts = []
        for pi, (window, dil) in enumerate(DILATED_PATTERNS):
            qa, ka, va = prepped[3 * pi:3 * pi + 3]
            a_parts.append(_band_attention(qa, ka, va, None, batch, seq, dil, window // (2 * dil),
                                           GROUP_WIDTH // HEAD_DIM, 1))
        oc = _band_attention(qc, kc, vc, sink[l][None].astype(F32), batch, seq, 1, C_HALF_WINDOW,
                             C_KV_HEADS, C_Q_HEADS // C_KV_HEADS)

        yb, bonus = [], []
        for dr in range(2):
            pvec = _pad_rows([b_mu_rkv[l, dr, 0], b_mu_rkv[l, dr, 1], b_mu_rkv[l, dr, 2], b_w0[l, dr],
                              b_a0[l, dr], b_k_k[l], b_k_a[l], b_r_k[l]])
            y, bo = _rwkv(p, pvec, _pad_lowrank(b_w2[l, dr], 64 * dr), _pad_lowrank(b_a2[l, dr], 64 * dr + 32),
                          g512, batch, seq, reverse=(dr == 1))
            yb.append(y)
            bonus.append(bo)

        od = []
        for dr in range(2):
            lb = lb_all[l, dr]
            od.append(_hgrn(p, _pad_rows([jnp.log(lb), jnp.log1p(-lb)]), batch, seq, reverse=(dr == 1)))

        pvec = _pad_rows([b_lnx_gain[l], b_lnx_bias[l], d_norm_gain[l]])
        mix = _mixpost(a_parts, yb, bonus, p, oc, od, pvec, g512, _pad_lowrank(b_g2[l], 0))
        x2 = _outproj(mix, w_out[l].astype(BF16), x2)

        x2 = _ffn(x2, ln_gain[l, 2][None], ffn_in, ffn_out, l, 1)
    return x2.reshape(batch, seq, d)
```

```python
import functools

import jax
import jax.numpy as jnp
from jax import lax
from jax.experimental import pallas as pl
from jax.experimental.pallas import tpu as pltpu

F32 = jnp.float32
BF16 = jnp.bfloat16

D_MODEL = 2048
DEPTH = 4
HEAD_DIM = 64
GROUP_WIDTH = 512
DILATED_PATTERNS = ((128, 1), (512, 4), (2048, 16))
LNX_EPS = 64e-5
C_Q_HEADS = 8
C_KV_HEADS = 2
C_HALF_WINDOW = 128
D_HEAD_K = 128
D_HEADS = 4
D_FF = 5632
ROPE_THETA = 500000.0
ROT_DIM = 16
NORM_EPS = 1e-6
NEG_INF = -1e30
LOG2_E = 1.4426950408889634

D_IN_EXT = 6912
(CB_AQ, CB_AK, CB_AV, CB_BR, CB_BK, CB_BV, CB_CQ, CB_DQ, CB_DI, CB_DG, CB_DFF, CB_DFB) = range(12)
CB_CK, CB_CV, CB_SECA, CB_SECB, CB_G1 = 48, 49, 50, 51, 52

VMEM_LIMIT = 56 * 1024 * 1024
RWKV_CHUNK = 128
RWKV_BLOCK = 256
HGRN_CHUNK = 64
LSE_WIDTH = 128
ATTN_SUB_ROWS = 256
HGRN_BLOCK = 256
HGRN_SUB = 16


def _cparams(sem):
    return pltpu.CompilerParams(dimension_semantics=sem, vmem_limit_bytes=VMEM_LIMIT)


def _seg_sum(x, g):
    hi = x.astype(BF16)
    lo = (x - hi.astype(F32)).astype(BF16)
    d = lambda a: jnp.dot(a, g, preferred_element_type=F32)
    return d(hi) + d(lo)


def _dot(a, b):
    return jnp.dot(a.astype(BF16), b.astype(BF16), preferred_element_type=F32)


def _dot_nt(a, b):
    return lax.dot_general(a.astype(BF16), b.astype(BF16), (((1,), (1,)), ((), ())),
                           preferred_element_type=F32)


def _dot_tn(a, b):
    return lax.dot_general(a.astype(BF16), b.astype(BF16), (((0,), (0,)), ((), ())),
                           preferred_element_type=F32)


def _softplus(u):
    return jnp.maximum(u, 0.0) + jnp.log(1.0 + jnp.exp(-jnp.abs(u)))


def _scan_cumsum(x, n, reverse, seg=None):
    seg = n if seg is None else seg
    row = lax.broadcasted_iota(jnp.int32, x.shape, 0) & (seg - 1)
    s = 1
    while s < seg:
        if reverse:
            x = x + jnp.where(row < seg - s, pltpu.roll(x, n - s, 0), 0.0)
        else:
            x = x + jnp.where(row >= s, pltpu.roll(x, s, 0), 0.0)
        s *= 2
    return x


def _scan_shift(x, carry_row, n, reverse):
    row = lax.broadcasted_iota(jnp.int32, x.shape, 0)
    if reverse:
        return jnp.where(row == n - 1, carry_row, pltpu.roll(x, n - 1, 0))
    return jnp.where(row == 0, carry_row, pltpu.roll(x, 1, 0))


def _ffn_kernel(x_ref, g_ref, wg_ref, wu_ref, wo_ref, o_ref, hn_ref):
    @pl.when(pl.program_id(1) == 0)
    def _():
        x = x_ref[...]
        ms = jnp.mean(x * x, axis=-1, keepdims=True)
        hn_ref[...] = (x * lax.rsqrt(ms + NORM_EPS) * g_ref[...]).astype(BF16)
        o_ref[...] = x

    h = hn_ref[...]
    gate = jnp.dot(h, wg_ref[...], preferred_element_type=F32)
    up = jnp.dot(h, wu_ref[...], preferred_element_type=F32)
    act = (0.5 * gate * jax.nn.sigmoid(gate) * up).astype(BF16)
    o_ref[...] += jnp.dot(act, wo_ref[...], preferred_element_type=F32)


def _ffn(x2, gain, w_in, w_out, layer, which, tm=1024, tf=512):
    n, d = x2.shape
    nf = D_FF // tf
    return pl.pallas_call(
        _ffn_kernel,
        grid=(n // tm, nf),
        in_specs=[
            pl.BlockSpec((tm, d), lambda i, j: (i, 0)),
            pl.BlockSpec((1, d), lambda i, j: (0, 0)),
            pl.BlockSpec((None, None, d, tf), lambda i, j: (layer, which, 0, j)),
            pl.BlockSpec((None, None, d, tf), lambda i, j: (layer, which, 0, j + nf)),
            pl.BlockSpec((None, None, tf, d), lambda i, j: (layer, which, j, 0)),
        ],
        out_specs=pl.BlockSpec((tm, d), lambda i, j: (i, 0)),
        out_shape=jax.ShapeDtypeStruct((n, d), F32),
        scratch_shapes=[pltpu.VMEM((tm, d), BF16)],
        compiler_params=_cparams(("parallel", "arbitrary")),
        name="ffn",
    )(x2, gain, w_in, w_in, w_out)


def _inproj_kernel(x_ref, g_ref, w_ref, o_ref, hn_ref):
    @pl.when(pl.program_id(1) == 0)
    def _():
        x = x_ref[...]
        ms = jnp.mean(x * x, axis=-1, keepdims=True)
        hn_ref[...] = (x * lax.rsqrt(ms + NORM_EPS) * g_ref[...]).astype(BF16)

    o_ref[...] = jnp.dot(hn_ref[...], w_ref[...], preferred_element_type=F32).astype(o_ref.dtype)


def _inproj(x2, gain, w_ext, tm=1024, tn=2304):
    n, d = x2.shape
    nc = w_ext.shape[1]
    return pl.pallas_call(
        _inproj_kernel,
        grid=(n // tm, nc // tn),
        in_specs=[
            pl.BlockSpec((tm, d), lambda i, j: (i, 0)),
            pl.BlockSpec((1, d), lambda i, j: (0, 0)),
            pl.BlockSpec((d, tn), lambda i, j: (0, j)),
        ],
        out_specs=pl.BlockSpec((tm, tn), lambda i, j: (i, j)),
        out_shape=jax.ShapeDtypeStruct((n, nc), BF16),
        scratch_shapes=[pltpu.VMEM((tm, d), BF16)],
        compiler_params=_cparams(("parallel", "arbitrary")),
        name="inproj",
    )(x2, gain, w_ext)


def _norm_rope(t, gain, g, c, s1, s2, scale):
    w = t.shape[-1]
    ss = _seg_sum(t * t, g)
    y = t * lax.rsqrt(ss * (1.0 / HEAD_DIM) + NORM_EPS) * gain
    half = ROT_DIM // 2
    out = y * c + pltpu.roll(y, w - half, 1) * s1 + pltpu.roll(y, half, 1) * s2
    return out * scale if scale != 1.0 else out


def _prep_kernel(aq_ref, ak_ref, av_ref, cq_ref, ck_ref, cv_ref, c_ref, s1_ref, s2_ref,
                 gain_ref, g_ref,
                 qa_ref, ka_ref, va_ref, qa4_ref, ka4_ref, va4_ref, qa16_ref, ka16_ref, va16_ref,
                 qc_ref, kc_ref, vc_ref, scr_ref, *, tm):
    c1, s11, s21 = c_ref[...], s1_ref[...], s2_ref[...]
    c4 = jnp.concatenate([c1] * 4, axis=1)
    s14 = jnp.concatenate([s11] * 4, axis=1)
    s24 = jnp.concatenate([s21] * 4, axis=1)
    g = g_ref[...]
    g1 = g_ref[0:128, 0:128]
    scale = HEAD_DIM ** -0.5

    def emit(y, nat_ref, dil_refs):
        nat_ref[...] = y.astype(BF16)
        for c in range(4):
            scr_ref[c] = y[:, c * 128:(c + 1) * 128]
        for d, ref in dil_refs:
            for r in range(d):
                for c in range(4):
                    lo = r * 512 + c * 128
                    ref[:, lo:lo + 128] = scr_ref[c, pl.ds(r, tm // d, stride=d), :].astype(BF16)

    emit(_norm_rope(aq_ref[...].astype(F32), gain_ref[0:1, :], g, c4, s14, s24, scale), qa_ref,
         ((4, qa4_ref), (16, qa16_ref)))
    emit(_norm_rope(ak_ref[...].astype(F32), gain_ref[1:2, :], g, c4, s14, s24, 1.0), ka_ref,
         ((4, ka4_ref), (16, ka16_ref)))
    emit(av_ref[...].astype(F32), va_ref, ((4, va4_ref), (16, va16_ref)))
    qc_ref[...] = _norm_rope(cq_ref[...].astype(F32), gain_ref[2:3, :], g, c4, s14, s24, scale).astype(BF16)
    kc_ref[...] = _norm_rope(ck_ref[...].astype(F32), gain_ref[3:4, 0:128], g1, c1, s11, s21, 1.0).astype(BF16)
    vc_ref[...] = cv_ref[...].astype(BF16)


def _prep(p, rope_c, rope_s1, rope_s2, gains, g512, seq, tm=512):
    n = p.shape[0]
    nseq = seq // tm
    wide = lambda cb: pl.BlockSpec((tm, 512), lambda i, cb=cb: (i, cb))
    narrow = lambda cb: pl.BlockSpec((tm, 128), lambda i, cb=cb: (i, cb))
    tab = pl.BlockSpec((tm, 128), lambda i: (i % nseq, 0))
    rowblk = lambda rows, width: pl.BlockSpec((rows, width), lambda i: (i, 0))
    shape = lambda rows, width: jax.ShapeDtypeStruct((rows, width), BF16)
    trio = lambda d: [rowblk(tm // d, d * 512)] * 3
    trio_shape = lambda d: [shape(n // d, d * 512)] * 3
    return pl.pallas_call(
        functools.partial(_prep_kernel, tm=tm),
        grid=(n // tm,),
        in_specs=[wide(CB_AQ), wide(CB_AK), wide(CB_AV), wide(CB_CQ), narrow(CB_CK), narrow(CB_CV),
                  tab, tab, tab,
                  pl.BlockSpec((8, 512), lambda i: (0, 0)),
                  pl.BlockSpec((512, 512), lambda i: (0, 0))],
        out_specs=trio(1) + trio(4) + trio(16) + [rowblk(tm, 512), rowblk(tm, 128), rowblk(tm, 128)],
        out_shape=trio_shape(1) + trio_shape(4) + trio_shape(16) + [shape(n, 512), shape(n, 128), shape(n, 128)],
        scratch_shapes=[pltpu.VMEM((4, tm, 128), F32)],
        compiler_params=_cparams(("parallel",)),
        name="prep_qk",
    )(p, p, p, p, p, p, rope_c, rope_s1, rope_s2, gains, g512)


def _band_kernel(*refs, tq, hb, sb, half, length, n_kv, group, with_sink):
    if with_sink:
        q_ref, kp_ref, km_ref, kn_ref, vp_ref, vm_ref, vn_ref, sink_ref, o_ref = refs
    else:
        q_ref, kp_ref, km_ref, kn_ref, vp_ref, vm_ref, vn_ref, o_ref, lse_ref = refs
    i = pl.program_id(2)
    nsub = tq // sb
    gs, nwin = group * sb, sb + 2 * hb
    row = lax.broadcasted_iota(jnp.int32, (gs, nwin), 0) & (sb - 1)
    col = lax.broadcasted_iota(jnp.int32, (gs, nwin), 1)
    band = jnp.abs(row + hb - col) <= half
    valid = []
    for j in range(nsub):
        kpos = i * tq + j * sb - hb + col
        valid.append(band & (kpos >= 0) & (kpos < length))
    q_all = q_ref[0]
    n_heads = n_kv * group
    hsl = [slice(h * HEAD_DIM, (h + 1) * HEAD_DIM) for h in range(n_heads)]
    k_heads = [jnp.concatenate([kp_ref[0, :, s], km_ref[0, :, s], kn_ref[0, :, s]], axis=0) for s in hsl[:n_kv]]
    v_heads = [jnp.concatenate([vp_ref[0, :, s], vm_ref[0, :, s], vn_ref[0, :, s]], axis=0) for s in hsl[:n_kv]]
    kvs = range(n_kv)
    if with_sink:
        sinks = sink_ref[...]
        rowg = lax.broadcasted_iota(jnp.int32, (gs, 1), 0) // sb
        sink_col = []
        for kv in kvs:
            col_kv = jnp.zeros((gs, 1), F32)
            for g in range(group):
                h = kv * group + g
                col_kv = jnp.where(rowg == g, sinks[0:1, h:h + 1], col_kv)
            sink_col.append(col_kv)
    lse_lane_head = lax.broadcasted_iota(jnp.int32, (sb, LSE_WIDTH), 1) // (LSE_WIDTH // n_heads)
    outs, lses = [], []
    for j in range(nsub):
        rows, win = slice(j * sb, (j + 1) * sb), slice(j * sb, (j + 1) * sb + 2 * hb)
        qs = [jnp.concatenate([q_all[rows, hsl[kv * group + g]] for g in range(group)], axis=0) for kv in kvs]
        s = [lax.dot_general(qs[kv], k_heads[kv][win], (((1,), (1,)), ((), ())),
                             preferred_element_type=F32) for kv in kvs]
        s = [jnp.where(valid[j], x, NEG_INF) for x in s]
        m = [jnp.max(x, axis=-1, keepdims=True) for x in s]
        p = [jnp.exp(x - y) for x, y in zip(s, m)]
        l = [jnp.sum(x, axis=-1, keepdims=True) for x in p]
        acc = [jnp.dot(p[kv].astype(BF16), v_heads[kv][win], preferred_element_type=F32) for kv in kvs]
        if with_sink:
            m_all = [jnp.maximum(x, y) for x, y in zip(m, sink_col)]
            sc = [jnp.exp(x - y) for x, y in zip(m, m_all)]
            den = [l[kv] * sc[kv] + jnp.exp(sink_col[kv] - m_all[kv]) for kv in kvs]
            o = [acc[kv] * (sc[kv] / den[kv]) for kv in kvs]
        else:
            o = [x / y for x, y in zip(acc, l)]
            lse = [x + jnp.log(y) for x, y in zip(m, l)]
            tile = jnp.zeros((sb, LSE_WIDTH), F32)
            for h in range(n_heads):
                tile = jnp.where(lse_lane_head == h, lse[h // group][(h % group) * sb:(h % group + 1) * sb], tile)
            lses.append(tile)
        outs.append(jnp.concatenate([o[h // group][(h % group) * sb:(h % group + 1) * sb]
                                     for h in range(n_heads)], axis=1))
    for j in range(nsub):
        o_ref[0, j * sb:(j + 1) * sb, :] = outs[j].astype(o_ref.dtype)
        if not with_sink:
            lse_ref[0, j * sb:(j + 1) * sb, :] = lses[j]


def _band_attention(q, k, v, sink, batch, seq, dil, half, n_kv, group, tq=512):
    wq, wk = q.shape[1] // dil, k.shape[1] // dil
    hb = half
    sl = seq // dil
    tq = min(tq, sl)
    nq = sl // tq
    per = tq // hb
    nhb = sl // hb
    qv = q.reshape(batch, sl, dil * wq)
    kv = k.reshape(batch, sl, dil * wk)
    vv = v.reshape(batch, sl, dil * wk)
    main = lambda w: pl.BlockSpec((1, tq, w), lambda b, r, i: (b, i, r))
    prev = pl.BlockSpec((1, hb, wk), lambda b, r, i: (b, jnp.maximum(i * per - 1, 0), r))
    nxt = pl.BlockSpec((1, hb, wk), lambda b, r, i: (b, jnp.minimum((i + 1) * per, nhb - 1), r))
    with_sink = sink is not None
    in_specs = [main(wq), prev, main(wk), nxt, prev, main(wk), nxt]
    args = [qv, kv, kv, kv, vv, vv, vv]
    if with_sink:
        in_specs.append(pl.BlockSpec((1, wq // HEAD_DIM), lambda b, r, i: (0, 0)))
        args.append(sink)
        out_specs = main(wq)
        out_shape = jax.ShapeDtypeStruct(qv.shape, BF16)
    else:
        out_specs = [main(wq), main(LSE_WIDTH)]
        out_shape = [jax.ShapeDtypeStruct(qv.shape, BF16),
                     jax.ShapeDtypeStruct((batch, sl, dil * LSE_WIDTH), F32)]
    kern = functools.partial(_band_kernel, tq=tq, hb=hb, sb=min(tq, max(hb, ATTN_SUB_ROWS // group)),
                             half=half, length=sl,
                             n_kv=n_kv, group=group, with_sink=with_sink)
    out = pl.pallas_call(
        kern,
        grid=(batch, dil, nq),
        in_specs=in_specs,
        out_specs=out_specs,
        out_shape=out_shape,
        compiler_params=_cparams(("parallel", "parallel", "parallel")),
        name="band_attn_sink" if with_sink else f"band_attn_d{dil}",
    )(*args)
    flat = lambda t: t.reshape(batch * sl, t.shape[-1])
    if with_sink:
        return flat(out)
    return flat(out[0]), flat(out[1])


def _rwkv_kernel(r_ref, k_ref, v_ref, sa_ref, sb_ref, pv_ref, w2_ref, a2_ref, g_ref,
                 y_ref, bonus_ref, state_ref, carry_ref, carryb_ref, *, n, cs, reverse):
    @pl.when(pl.program_id(1) == 0)
    def _():
        state_ref[...] = jnp.zeros_like(state_ref)
        carry_ref[...] = jnp.zeros_like(carry_ref)
        carryb_ref[...] = jnp.zeros_like(carryb_ref)

    last = 0 if reverse else n - 1
    r0, k0, v0, sb = (ref[...].astype(F32) for ref in (r_ref, k_ref, v_ref, sb_ref))
    shift = lambda x, c: _scan_shift(x, c, n, reverse)
    r = r0 + (shift(r0, carry_ref[0:1, :]) - r0) * pv_ref[0:1, :]
    k = k0 + (shift(k0, carry_ref[1:2, :]) - k0) * pv_ref[1:2, :]
    v = v0 + (shift(v0, carry_ref[2:3, :]) - v0) * pv_ref[2:3, :]
    xlr = sa_ref[...].astype(F32) + shift(sb, carryb_ref[0:1, :])
    carry_ref[0:1, :] = r0[last:last + 1, :]
    carry_ref[1:2, :] = k0[last:last + 1, :]
    carry_ref[2:3, :] = v0[last:last + 1, :]
    carryb_ref[0:1, :] = sb[last:last + 1, :]

    wl = pv_ref[3:4, :] + jnp.dot(jnp.tanh(xlr).astype(BF16), w2_ref[...], preferred_element_type=F32)
    w_log = -_softplus(-wl) - 0.5
    ld = -jnp.exp(w_log)
    a = jax.nn.sigmoid(pv_ref[4:5, :] + jnp.dot(xlr.astype(BF16), a2_ref[...], preferred_element_type=F32))
    g = g_ref[...]
    kkr = k * pv_ref[5:6, :]
    kx = k * (1.0 + (a - 1.0) * pv_ref[6:7, :])
    sums = _seg_sum(jnp.concatenate([kkr * kkr, r * kx * pv_ref[7:8, :]], axis=0), g)
    kk = kkr / jnp.maximum(jnp.sqrt(sums[:n]), 1e-12)
    bonus_ref[...] = (sums[n:] * v).astype(bonus_ref.dtype)

    nsub = n // cs
    last_row = lambda gi: gi * cs if reverse else (gi + 1) * cs - 1
    row1 = lax.broadcasted_iota(jnp.int32, (n, 1), 0)
    c = _scan_cumsum(ld, n, reverse, seg=cs)
    c_last = c[last_row(0):last_row(0) + 1, :]
    for gi in range(1, nsub):
        c_last = jnp.where(row1 >= gi * cs, c[last_row(gi):last_row(gi) + 1, :], c_last)
    e_c = jnp.exp(c)
    e_cx = jnp.exp(c - ld)
    e_nc = jnp.exp(-c)
    e_end = jnp.exp(c_last - c)
    kb = kk * a
    rt = r * e_c
    at = -kk * e_cx
    bt = kb * e_nc
    kt = kx * e_nc
    bh = kb * e_end
    kh = kx * e_end

    rt_b, at_b, bt_b, kt_b, v_b = (t.astype(BF16) for t in (rt, at, bt, kt, v))
    bh_b, kh_b = bh.astype(BF16), kh.astype(BF16)

    row = lax.broadcasted_iota(jnp.int32, (2 * cs, 2 * cs), 0)
    col = lax.broadcasted_iota(jnp.int32, (2 * cs, 2 * cs), 1)
    rr, cc = row & (cs - 1), col & (cs - 1)
    strict = (cc > rr) if reverse else (cc < rr)
    mask = strict | ((cc == rr) & (row >= cs))
    eye = (lax.broadcasted_iota(jnp.int32, (cs, cs), 0) == lax.broadcasted_iota(jnp.int32, (cs, cs), 1)).astype(F32)

    heads = range(GROUP_WIDTH // HEAD_DIM)
    nh = len(heads)
    cat = lambda a, b: jnp.concatenate([a, b], axis=0)
    pairs = [(slice(gi * cs, (gi + 1) * cs), slice(h * HEAD_DIM, (h + 1) * HEAD_DIM))
             for gi in range(nsub) for h in heads]
    big = [jnp.where(mask, _dot_nt(cat(at_b[rs, s], rt_b[rs, s]), cat(bt_b[rs, s], kt_b[rs, s])), 0.0)
           for rs, s in pairs]
    akv = [_dot(b[:cs, cs:], v_b[rs, s]) for b, (rs, s) in zip(big, pairs)]
    a_ab = [b[:cs, :cs] for b in big]
    t_inv = [eye + x for x in a_ab]
    pw = [_dot(x, x) for x in a_ab]
    lvl = 4
    while lvl < cs:
        res = [_dot(cat(x, t), x) for x, t in zip(pw, t_inv)]
        t_inv = [t + x[cs:] for t, x in zip(t_inv, res)]
        pw = [x[:cs] for x in res]
        lvl *= 2
    t_inv = [t + _dot(t, x) for t, x in zip(t_inv, pw)]
    ta = [_dot(t, at_b[rs, s]) for t, (rs, s) in zip(t_inv, pairs)]
    tv = [_dot(t, x) for t, x in zip(t_inv, akv)]

    st = [state_ref[h] for h in heads]
    ys = [None] * nsub
    for gi in (range(nsub - 1, -1, -1) if reverse else range(nsub)):
        wc = jnp.exp(c[last_row(gi):last_row(gi) + 1, :])
        sl = slice(gi * nh, (gi + 1) * nh)
        u = [_dot_nt(x, s0) + z for x, s0, z in zip(ta[sl], st, tv[sl])]
        uv = [cat(x, v[rs, s]).astype(BF16) for x, (rs, s) in zip(u, pairs[sl])]
        ys[gi] = [_dot_nt(rt_b[rs, s], s0) + _dot(b[cs:, :], x)
                  for (rs, s), s0, b, x in zip(pairs[sl], st, big[sl], uv)]
        st = [s0 * wc[:, s] + _dot_tn(x, cat(bh_b[rs, s], kh_b[rs, s]))
              for s0, (rs, s), x in zip(st, pairs[sl], uv)]
    for gi in range(nsub):
        for h in heads:
            rs, s = pairs[gi * nh + h]
            y_ref[rs, s] = ys[gi][h]
    for h in heads:
        state_ref[h] = st[h]


def _rwkv(p, pvec, w2pad, a2pad, g512, batch, seq, reverse):
    n = p.shape[0]
    c = RWKV_BLOCK
    nc = seq // c
    if reverse:
        rowidx = lambda b, j: b * nc + (nc - 1 - j)
    else:
        rowidx = lambda b, j: b * nc + j
    wide = lambda cb: pl.BlockSpec((c, 512), lambda b, j, cb=cb: (rowidx(b, j), cb))
    narrow = lambda cb: pl.BlockSpec((c, 128), lambda b, j, cb=cb: (rowidx(b, j), cb))
    const = lambda shape: pl.BlockSpec(shape, lambda b, j: (0,) * len(shape))
    out = pl.BlockSpec((c, 512), lambda b, j: (rowidx(b, j), 0))
    return pl.pallas_call(
        functools.partial(_rwkv_kernel, n=c, cs=RWKV_CHUNK, reverse=reverse),
        grid=(batch, nc),
        in_specs=[wide(CB_BR), wide(CB_BK), wide(CB_BV), narrow(CB_SECA), narrow(CB_SECB),
                  const((8, 512)), const((128, 512)), const((128, 512)), const((512, 512))],
        out_specs=[out, out],
        out_shape=[jax.ShapeDtypeStruct((n, 512), F32), jax.ShapeDtypeStruct((n, 512), BF16)],
        scratch_shapes=[pltpu.VMEM((8, HEAD_DIM, HEAD_DIM), F32), pltpu.VMEM((8, 512), F32),
                        pltpu.VMEM((8, 128), F32)],
        compiler_params=_cparams(("parallel", "arbitrary")),
        name="rwkv_bwd" if reverse else "rwkv_fwd",
    )(p, p, p, p, p, pvec, w2pad, a2pad, g512)


def _hgrn_kernel(q_ref, v_ref, z_ref, lb_ref, o_ref, state_ref, *, n, cs, sub, reverse):
    @pl.when(pl.program_id(1) == 0)
    def _():
        state_ref[...] = jnp.zeros_like(state_ref)

    q, v, z = (ref[...].astype(F32) for ref in (q_ref, v_ref, z_ref))
    log_sig = -_softplus(-z)
    x1 = lb_ref[0:1, :]
    x2 = lb_ref[1:2, :] + log_sig
    logf = jnp.maximum(x1, x2) + jnp.log(1.0 + jnp.exp(-jnp.abs(x1 - x2)))
    kf = 1.0 - jnp.exp(logf)
    nchunk, nsub = n // cs, cs // sub
    last_row = lambda gi: gi * cs if reverse else (gi + 1) * cs - 1
    row1 = lax.broadcasted_iota(jnp.int32, (n, 1), 0)
    b = _scan_cumsum(logf, n, reverse, seg=cs)
    b_last = b[last_row(0):last_row(0) + 1, :]
    for gi in range(1, nchunk):
        b_last = jnp.where(row1 >= gi * cs, b[last_row(gi):last_row(gi) + 1, :], b_last)
    qe = q * jnp.exp(b)
    ke = kf * jnp.exp(b_last - b)
    b2 = b * LOG2_E
    c2 = b2 - jnp.log(kf) * LOG2_E
    srow = lax.broadcasted_iota(jnp.int32, (sub, sub), 0)
    lane = lax.broadcasted_iota(jnp.int32, (sub, sub), 1)
    tri = (srow <= lane) if reverse else (srow >= lane)
    hsl = [slice(h * D_HEAD_K, (h + 1) * D_HEAD_K) for h in range(D_HEADS)]

    intra = {}
    for gi in range(nchunk):
        base = gi * cs
        for h, hs in enumerate(hsl):
            for m in range(nsub):
                ms = slice(base + m * sub, base + (m + 1) * sub)
                q_m, b_m = q[ms, hs], b[ms, hs]
                b2_m, c2_m = b2[ms, hs], c2[ms, hs]
                dmat = jnp.zeros((sub, sub), F32)
                for s in range(sub):
                    w = q_m * jnp.exp2(b2_m - c2_m[s:s + 1])
                    dmat = jnp.where(lane == s, jnp.sum(w, axis=-1, keepdims=True), dmat)
                dmat = jnp.where(tri, dmat, 0.0)
                o_m = _dot(dmat, v[ms, hs])
                es = slice(base + (m + 1) * sub, base + cs) if reverse else slice(base, base + m * sub)
                if es.stop > es.start:
                    first = base + ((m + 1) * sub - 1 if reverse else m * sub)
                    rho = b[first:first + 1, hs] - logf[first:first + 1, hs]
                    qt = q_m * jnp.exp(b_m - rho)
                    kt = kf[es, hs] * jnp.exp(rho - b[es, hs])
                    o_m = o_m + _dot(_dot_nt(qt, kt), v[es, hs])
                intra[gi, h, m] = o_m

    st = [state_ref[h] for h in range(D_HEADS)]
    inter = {}
    for gi in (range(nchunk - 1, -1, -1) if reverse else range(nchunk)):
        rs = slice(gi * cs, (gi + 1) * cs)
        wend = jnp.exp(b[last_row(gi):last_row(gi) + 1, :])
        for h, hs in enumerate(hsl):
            inter[gi, h] = _dot_nt(qe[rs, hs], st[h])
        st = [st[h] * wend[:, hs] + _dot_tn(v[rs, hs], ke[rs, hs]) for h, hs in enumerate(hsl)]
    for (gi, h, m), o_m in intra.items():
        lo = gi * cs + m * sub
        o_ref[lo:lo + sub, hsl[h]] = (o_m + inter[gi, h][m * sub:(m + 1) * sub]).astype(o_ref.dtype)
    for h in range(D_HEADS):
        state_ref[h] = st[h]


def _hgrn(p, lbvec, batch, seq, reverse):
    n = p.shape[0]
    c = HGRN_BLOCK
    nc = seq // c
    if reverse:
        rowidx = lambda b, j: b * nc + (nc - 1 - j)
    else:
        rowidx = lambda b, j: b * nc + j
    wide = lambda cb: pl.BlockSpec((c, 512), lambda b, j, cb=cb: (rowidx(b, j), cb))
    return pl.pallas_call(
        functools.partial(_hgrn_kernel, n=c, cs=HGRN_CHUNK, sub=HGRN_SUB, reverse=reverse),
        grid=(batch, nc),
        in_specs=[wide(CB_DQ), wide(CB_DI), wide(CB_DFB if reverse else CB_DFF),
                  pl.BlockSpec((8, 512), lambda b, j: (0, 0))],
        out_specs=pl.BlockSpec((c, 512), lambda b, j: (rowidx(b, j), 0)),
        out_shape=jax.ShapeDtypeStruct((n, 512), BF16),
        scratch_shapes=[pltpu.VMEM((D_HEADS, D_HEAD_K, D_HEAD_K), F32)],
        compiler_params=_cparams(("parallel", "arbitrary")),
        name="hgrn_bwd" if reverse else "hgrn_fwd",
    )(p, p, p, lbvec)


def _mixpost_kernel(o1_ref, l1_ref, o2_ref, l2_ref, o3_ref, l3_ref,
                    y0_ref, y1_ref, bo0_ref, bo1_ref, g1_ref, oc_ref,
                    h0_ref, h1_ref, dg_ref, pv_ref, g_ref, g2_ref, mix_ref,
                    o2s_ref, l2s_ref, o3s_ref, l3s_ref, *, tm):
    for d, src, dst in ((4, o2_ref, o2s_ref), (16, o3_ref, o3s_ref)):
        for r in range(d):
            for c in range(4):
                lo = r * 512 + c * 128
                dst[c, pl.ds(r, tm // d, stride=d), :] = src[:, lo:lo + 128].astype(F32)
    for d, src, dst in ((4, l2_ref, l2s_ref), (16, l3_ref, l3s_ref)):
        for r in range(d):
            dst[0, pl.ds(r, tm // d, stride=d), :] = src[:, r * LSE_WIDTH:(r + 1) * LSE_WIDTH]
    wide = lambda ref: jnp.concatenate([ref[c] for c in range(4)], axis=1)
    l1, l2, l3 = l1_ref[...], l2s_ref[0], l3s_ref[0]
    mx = jnp.maximum(jnp.maximum(l1, l2), l3)
    w1, w2, w3 = jnp.exp(l1 - mx), jnp.exp(l2 - mx), jnp.exp(l3 - mx)
    inv = 1.0 / (w1 + w2 + w3)
    src_lane = lax.broadcasted_iota(jnp.int32, (LSE_WIDTH, GROUP_WIDTH), 0)
    dst_head = lax.broadcasted_iota(jnp.int32, (LSE_WIDTH, GROUP_WIDTH), 1) // HEAD_DIM
    spread = (src_lane == dst_head * (LSE_WIDTH * HEAD_DIM // GROUP_WIDTH)).astype(BF16)
    e1, e2, e3 = (_seg_sum(w * inv, spread) for w in (w1, w2, w3))
    out_a = o1_ref[...] * e1 + wide(o2s_ref) * e2 + wide(o3s_ref) * e3
    mix_ref[:, 0:512] = out_a.astype(BF16)
    g = g_ref[...]
    y = y0_ref[...] + y1_ref[...]
    mu = _seg_sum(y, g) * (1.0 / HEAD_DIM)
    yc = y - mu
    var = _seg_sum(yc * yc, g) * (1.0 / HEAD_DIM)
    yn = yc * lax.rsqrt(var + LNX_EPS) * pv_ref[0:1, :] + pv_ref[1:2, :]
    gate = jnp.dot(jax.nn.sigmoid(g1_ref[...].astype(F32)).astype(BF16), g2_ref[...], preferred_element_type=F32)
    mix_ref[:, 512:1024] = ((yn + bo0_ref[...] + bo1_ref[...]) * gate).astype(BF16)
    mix_ref[:, 1024:1536] = oc_ref[...]
    o = h0_ref[...].astype(F32) + h1_ref[...].astype(F32)
    dg = dg_ref[...].astype(F32)
    silu = dg * jax.nn.sigmoid(dg)
    for h in range(D_HEADS):
        hs = slice(h * D_HEAD_K, (h + 1) * D_HEAD_K)
        oh = o[:, hs]
        ms = jnp.mean(oh * oh, axis=-1, keepdims=True)
        res = oh * lax.rsqrt(ms + NORM_EPS) * pv_ref[2:3, hs] * silu[:, hs]
        mix_ref[:, 1536 + h * D_HEAD_K:1536 + (h + 1) * D_HEAD_K] = res.astype(BF16)


def _mixpost(a_parts, yb, bonus, p, oc, od, pvec, g512, g2pad, tm=256):
    n = p.shape[0]
    blk = pl.BlockSpec((tm, 512), lambda i: (i, 0))
    const = lambda shape: pl.BlockSpec(shape, lambda i: (0,) * len(shape))
    args = []
    for o, l in a_parts:
        args += [o, l]
    args += [yb[0], yb[1], bonus[0], bonus[1], p, oc, od[0], od[1], p, pvec, g512, g2pad]
    dil = lambda d, w=512: pl.BlockSpec((tm // d, d * w), lambda i: (i, 0))
    in_specs = [blk, dil(1, LSE_WIDTH), dil(4), dil(4, LSE_WIDTH), dil(16), dil(16, LSE_WIDTH)] + [blk] * 4 + [
        pl.BlockSpec((tm, 128), lambda i: (i, CB_G1)), blk, blk, blk,
        pl.BlockSpec((tm, 512), lambda i: (i, CB_DG)),
        const((8, 512)), const((512, 512)), const((128, 512))]
    return pl.pallas_call(
        functools.partial(_mixpost_kernel, tm=tm),
        grid=(n // tm,),
        in_specs=in_specs,
        out_specs=pl.BlockSpec((tm, D_MODEL), lambda i: (i, 0)),
        out_shape=jax.ShapeDtypeStruct((n, D_MODEL), BF16),
        scratch_shapes=[pltpu.VMEM((4, tm, 128), F32), pltpu.VMEM((1, tm, LSE_WIDTH), F32)] * 2,
        compiler_params=_cparams(("parallel",)),
        name="mixpost",
    )(*args)


def _outproj_kernel(mix_ref, w_ref, x_ref, o_ref):
    o_ref[...] = x_ref[...] + jnp.dot(mix_ref[...], w_ref[...], preferred_element_type=F32)


def _outproj(mix, w, x2, tm=512, tn=2048):
    n, d = x2.shape
    return pl.pallas_call(
        _outproj_kernel,
        grid=(n // tm, d // tn),
        in_specs=[pl.BlockSpec((tm, mix.shape[1]), lambda i, j: (i, 0)),
                  pl.BlockSpec((mix.shape[1], tn), lambda i, j: (0, j)),
                  pl.BlockSpec((tm, tn), lambda i, j: (i, j))],
        out_specs=pl.BlockSpec((tm, tn), lambda i, j: (i, j)),
        out_shape=jax.ShapeDtypeStruct((n, d), F32),
        compiler_params=_cparams(("parallel", "arbitrary")),
        name="outproj",
    )(mix, w, x2)


def _rope_tables(seq):
    half = ROT_DIM // 2
    inv = ROPE_THETA ** (-jnp.arange(0, ROT_DIM, 2, dtype=F32) / ROT_DIM)
    ang = jnp.arange(seq, dtype=F32)[:, None] * inv[None]
    cos, sin = jnp.cos(ang), jnp.sin(ang)
    ones = jnp.ones((seq, HEAD_DIM - ROT_DIM), F32)
    zeros = jnp.zeros((seq, HEAD_DIM - ROT_DIM), F32)
    z8 = jnp.zeros((seq, half), F32)
    c = jnp.concatenate([cos, cos, ones], axis=1)
    s1 = jnp.concatenate([-sin, z8, zeros], axis=1)
    s2 = jnp.concatenate([z8, sin, zeros], axis=1)
    tile2 = lambda t: jnp.concatenate([t, t], axis=1)
    return tile2(c), tile2(s1), tile2(s2)


def _block_diag_ones(width, seg):
    idx = jnp.arange(width) // seg
    return (idx[:, None] == idx[None, :]).astype(BF16)


def _pad_rows(rows, width=512, total=8):
    rows = [jnp.pad(r.astype(F32).reshape(-1), (0, width - r.size)) for r in rows]
    rows += [jnp.zeros((width,), F32)] * (total - len(rows))
    return jnp.stack(rows)


def _w_ext(w_in, mu_wa, w1, a1, g1):
    cols_a, cols_b = [], []
    for d in range(2):
        for mu, w in ((mu_wa[d, 0], w1[d]), (mu_wa[d, 1], a1[d])):
            cols_a.append((1.0 - mu)[:, None] * w)
            cols_b.append(mu[:, None] * w)
    parts = [w_in[:, 0:3584], w_in[:, 3840:6400], w_in[:, 3584:3840]] + cols_a + cols_b
    parts.append(jnp.pad(g1, ((0, 0), (0, 128 - g1.shape[1]))))
    w = jnp.concatenate(parts, axis=1)
    return jnp.pad(w, ((0, 0), (0, D_IN_EXT - w.shape[1]))).astype(BF16)


def _pad_lowrank(w, row0):
    return jnp.pad(w, ((row0, 128 - row0 - w.shape[0]), (0, 0))).astype(BF16)


def kernel(x, ln_gain, w_in, w_out, w_ffn_in, w_ffn_out, qk_gain, sink, b_mu_rkv, b_mu_wa, b_w0, b_w1,
           b_w2, b_a0, b_a1, b_a2, b_k_k, b_k_a, b_r_k, b_lnx_gain, b_lnx_bias, b_g1, b_g2,
           d_lb_logits, d_norm_gain):
    batch, seq, d = x.shape
    x2 = x.reshape(batch * seq, d)
    rope_c, rope_s1, rope_s2 = _rope_tables(seq)
    g512 = _block_diag_ones(512, HEAD_DIM)
    pr = jax.nn.softmax(d_lb_logits.astype(F32), axis=0)
    cs = jnp.cumsum(pr, axis=0)
    lb_all = cs - cs[0:1]

    ffn_in, ffn_out = w_ffn_in.astype(BF16), w_ffn_out.astype(BF16)

    for l in range(DEPTH):
        x2 = _ffn(x2, ln_gain[l, 0][None], ffn_in, ffn_out, l, 0)

        p = _inproj(x2, ln_gain[l, 1][None], _w_ext(w_in[l], b_mu_wa[l], b_w1[l], b_a1[l], b_g1[l]))

        tile8 = lambda t: jnp.tile(t, 8)
        gains = _pad_rows([tile8(qk_gain[l, 0, 0]), tile8(qk_gain[l, 0, 1]),
                           tile8(qk_gain[l, 1, 0]), jnp.tile(qk_gain[l, 1, 1], 2)])
        prepped = _prep(p, rope_c, rope_s1, rope_s2, gains, g512, seq)
        qc, kc, vc = prepped[9:]

        a_parts = []
        for pi, (window, dil) in enumerate(DILATED_PATTERNS):
            qa, ka, va = prepped[3 * pi:3 * pi + 3]
            a_parts.append(_band_attention(qa, ka, va, None, batch, seq, dil, window // (2 * dil),
                                           GROUP_WIDTH // HEAD_DIM, 1))
        oc = _band_attention(qc, kc, vc, sink[l][None].astype(F32), batch, seq, 1, C_HALF_WINDOW,
                             C_KV_HEADS, C_Q_HEADS // C_KV_HEADS)

        yb, bonus = [], []
        for dr in range(2):
            pvec = _pad_rows([b_mu_rkv[l, dr, 0], b_mu_rkv[l, dr, 1], b_mu_rkv[l, dr, 2], b_w0[l, dr],
                              b_a0[l, dr], b_k_k[l], b_k_a[l], b_r_k[l]])
            y, bo = _rwkv(p, pvec, _pad_lowrank(b_w2[l, dr], 64 * dr), _pad_lowrank(b_a2[l, dr], 64 * dr + 32),
                          g512, batch, seq, reverse=(dr == 1))
            yb.append(y)
            bonus.append(bo)

        od = []
        for dr in range(2):
            lb = lb_all[l, dr]
            od.append(_hgrn(p, _pad_rows([jnp.log(lb), jnp.log1p(-lb)]), batch, seq, reverse=(dr == 1)))

        pvec = _pad_rows([b_lnx_gain[l], b_lnx_bias[l], d_norm_gain[l]])
        mix = _mixpost(a_parts, yb, bonus, p, oc, od, pvec, g512, _pad_lowrank(b_g2[l], 0))
        x2 = _outproj(mix, w_out[l].astype(BF16), x2)

        x2 = _ffn(x2, ln_gain[l, 2][None], ffn_in, ffn_out, l, 1)
    return x2.reshape(batch, seq, d)
```

```python
import functools

import jax
import jax.numpy as jnp
from jax import lax
from jax.experimental import pallas as pl
from jax.experimental.pallas import tpu as pltpu

F32 = jnp.float32
BF16 = jnp.bfloat16

D_MODEL = 2048
DEPTH = 4
HEAD_DIM = 64
GROUP_WIDTH = 512
DILATED_PATTERNS = ((128, 1), (512, 4), (2048, 16))
LNX_EPS = 64e-5
C_Q_HEADS = 8
C_KV_HEADS = 2
C_HALF_WINDOW = 128
D_HEAD_K = 128
D_HEADS = 4
D_FF = 5632
ROPE_THETA = 500000.0
ROT_DIM = 16
NORM_EPS = 1e-6
NEG_INF = -1e30
LOG2_E = 1.4426950408889634

D_IN_EXT = 6912
(CB_AQ, CB_AK, CB_AV, CB_BR, CB_BK, CB_BV, CB_CQ, CB_DQ, CB_DI, CB_DG, CB_DFF, CB_DFB) = range(12)
CB_CK, CB_CV, CB_SECA, CB_SECB, CB_G1 = 48, 49, 50, 51, 52

VMEM_LIMIT = 56 * 1024 * 1024
RWKV_CHUNK = 128
RWKV_BLOCK = 512
HGRN_CHUNK = 64
LSE_WIDTH = 128
ATTN_SUB_ROWS = 256
HGRN_BLOCK = (512, 256)
HGRN_SUB = 16


def _cparams(sem):
    return pltpu.CompilerParams(dimension_semantics=sem, vmem_limit_bytes=VMEM_LIMIT)


def _seg_sum(x, g):
    hi = x.astype(BF16)
    lo = (x - hi.astype(F32)).astype(BF16)
    d = lambda a: jnp.dot(a, g, preferred_element_type=F32)
    return d(hi) + d(lo)


def _dot(a, b):
    return jnp.dot(a.astype(BF16), b.astype(BF16), preferred_element_type=F32)


def _dot_nt(a, b):
    return lax.dot_general(a.astype(BF16), b.astype(BF16), (((1,), (1,)), ((), ())),
                           preferred_element_type=F32)


def _dot_tn(a, b):
    return lax.dot_general(a.astype(BF16), b.astype(BF16), (((0,), (0,)), ((), ())),
                           preferred_element_type=F32)


def _softplus(u):
    return jnp.maximum(u, 0.0) + jnp.log(1.0 + jnp.exp(-jnp.abs(u)))


def _scan_cumsum(x, n, reverse, seg=None):
    seg = n if seg is None else seg
    row = lax.broadcasted_iota(jnp.int32, x.shape, 0) & (seg - 1)
    s = 1
    while s < seg:
        if reverse:
            x = x + jnp.where(row < seg - s, pltpu.roll(x, n - s, 0), 0.0)
        else:
            x = x + jnp.where(row >= s, pltpu.roll(x, s, 0), 0.0)
        s *= 2
    return x


def _scan_shift(x, carry_row, n, reverse):
    row = lax.broadcasted_iota(jnp.int32, x.shape, 0)
    if reverse:
        return jnp.where(row == n - 1, carry_row, pltpu.roll(x, n - 1, 0))
    return jnp.where(row == 0, carry_row, pltpu.roll(x, 1, 0))


def _ffn_kernel(x_ref, g_ref, wg_ref, wu_ref, wo_ref, o_ref, hn_ref):
    @pl.when(pl.program_id(1) == 0)
    def _():
        x = x_ref[...]
        ms = jnp.mean(x * x, axis=-1, keepdims=True)
        hn_ref[...] = (x * lax.rsqrt(ms + NORM_EPS) * g_ref[...]).astype(BF16)
        o_ref[...] = x

    h = hn_ref[...]
    gate = jnp.dot(h, wg_ref[...], preferred_element_type=F32)
    up = jnp.dot(h, wu_ref[...], preferred_element_type=F32)
    act = (0.5 * gate * jax.nn.sigmoid(gate) * up).astype(BF16)
    o_ref[...] += jnp.dot(act, wo_ref[...], preferred_element_type=F32)


def _ffn(x2, gain, w_in, w_out, layer, which, tm=1024, tf=512):
    n, d = x2.shape
    nf = D_FF // tf
    return pl.pallas_call(
        _ffn_kernel,
        grid=(n // tm, nf),
        in_specs=[
            pl.BlockSpec((tm, d), lambda i, j: (i, 0)),
            pl.BlockSpec((1, d), lambda i, j: (0, 0)),
            pl.BlockSpec((None, None, d, tf), lambda i, j: (layer, which, 0, j)),
            pl.BlockSpec((None, None, d, tf), lambda i, j: (layer, which, 0, j + nf)),
            pl.BlockSpec((None, None, tf, d), lambda i, j: (layer, which, j, 0)),
        ],
        out_specs=pl.BlockSpec((tm, d), lambda i, j: (i, 0)),
        out_shape=jax.ShapeDtypeStruct((n, d), F32),
        scratch_shapes=[pltpu.VMEM((tm, d), BF16)],
        compiler_params=_cparams(("parallel", "arbitrary")),
        name="ffn",
    )(x2, gain, w_in, w_in, w_out)


def _inproj_kernel(x_ref, g_ref, w_ref, o_ref, hn_ref):
    @pl.when(pl.program_id(1) == 0)
    def _():
        x = x_ref[...]
        ms = jnp.mean(x * x, axis=-1, keepdims=True)
        hn_ref[...] = (x * lax.rsqrt(ms + NORM_EPS) * g_ref[...]).astype(BF16)

    o_ref[...] = jnp.dot(hn_ref[...], w_ref[...], preferred_element_type=F32).astype(o_ref.dtype)


def _inproj(x2, gain, w_ext, tm=1024, tn=2304):
    n, d = x2.shape
    nc = w_ext.shape[1]
    return pl.pallas_call(
        _inproj_kernel,
        grid=(n // tm, nc // tn),
        in_specs=[
            pl.BlockSpec((tm, d), lambda i, j: (i, 0)),
            pl.BlockSpec((1, d), lambda i, j: (0, 0)),
            pl.BlockSpec((d, tn), lambda i, j: (0, j)),
        ],
        out_specs=pl.BlockSpec((tm, tn), lambda i, j: (i, j)),
        out_shape=jax.ShapeDtypeStruct((n, nc), BF16),
        scratch_shapes=[pltpu.VMEM((tm, d), BF16)],
        compiler_params=_cparams(("parallel", "arbitrary")),
        name="inproj",
    )(x2, gain, w_ext)


def _norm_rope(t, gain, g, c, s1, s2, scale):
    w = t.shape[-1]
    ss = _seg_sum(t * t, g)
    y = t * lax.rsqrt(ss * (1.0 / HEAD_DIM) + NORM_EPS) * gain
    half = ROT_DIM // 2
    out = y * c + pltpu.roll(y, w - half, 1) * s1 + pltpu.roll(y, half, 1) * s2
    return out * scale if scale != 1.0 else out


def _prep_kernel(aq_ref, ak_ref, av_ref, cq_ref, ck_ref, cv_ref, c_ref, s1_ref, s2_ref,
                 gain_ref, g_ref,
                 qa_ref, ka_ref, va_ref, qa4_ref, ka4_ref, va4_ref, qa16_ref, ka16_ref, va16_ref,
                 qc_ref, kc_ref, vc_ref, scr_ref, *, tm):
    c1, s11, s21 = c_ref[...], s1_ref[...], s2_ref[...]
    c4 = jnp.concatenate([c1] * 4, axis=1)
    s14 = jnp.concatenate([s11] * 4, axis=1)
    s24 = jnp.concatenate([s21] * 4, axis=1)
    g = g_ref[...]
    g1 = g_ref[0:128, 0:128]
    scale = HEAD_DIM ** -0.5

    def emit(y, nat_ref, dil_refs):
        nat_ref[...] = y.astype(BF16)
        for c in range(4):
            scr_ref[c] = y[:, c * 128:(c + 1) * 128]
        for d, ref in dil_refs:
            for r in range(d):
                for c in range(4):
                    lo = r * 512 + c * 128
                    ref[:, lo:lo + 128] = scr_ref[c, pl.ds(r, tm // d, stride=d), :].astype(BF16)

    emit(_norm_rope(aq_ref[...].astype(F32), gain_ref[0:1, :], g, c4, s14, s24, scale), qa_ref,
         ((4, qa4_ref), (16, qa16_ref)))
    emit(_norm_rope(ak_ref[...].astype(F32), gain_ref[1:2, :], g, c4, s14, s24, 1.0), ka_ref,
         ((4, ka4_ref), (16, ka16_ref)))
    emit(av_ref[...].astype(F32), va_ref, ((4, va4_ref), (16, va16_ref)))
    qc_ref[...] = _norm_rope(cq_ref[...].astype(F32), gain_ref[2:3, :], g, c4, s14, s24, scale).astype(BF16)
    kc_ref[...] = _norm_rope(ck_ref[...].astype(F32), gain_ref[3:4, 0:128], g1, c1, s11, s21, 1.0).astype(BF16)
    vc_ref[...] = cv_ref[...].astype(BF16)


def _prep(p, rope_c, rope_s1, rope_s2, gains, g512, seq, tm=512):
    n = p.shape[0]
    nseq = seq // tm
    wide = lambda cb: pl.BlockSpec((tm, 512), lambda i, cb=cb: (i, cb))
    narrow = lambda cb: pl.BlockSpec((tm, 128), lambda i, cb=cb: (i, cb))
    tab = pl.BlockSpec((tm, 128), lambda i: (i % nseq, 0))
    rowblk = lambda rows, width: pl.BlockSpec((rows, width), lambda i: (i, 0))
    shape = lambda rows, width: jax.ShapeDtypeStruct((rows, width), BF16)
    trio = lambda d: [rowblk(tm // d, d * 512)] * 3
    trio_shape = lambda d: [shape(n // d, d * 512)] * 3
    return pl.pallas_call(
        functools.partial(_prep_kernel, tm=tm),
        grid=(n // tm,),
        in_specs=[wide(CB_AQ), wide(CB_AK), wide(CB_AV), wide(CB_CQ), narrow(CB_CK), narrow(CB_CV),
                  tab, tab, tab,
                  pl.BlockSpec((8, 512), lambda i: (0, 0)),
                  pl.BlockSpec((512, 512), lambda i: (0, 0))],
        out_specs=trio(1) + trio(4) + trio(16) + [rowblk(tm, 512), rowblk(tm, 128), rowblk(tm, 128)],
        out_shape=trio_shape(1) + trio_shape(4) + trio_shape(16) + [shape(n, 512), shape(n, 128), shape(n, 128)],
        scratch_shapes=[pltpu.VMEM((4, tm, 128), F32)],
        compiler_params=_cparams(("parallel",)),
        name="prep_qk",
    )(p, p, p, p, p, p, rope_c, rope_s1, rope_s2, gains, g512)


def _band_kernel(*refs, tq, hb, sb, half, length, n_kv, group, with_sink):
    if with_sink:
        q_ref, kp_ref, km_ref, kn_ref, vp_ref, vm_ref, vn_ref, sink_ref, o_ref = refs
    else:
        q_ref, kp_ref, km_ref, kn_ref, vp_ref, vm_ref, vn_ref, o_ref, lse_ref = refs
    i = pl.program_id(2)
    nsub = tq // sb
    gs, nwin = group * sb, sb + 2 * hb
    row = lax.broadcasted_iota(jnp.int32, (gs, nwin), 0) & (sb - 1)
    col = lax.broadcasted_iota(jnp.int32, (gs, nwin), 1)
    band = jnp.abs(row + hb - col) <= half
    valid = []
    for j in range(nsub):
        kpos = i * tq + j * sb - hb + col
        valid.append(band & (kpos >= 0) & (kpos < length))
    q_all = q_ref[0]
    n_heads = n_kv * group
    hsl = [slice(h * HEAD_DIM, (h + 1) * HEAD_DIM) for h in range(n_heads)]
    k_heads = [jnp.concatenate([kp_ref[0, :, s], km_ref[0, :, s], kn_ref[0, :, s]], axis=0) for s in hsl[:n_kv]]
    v_heads = [jnp.concatenate([vp_ref[0, :, s], vm_ref[0, :, s], vn_ref[0, :, s]], axis=0) for s in hsl[:n_kv]]
    kvs = range(n_kv)
    if with_sink:
        sinks = sink_ref[...]
        rowg = lax.broadcasted_iota(jnp.int32, (gs, 1), 0) // sb
        sink_col = []
        for kv in kvs:
            col_kv = jnp.zeros((gs, 1), F32)
            for g in range(group):
                h = kv * group + g
                col_kv = jnp.where(rowg == g, sinks[0:1, h:h + 1], col_kv)
            sink_col.append(col_kv)
    lse_lane_head = lax.broadcasted_iota(jnp.int32, (sb, LSE_WIDTH), 1) // (LSE_WIDTH // n_heads)
    outs, lses = [], []
    for j in range(nsub):
        rows, win = slice(j * sb, (j + 1) * sb), slice(j * sb, (j + 1) * sb + 2 * hb)
        qs = [jnp.concatenate([q_all[rows, hsl[kv * group + g]] for g in range(group)], axis=0) for kv in kvs]
        s = [lax.dot_general(qs[kv], k_heads[kv][win], (((1,), (1,)), ((), ())),
                             preferred_element_type=F32) for kv in kvs]
        s = [jnp.where(valid[j], x, NEG_INF) for x in s]
        m = [jnp.max(x, axis=-1, keepdims=True) for x in s]
        p = [jnp.exp(x - y) for x, y in zip(s, m)]
        l = [jnp.sum(x, axis=-1, keepdims=True) for x in p]
        acc = [jnp.dot(p[kv].astype(BF16), v_heads[kv][win], preferred_element_type=F32) for kv in kvs]
        if with_sink:
            m_all = [jnp.maximum(x, y) for x, y in zip(m, sink_col)]
            sc = [jnp.exp(x - y) for x, y in zip(m, m_all)]
            den = [l[kv] * sc[kv] + jnp.exp(sink_col[kv] - m_all[kv]) for kv in kvs]
            o = [acc[kv] * (sc[kv] / den[kv]) for kv in kvs]
        else:
            o = [x / y for x, y in zip(acc, l)]
            lse = [x + jnp.log(y) for x, y in zip(m, l)]
            tile = jnp.zeros((sb, LSE_WIDTH), F32)
            for h in range(n_heads):
                tile = jnp.where(lse_lane_head == h, lse[h // group][(h % group) * sb:(h % group + 1) * sb], tile)
            lses.append(tile)
        outs.append(jnp.concatenate([o[h // group][(h % group) * sb:(h % group + 1) * sb]
                                     for h in range(n_heads)], axis=1))
    for j in range(nsub):
        o_ref[0, j * sb:(j + 1) * sb, :] = outs[j].astype(o_ref.dtype)
        if not with_sink:
            lse_ref[0, j * sb:(j + 1) * sb, :] = lses[j]


def _band_attention(q, k, v, sink, batch, seq, dil, half, n_kv, group, tq=512):
    wq, wk = q.shape[1] // dil, k.shape[1] // dil
    hb = half
    sl = seq // dil
    tq = min(tq, sl)
    nq = sl // tq
    per = tq // hb
    nhb = sl // hb
    qv = q.reshape(batch, sl, dil * wq)
    kv = k.reshape(batch, sl, dil * wk)
    vv = v.reshape(batch, sl, dil * wk)
    main = lambda w: pl.BlockSpec((1, tq, w), lambda b, r, i: (b, i, r))
    prev = pl.BlockSpec((1, hb, wk), lambda b, r, i: (b, jnp.maximum(i * per - 1, 0), r))
    nxt = pl.BlockSpec((1, hb, wk), lambda b, r, i: (b, jnp.minimum((i + 1) * per, nhb - 1), r))
    with_sink = sink is not None
    in_specs = [main(wq), prev, main(wk), nxt, prev, main(wk), nxt]
    args = [qv, kv, kv, kv, vv, vv, vv]
    if with_sink:
        in_specs.append(pl.BlockSpec((1, wq // HEAD_DIM), lambda b, r, i: (0, 0)))
        args.append(sink)
        out_specs = main(wq)
        out_shape = jax.ShapeDtypeStruct(qv.shape, BF16)
    else:
        out_specs = [main(wq), main(LSE_WIDTH)]
        out_shape = [jax.ShapeDtypeStruct(qv.shape, BF16),
                     jax.ShapeDtypeStruct((batch, sl, dil * LSE_WIDTH), F32)]
    kern = functools.partial(_band_kernel, tq=tq, hb=hb, sb=min(tq, max(hb, ATTN_SUB_ROWS // group)),
                             half=half, length=sl,
                             n_kv=n_kv, group=group, with_sink=with_sink)
    out = pl.pallas_call(
        kern,
        grid=(batch, dil, nq),
        in_specs=in_specs,
        out_specs=out_specs,
        out_shape=out_shape,
        compiler_params=_cparams(("parallel", "parallel", "parallel")),
        name="band_attn_sink" if with_sink else f"band_attn_d{dil}",
    )(*args)
    flat = lambda t: t.reshape(batch * sl, t.shape[-1])
    if with_sink:
        return flat(out)
    return flat(out[0]), flat(out[1])


def _rwkv_kernel(r_ref, k_ref, v_ref, sa_ref, sb_ref, pv_ref, w2_ref, a2_ref, g_ref,
                 y_ref, bonus_ref, state_ref, carry_ref, carryb_ref, *, n, cs, reverse):
    @pl.when(pl.program_id(1) == 0)
    def _():
        state_ref[...] = jnp.zeros_like(state_ref)
        carry_ref[...] = jnp.zeros_like(carry_ref)
        carryb_ref[...] = jnp.zeros_like(carryb_ref)

    last = 0 if reverse else n - 1
    r0, k0, v0, sb = (ref[...].astype(F32) for ref in (r_ref, k_ref, v_ref, sb_ref))
    shift = lambda x, c: _scan_shift(x, c, n, reverse)
    r = r0 + (shift(r0, carry_ref[0:1, :]) - r0) * pv_ref[0:1, :]
    k = k0 + (shift(k0, carry_ref[1:2, :]) - k0) * pv_ref[1:2, :]
    v = v0 + (shift(v0, carry_ref[2:3, :]) - v0) * pv_ref[2:3, :]
    xlr = sa_ref[...].astype(F32) + shift(sb, carryb_ref[0:1, :])
    carry_ref[0:1, :] = r0[last:last + 1, :]
    carry_ref[1:2, :] = k0[last:last + 1, :]
    carry_ref[2:3, :] = v0[last:last + 1, :]
    carryb_ref[0:1, :] = sb[last:last + 1, :]

    wl = pv_ref[3:4, :] + jnp.dot(jnp.tanh(xlr).astype(BF16), w2_ref[...], preferred_element_type=F32)
    w_log = -_softplus(-wl) - 0.5
    ld = -jnp.exp(w_log)
    a = jax.nn.sigmoid(pv_ref[4:5, :] + jnp.dot(xlr.astype(BF16), a2_ref[...], preferred_element_type=F32))
    g = g_ref[...]
    kkr = k * pv_ref[5:6, :]
    kx = k * (1.0 + (a - 1.0) * pv_ref[6:7, :])
    sums = _seg_sum(jnp.concatenate([kkr * kkr, r * kx * pv_ref[7:8, :]], axis=0), g)
    kk = kkr / jnp.maximum(jnp.sqrt(sums[:n]), 1e-12)
    bonus_ref[...] = (sums[n:] * v).astype(bonus_ref.dtype)

    nsub = n // cs
    last_row = lambda gi: gi * cs if reverse else (gi + 1) * cs - 1
    row1 = lax.broadcasted_iota(jnp.int32, (n, 1), 0)
    c = _scan_cumsum(ld, n, reverse, seg=cs)
    c_last = c[last_row(0):last_row(0) + 1, :]
    for gi in range(1, nsub):
        c_last = jnp.where(row1 >= gi * cs, c[last_row(gi):last_row(gi) + 1, :], c_last)
    e_c = jnp.exp(c)
    e_cx = jnp.exp(c - ld)
    e_nc = jnp.exp(-c)
    e_end = jnp.exp(c_last - c)
    kb = kk * a
    rt = r * e_c
    at = -kk * e_cx
    bt = kb * e_nc
    kt = kx * e_nc
    bh = kb * e_end
    kh = kx * e_end

    rt_b, at_b, bt_b, kt_b, v_b = (t.astype(BF16) for t in (rt, at, bt, kt, v))
    bh_b, kh_b = bh.astype(BF16), kh.astype(BF16)

    row = lax.broadcasted_iota(jnp.int32, (2 * cs, 2 * cs), 0)
    col = lax.broadcasted_iota(jnp.int32, (2 * cs, 2 * cs), 1)
    rr, cc = row & (cs - 1), col & (cs - 1)
    strict = (cc > rr) if reverse else (cc < rr)
    mask = strict | ((cc == rr) & (row >= cs))
    eye = (lax.broadcasted_iota(jnp.int32, (cs, cs), 0) == lax.broadcasted_iota(jnp.int32, (cs, cs), 1)).astype(F32)

    heads = range(GROUP_WIDTH // HEAD_DIM)
    nh = len(heads)
    cat = lambda a, b: jnp.concatenate([a, b], axis=0)
    pairs = [(slice(gi * cs, (gi + 1) * cs), slice(h * HEAD_DIM, (h + 1) * HEAD_DIM))
             for gi in range(nsub) for h in heads]
    big = [jnp.where(mask, _dot_nt(cat(at_b[rs, s], rt_b[rs, s]), cat(bt_b[rs, s], kt_b[rs, s])), 0.0)
           for rs, s in pairs]
    akv = [_dot(b[:cs, cs:], v_b[rs, s]) for b, (rs, s) in zip(big, pairs)]
    a_ab = [b[:cs, :cs] for b in big]
    t_inv = [eye + x for x in a_ab]
    pw = [_dot(x, x) for x in a_ab]
    lvl = 4
    while lvl < cs:
        res = [_dot(cat(x, t), x) for x, t in zip(pw, t_inv)]
        t_inv = [t + x[cs:] for t, x in zip(t_inv, res)]
        pw = [x[:cs] for x in res]
        lvl *= 2
    t_inv = [t + _dot(t, x) for t, x in zip(t_inv, pw)]
    ta = [_dot(t, at_b[rs, s]) for t, (rs, s) in zip(t_inv, pairs)]
    tv = [_dot(t, x) for t, x in zip(t_inv, akv)]

    st = [state_ref[h] for h in heads]
    ys = [None] * nsub
    for gi in (range(nsub - 1, -1, -1) if reverse else range(nsub)):
        wc = jnp.exp(c[last_row(gi):last_row(gi) + 1, :])
        sl = slice(gi * nh, (gi + 1) * nh)
        u = [_dot_nt(x, s0) + z for x, s0, z in zip(ta[sl], st, tv[sl])]
        uv = [cat(x, v[rs, s]).astype(BF16) for x, (rs, s) in zip(u, pairs[sl])]
        ys[gi] = [_dot_nt(rt_b[rs, s], s0) + _dot(b[cs:, :], x)
                  for (rs, s), s0, b, x in zip(pairs[sl], st, big[sl], uv)]
        st = [s0 * wc[:, s] + _dot_tn(x, cat(bh_b[rs, s], kh_b[rs, s]))
              for s0, (rs, s), x in zip(st, pairs[sl], uv)]
    for gi in range(nsub):
        for h in heads:
            rs, s = pairs[gi * nh + h]
            y_ref[rs, s] = ys[gi][h]
    for h in heads:
        state_ref[h] = st[h]


def _rwkv(p, pvec, w2pad, a2pad, g512, batch, seq, reverse):
    n = p.shape[0]
    c = RWKV_BLOCK
    nc = seq // c
    if reverse:
        rowidx = lambda b, j: b * nc + (nc - 1 - j)
    else:
        rowidx = lambda b, j: b * nc + j
    wide = lambda cb: pl.BlockSpec((c, 512), lambda b, j, cb=cb: (rowidx(b, j), cb))
    narrow = lambda cb: pl.BlockSpec((c, 128), lambda b, j, cb=cb: (rowidx(b, j), cb))
    const = lambda shape: pl.BlockSpec(shape, lambda b, j: (0,) * len(shape))
    out = pl.BlockSpec((c, 512), lambda b, j: (rowidx(b, j), 0))
    return pl.pallas_call(
        functools.partial(_rwkv_kernel, n=c, cs=RWKV_CHUNK, reverse=reverse),
        grid=(batch, nc),
        in_specs=[wide(CB_BR), wide(CB_BK), wide(CB_BV), narrow(CB_SECA), narrow(CB_SECB),
                  const((8, 512)), const((128, 512)), const((128, 512)), const((512, 512))],
        out_specs=[out, out],
        out_shape=[jax.ShapeDtypeStruct((n, 512), F32), jax.ShapeDtypeStruct((n, 512), BF16)],
        scratch_shapes=[pltpu.VMEM((8, HEAD_DIM, HEAD_DIM), F32), pltpu.VMEM((8, 512), F32),
                        pltpu.VMEM((8, 128), F32)],
        compiler_params=_cparams(("parallel", "arbitrary")),
        name="rwkv_bwd" if reverse else "rwkv_fwd",
    )(p, p, p, p, p, pvec, w2pad, a2pad, g512)


def _hgrn_kernel(q_ref, v_ref, z_ref, lb_ref, o_ref, state_ref, *, n, cs, sub, reverse):
    @pl.when(pl.program_id(1) == 0)
    def _():
        state_ref[...] = jnp.zeros_like(state_ref)

    q, v, z = (ref[...].astype(F32) for ref in (q_ref, v_ref, z_ref))
    log_sig = -_softplus(-z)
    x1 = lb_ref[0:1, :]
    x2 = lb_ref[1:2, :] + log_sig
    logf = jnp.maximum(x1, x2) + jnp.log(1.0 + jnp.exp(-jnp.abs(x1 - x2)))
    kf = 1.0 - jnp.exp(logf)
    nchunk, nsub = n // cs, cs // sub
    last_row = lambda gi: gi * cs if reverse else (gi + 1) * cs - 1
    row1 = lax.broadcasted_iota(jnp.int32, (n, 1), 0)
    b = _scan_cumsum(logf, n, reverse, seg=cs)
    b_last = b[last_row(0):last_row(0) + 1, :]
    for gi in range(1, nchunk):
        b_last = jnp.where(row1 >= gi * cs, b[last_row(gi):last_row(gi) + 1, :], b_last)
    qe = q * jnp.exp(b)
    ke = kf * jnp.exp(b_last - b)
    b2 = b * LOG2_E
    c2 = b2 - jnp.log(kf) * LOG2_E
    srow = lax.broadcasted_iota(jnp.int32, (sub, sub), 0)
    lane = lax.broadcasted_iota(jnp.int32, (sub, sub), 1)
    tri = (srow <= lane) if reverse else (srow >= lane)
    hsl = [slice(h * D_HEAD_K, (h + 1) * D_HEAD_K) for h in range(D_HEADS)]

    intra = {}
    for gi in range(nchunk):
        base = gi * cs
        for h, hs in enumerate(hsl):
            for m in range(nsub):
                ms = slice(base + m * sub, base + (m + 1) * sub)
                q_m, b_m = q[ms, hs], b[ms, hs]
                b2_m, c2_m = b2[ms, hs], c2[ms, hs]
                dmat = jnp.zeros((sub, sub), F32)
                for s in range(sub):
                    w = q_m * jnp.exp2(b2_m - c2_m[s:s + 1])
                    dmat = jnp.where(lane == s, jnp.sum(w, axis=-1, keepdims=True), dmat)
                dmat = jnp.where(tri, dmat, 0.0)
                o_m = _dot(dmat, v[ms, hs])
                es = slice(base + (m + 1) * sub, base + cs) if reverse else slice(base, base + m * sub)
                if es.stop > es.start:
                    first = base + ((m + 1) * sub - 1 if reverse else m * sub)
                    rho = b[first:first + 1, hs] - logf[first:first + 1, hs]
                    qt = q_m * jnp.exp(b_m - rho)
                    kt = kf[es, hs] * jnp.exp(rho - b[es, hs])
                    o_m = o_m + _dot(_dot_nt(qt, kt), v[es, hs])
                intra[gi, h, m] = o_m

    st = [state_ref[h] for h in range(D_HEADS)]
    inter = {}
    for gi in (range(nchunk - 1, -1, -1) if reverse else range(nchunk)):
        rs = slice(gi * cs, (gi + 1) * cs)
        wend = jnp.exp(b[last_row(gi):last_row(gi) + 1, :])
        for h, hs in enumerate(hsl):
            inter[gi, h] = _dot_nt(qe[rs, hs], st[h])
        st = [st[h] * wend[:, hs] + _dot_tn(v[rs, hs], ke[rs, hs]) for h, hs in enumerate(hsl)]
    for (gi, h, m), o_m in intra.items():
        lo = gi * cs + m * sub
        o_ref[lo:lo + sub, hsl[h]] = (o_m + inter[gi, h][m * sub:(m + 1) * sub]).astype(o_ref.dtype)
    for h in range(D_HEADS):
        state_ref[h] = st[h]


def _hgrn(p, lbvec, batch, seq, reverse):
    n = p.shape[0]
    c = HGRN_BLOCK[1 if reverse else 0]
    nc = seq // c
    if reverse:
        rowidx = lambda b, j: b * nc + (nc - 1 - j)
    else:
        rowidx = lambda b, j: b * nc + j
    wide = lambda cb: pl.BlockSpec((c, 512), lambda b, j, cb=cb: (rowidx(b, j), cb))
    return pl.pallas_call(
        functools.partial(_hgrn_kernel, n=c, cs=HGRN_CHUNK, sub=HGRN_SUB, reverse=reverse),
        grid=(batch, nc),
        in_specs=[wide(CB_DQ), wide(CB_DI), wide(CB_DFB if reverse else CB_DFF),
                  pl.BlockSpec((8, 512), lambda b, j: (0, 0))],
        out_specs=pl.BlockSpec((c, 512), lambda b, j: (rowidx(b, j), 0)),
        out_shape=jax.ShapeDtypeStruct((n, 512), BF16),
        scratch_shapes=[pltpu.VMEM((D_HEADS, D_HEAD_K, D_HEAD_K), F32)],
        compiler_params=_cparams(("parallel", "arbitrary")),
        name="hgrn_bwd" if reverse else "hgrn_fwd",
    )(p, p, p, lbvec)


def _mixpost_kernel(o1_ref, l1_ref, o2_ref, l2_ref, o3_ref, l3_ref,
                    y0_ref, y1_ref, bo0_ref, bo1_ref, g1_ref, oc_ref,
                    h0_ref, h1_ref, dg_ref, pv_ref, g_ref, g2_ref, mix_ref,
                    o2s_ref, l2s_ref, o3s_ref, l3s_ref, *, tm):
    for d, src, dst in ((4, o2_ref, o2s_ref), (16, o3_ref, o3s_ref)):
        for r in range(d):
            for c in range(4):
                lo = r * 512 + c * 128
                dst[c, pl.ds(r, tm // d, stride=d), :] = src[:, lo:lo + 128].astype(F32)
    for d, src, dst in ((4, l2_ref, l2s_ref), (16, l3_ref, l3s_ref)):
        for r in range(d):
            dst[0, pl.ds(r, tm // d, stride=d), :] = src[:, r * LSE_WIDTH:(r + 1) * LSE_WIDTH]
    wide = lambda ref: jnp.concatenate([ref[c] for c in range(4)], axis=1)
    l1, l2, l3 = l1_ref[...], l2s_ref[0], l3s_ref[0]
    mx = jnp.maximum(jnp.maximum(l1, l2), l3)
    w1, w2, w3 = jnp.exp(l1 - mx), jnp.exp(l2 - mx), jnp.exp(l3 - mx)
    inv = 1.0 / (w1 + w2 + w3)
    src_lane = lax.broadcasted_iota(jnp.int32, (LSE_WIDTH, GROUP_WIDTH), 0)
    dst_head = lax.broadcasted_iota(jnp.int32, (LSE_WIDTH, GROUP_WIDTH), 1) // HEAD_DIM
    spread = (src_lane == dst_head * (LSE_WIDTH * HEAD_DIM // GROUP_WIDTH)).astype(BF16)
    e1, e2, e3 = (_seg_sum(w * inv, spread) for w in (w1, w2, w3))
    out_a = o1_ref[...] * e1 + wide(o2s_ref) * e2 + wide(o3s_ref) * e3
    mix_ref[:, 0:512] = out_a.astype(BF16)
    g = g_ref[...]
    y = y0_ref[...] + y1_ref[...]
    mu = _seg_sum(y, g) * (1.0 / HEAD_DIM)
    yc = y - mu
    var = _seg_sum(yc * yc, g) * (1.0 / HEAD_DIM)
    yn = yc * lax.rsqrt(var + LNX_EPS) * pv_ref[0:1, :] + pv_ref[1:2, :]
    gate = jnp.dot(jax.nn.sigmoid(g1_ref[...].astype(F32)).astype(BF16), g2_ref[...], preferred_element_type=F32)
    mix_ref[:, 512:1024] = ((yn + bo0_ref[...] + bo1_ref[...]) * gate).astype(BF16)
    mix_ref[:, 1024:1536] = oc_ref[...]
    o = h0_ref[...].astype(F32) + h1_ref[...].astype(F32)
    dg = dg_ref[...].astype(F32)
    silu = dg * jax.nn.sigmoid(dg)
    for h in range(D_HEADS):
        hs = slice(h * D_HEAD_K, (h + 1) * D_HEAD_K)
        oh = o[:, hs]
        ms = jnp.mean(oh * oh, axis=-1, keepdims=True)
        res = oh * lax.rsqrt(ms + NORM_EPS) * pv_ref[2:3, hs] * silu[:, hs]
        mix_ref[:, 1536 + h * D_HEAD_K:1536 + (h + 1) * D_HEAD_K] = res.astype(BF16)


def _mixpost(a_parts, yb, bonus, p, oc, od, pvec, g512, g2pad, tm=256):
    n = p.shape[0]
    blk = pl.BlockSpec((tm, 512), lambda i: (i, 0))
    const = lambda shape: pl.BlockSpec(shape, lambda i: (0,) * len(shape))
    args = []
    for o, l in a_parts:
        args += [o, l]
    args += [yb[0], yb[1], bonus[0], bonus[1], p, oc, od[0], od[1], p, pvec, g512, g2pad]
    dil = lambda d, w=512: pl.BlockSpec((tm // d, d * w), lambda i: (i, 0))
    in_specs = [blk, dil(1, LSE_WIDTH), dil(4), dil(4, LSE_WIDTH), dil(16), dil(16, LSE_WIDTH)] + [blk] * 4 + [
        pl.BlockSpec((tm, 128), lambda i: (i, CB_G1)), blk, blk, blk,
        pl.BlockSpec((tm, 512), lambda i: (i, CB_DG)),
        const((8, 512)), const((512, 512)), const((128, 512))]
    return pl.pallas_call(
        functools.partial(_mixpost_kernel, tm=tm),
        grid=(n // tm,),
        in_specs=in_specs,
        out_specs=pl.BlockSpec((tm, D_MODEL), lambda i: (i, 0)),
        out_shape=jax.ShapeDtypeStruct((n, D_MODEL), BF16),
        scratch_shapes=[pltpu.VMEM((4, tm, 128), F32), pltpu.VMEM((1, tm, LSE_WIDTH), F32)] * 2,
        compiler_params=_cparams(("parallel",)),
        name="mixpost",
    )(*args)


def _outproj_kernel(mix_ref, w_ref, x_ref, o_ref):
    o_ref[...] = x_ref[...] + jnp.dot(mix_ref[...], w_ref[...], preferred_element_type=F32)


def _outproj(mix, w, x2, tm=512, tn=2048):
    n, d = x2.shape
    return pl.pallas_call(
        _outproj_kernel,
        grid=(n // tm, d // tn),
        in_specs=[pl.BlockSpec((tm, mix.shape[1]), lambda i, j: (i, 0)),
                  pl.BlockSpec((mix.shape[1], tn), lambda i, j: (0, j)),
                  pl.BlockSpec((tm, tn), lambda i, j: (i, j))],
        out_specs=pl.BlockSpec((tm, tn), lambda i, j: (i, j)),
        out_shape=jax.ShapeDtypeStruct((n, d), F32),
        compiler_params=_cparams(("parallel", "arbitrary")),
        name="outproj",
    )(mix, w, x2)


def _rope_tables(seq):
    half = ROT_DIM // 2
    inv = ROPE_THETA ** (-jnp.arange(0, ROT_DIM, 2, dtype=F32) / ROT_DIM)
    ang = jnp.arange(seq, dtype=F32)[:, None] * inv[None]
    cos, sin = jnp.cos(ang), jnp.sin(ang)
    ones = jnp.ones((seq, HEAD_DIM - ROT_DIM), F32)
    zeros = jnp.zeros((seq, HEAD_DIM - ROT_DIM), F32)
    z8 = jnp.zeros((seq, half), F32)
    c = jnp.concatenate([cos, cos, ones], axis=1)
    s1 = jnp.concatenate([-sin, z8, zeros], axis=1)
    s2 = jnp.concatenate([z8, sin, zeros], axis=1)
    tile2 = lambda t: jnp.concatenate([t, t], axis=1)
    return tile2(c), tile2(s1), tile2(s2)


def _block_diag_ones(width, seg):
    idx = jnp.arange(width) // seg
    return (idx[:, None] == idx[None, :]).astype(BF16)


def _pad_rows(rows, width=512, total=8):
    rows = [jnp.pad(r.astype(F32).reshape(-1), (0, width - r.size)) for r in rows]
    rows += [jnp.zeros((width,), F32)] * (total - len(rows))
    return jnp.stack(rows)


def _w_ext(w_in, mu_wa, w1, a1, g1):
    cols_a, cols_b = [], []
    for d in range(2):
        for mu, w in ((mu_wa[d, 0], w1[d]), (mu_wa[d, 1], a1[d])):
            cols_a.append((1.0 - mu)[:, None] * w)
            cols_b.append(mu[:, None] * w)
    parts = [w_in[:, 0:3584], w_in[:, 3840:6400], w_in[:, 3584:3840]] + cols_a + cols_b
    parts.append(jnp.pad(g1, ((0, 0), (0, 128 - g1.shape[1]))))
    w = jnp.concatenate(parts, axis=1)
    return jnp.pad(w, ((0, 0), (0, D_IN_EXT - w.shape[1]))).astype(BF16)


def _pad_lowrank(w, row0):
    return jnp.pad(w, ((row0, 128 - row0 - w.shape[0]), (0, 0))).astype(BF16)


def kernel(x, ln_gain, w_in, w_out, w_ffn_in, w_ffn_out, qk_gain, sink, b_mu_rkv, b_mu_wa, b_w0, b_w1,
           b_w2, b_a0, b_a1, b_a2, b_k_k, b_k_a, b_r_k, b_lnx_gain, b_lnx_bias, b_g1, b_g2,
           d_lb_logits, d_norm_gain):
    batch, seq, d = x.shape
    x2 = x.reshape(batch * seq, d)
    rope_c, rope_s1, rope_s2 = _rope_tables(seq)
    g512 = _block_diag_ones(512, HEAD_DIM)
    pr = jax.nn.softmax(d_lb_logits.astype(F32), axis=0)
    cs = jnp.cumsum(pr, axis=0)
    lb_all = cs - cs[0:1]

    ffn_in, ffn_out = w_ffn_in.astype(BF16), w_ffn_out.astype(BF16)

    for l in range(DEPTH):
        x2 = _ffn(x2, ln_gain[l, 0][None], ffn_in, ffn_out, l, 0)

        p = _inproj(x2, ln_gain[l, 1][None], _w_ext(w_in[l], b_mu_wa[l], b_w1[l], b_a1[l], b_g1[l]))

        tile8 = lambda t: jnp.tile(t, 8)
        gains = _pad_rows([tile8(qk_gain[l, 0, 0]), tile8(qk_gain[l, 0, 1]),
                           tile8(qk_gain[l, 1, 0]), jnp.tile(qk_gain[l, 1, 1], 2)])
        prepped = _prep(p, rope_c, rope_s1, rope_s2, gains, g512, seq)
        qc, kc, vc = prepped[9:]

        a_parts = []
        for pi, (window, dil) in enumerate(DILATED_PATTERNS):
            qa, ka, va = prepped[3 * pi:3 * pi + 3]
            a_parts.append(_band_attention(qa, ka, va, None, batch, seq, dil, window // (2 * dil),
                                           GROUP_WIDTH // HEAD_DIM, 1))
        oc = _band_attention(qc, kc, vc, sink[l][None].astype(F32), batch, seq, 1, C_HALF_WINDOW,
                             C_KV_HEADS, C_Q_HEADS // C_KV_HEADS)

        yb, bonus = [], []
        for dr in range(2):
            pvec = _pad_rows([b_mu_rkv[l, dr, 0], b_mu_rkv[l, dr, 1], b_mu_rkv[l, dr, 2], b_w0[l, dr],
                              b_a0[l, dr], b_k_k[l], b_k_a[l], b_r_k[l]])
            y, bo = _rwkv(p, pvec, _pad_lowrank(b_w2[l, dr], 64 * dr), _pad_lowrank(b_a2[l, dr], 64 * dr + 32),
                          g512, batch, seq, reverse=(dr == 1))
            yb.append(y)
            bonus.append(bo)

        od = []
        for dr in range(2):
            lb = lb_all[l, dr]
            od.append(_hgrn(p, _pad_rows([jnp.log(lb), jnp.log1p(-lb)]), batch, seq, reverse=(dr == 1)))

        pvec = _pad_rows([b_lnx_gain[l], b_lnx_bias[l], d_norm_gain[l]])
        mix = _mixpost(a_parts, yb, bonus, p, oc, od, pvec, g512, _pad_lowrank(b_g2[l], 0))
        x2 = _outproj(mix, w_out[l].astype(BF16), x2)

        x2 = _ffn(x2, ln_gain[l, 2][None], ffn_in, ffn_out, l, 1)
    return x2.reshape(batch, seq, d)
```
